```python
import math
import jax, jax.numpy as jnp
from jax import lax
import numpy as np

D_MODEL = 1024
BATCH = 8
SEQ = 2048
DEPTH = 2
DEC_BATCH = 128
DEC_SEQ = 1
PAST_LEN = 16384
PAGE_SIZE = 128

F32 = jnp.float32
EPS = 1e-6
CHUNK = 64
H_GLA = 6
DK_GLA = 32
DV_GLA = 64
GLA_RANK = 16
GLA_TAU = 16.0
H_DN = 6
DK_DN = 64
DV_DN = 64
CONV_W = 4
CONV_DIM = H_DN * (2 * DK_DN + DV_DN)
H_HG = 4
DK_HG = 64
DV_HG = 64
MIX = H_GLA * DV_GLA + H_DN * DV_DN + H_HG * DV_HG
IN_SIZES = (H_GLA * DK_GLA, H_GLA * DK_GLA, H_GLA * DV_GLA, GLA_RANK, H_GLA * DV_GLA,
            CONV_DIM, H_DN, H_DN, H_DN * DV_DN,
            H_HG * DK_HG, H_HG * DK_HG, H_HG * DV_HG, H_HG * DV_HG)
IN_COLS = 2 * H_GLA * DK_GLA + 2 * H_GLA * DV_GLA + GLA_RANK + CONV_DIM + 2 * H_DN + H_DN * DV_DN + 2 * H_HG * DK_HG + 2 * H_HG * DV_HG
N_GROUPS = 4
EXP_PER_GROUP = 8
N_EXPERTS = N_GROUPS * EXP_PER_GROUP
TOP_K = 2
D_FF_EXPERT = 256

kernel_name = 'hybrid_gla_gdn_hgrn2_hmoe_step'


def _rms_norm(x, g):
    xf = x.astype(F32)
    y = xf * lax.rsqrt(jnp.mean(xf * xf, axis=-1, keepdims=True) + EPS)
    return (y * g.astype(F32)).astype(x.dtype)


def _l2norm(x):
    return x * lax.rsqrt(jnp.sum(x * x, axis=-1, keepdims=True) + EPS)


def _heads(t, h):
    b, l, _ = t.shape
    return t.reshape(b, l, h, -1).transpose(0, 2, 1, 3)


def _unheads(t):
    b, h, l, d = t.shape
    return t.transpose(0, 2, 1, 3).reshape(b, l, h * d)


def _to_chunks(t, c, n):
    b, h, l = t.shape[:3]
    t = jnp.pad(t, [(0, 0), (0, 0), (0, n * c - l)] + [(0, 0)] * (t.ndim - 3))
    t = t.reshape((b, h, n, c) + t.shape[3:])
    return jnp.moveaxis(t, 2, 0)


def _from_chunks(o, l):
    n, b, h, c, d = o.shape
    return jnp.moveaxis(o, 0, 2).reshape(b, h, n * c, d)[:, :, :l]


def _chunk_gla(q, k, v, log_a, s0):
    l = q.shape[2]
    c = min(CHUNK, l)
    n = -(-l // c)
    mask = jnp.tril(jnp.ones((c, c), bool))
    xs = tuple(_to_chunks(t, c, n) for t in (q, k, v, log_a))

    def step(s, inp):
        qc, kc, vc, ac = inp
        bcum = jnp.cumsum(ac, axis=-2)
        o_inter = jnp.einsum('bhcd,bhde->bhce', qc * jnp.exp(bcum), s)
        diff = bcum[..., :, None, :] - bcum[..., None, :, :]
        m3 = mask[:, :, None]
        decay = jnp.where(m3, jnp.exp(jnp.where(m3, diff, 0.0)), 0.0)
        att = jnp.einsum('bhid,bhjd,bhijd->bhij', qc, kc, decay)
        o = o_inter + jnp.einsum('bhij,bhje->bhie', att, vc)
        b_last = bcum[..., -1:, :]
        s = s * jnp.exp(b_last)[..., 0, :, None] + jnp.einsum('bhcd,bhce->bhde', kc * jnp.exp(b_last - bcum), vc)
        return s, o

    s, o = lax.scan(step, s0, xs)
    return _from_chunks(o, l), s


def _chunk_gated_delta(q, k, v, g, beta, s0):
    l = q.shape[2]
    c = min(CHUNK, l)
    n = -(-l // c)
    mask = jnp.tril(jnp.ones((c, c), bool))
    strict = jnp.tril(jnp.ones((c, c), bool), -1)
    eye = jnp.eye(c, dtype=F32)
    xs = tuple(_to_chunks(t, c, n) for t in (q, k, v, g, beta))

    def step(s, inp):
        qc, kc, vc, gc, bc = inp
        gcum = jnp.cumsum(gc, axis=-1)
        gdiff = gcum[..., :, None] - gcum[..., None, :]
        gam = jnp.where(mask, jnp.exp(jnp.where(mask, gdiff, 0.0)), 0.0)
        kb = kc * bc[..., None]
        vb = vc * bc[..., None]
        m = jnp.where(strict, jnp.einsum('bhid,bhjd->bhij', kb, kc) * gam, 0.0)
        tmat = eye + m
        u = lax.linalg.triangular_solve(tmat, vb, left_side=True, lower=True, unit_diagonal=True)
        w = lax.linalg.triangular_solve(tmat, kb * jnp.exp(gcum)[..., None], left_side=True, lower=True, unit_diagonal=True)
        v_new = u - jnp.einsum('bhcd,bhde->bhce', w, s)
        att = jnp.einsum('bhid,bhjd->bhij', qc, kc) * gam
        o = jnp.einsum('bhcd,bhde->bhce', qc * jnp.exp(gcum)[..., None], s) + jnp.einsum('bhij,bhje->bhie', att, v_new)
        g_last = gcum[..., -1:]
        s = s * jnp.exp(g_last)[..., None] + jnp.einsum('bhcd,bhce->bhde', kc * jnp.exp(g_last - gcum)[..., None], v_new)
        return s, o

    s, o = lax.scan(step, s0, xs)
    return _from_chunks(o, l), s


def _gla_group(q, k, v, a_lr, g, wa2, ba, norm_g, s0):
    q = _heads(q, H_GLA).astype(F32) * DK_GLA ** -0.5
    k = _heads(k, H_GLA).astype(F32)
    v = _heads(v, H_GLA).astype(F32)
    log_a = jax.nn.log_sigmoid((a_lr @ wa2 + ba).astype(F32)) / GLA_TAU
    o, s = _chunk_gla(q, k, v, _heads(log_a, H_GLA), s0.astype(F32))
    o = _rms_norm(o, norm_g) * jax.nn.silu(_heads(g, H_GLA).astype(F32))
    return _unheads(o).astype(g.dtype), s


def _deltanet_group(qkv, a, b, g, conv_w, a_log, dt_bias, norm_g, s0, buf):
    l = qkv.shape[1]
    ext = jnp.concatenate([buf.astype(qkv.dtype), qkv], axis=1)
    conv = sum(ext[:, i:i + l] * conv_w[i] for i in range(CONV_W))
    new_buf = ext[:, l:]
    qkv_c = jax.nn.silu(conv.astype(F32))
    q, k, v = jnp.split(qkv_c, [H_DN * DK_DN, 2 * H_DN * DK_DN], axis=-1)
    q = _l2norm(_heads(q, H_DN)) * DK_DN ** -0.5
    k = _l2norm(_heads(k, H_DN))
    v = _heads(v, H_DN)
    beta = jax.nn.sigmoid(b.astype(F32)).transpose(0, 2, 1)
    gdec = (-jnp.exp(a_log.astype(F32)) * jax.nn.softplus(a.astype(F32) + dt_bias.astype(F32))).transpose(0, 2, 1)
    o, s = _chunk_gated_delta(q, k, v, gdec, beta, s0.astype(F32))
    o = _rms_norm(o, norm_g) * jax.nn.silu(_heads(g, H_DN).astype(F32))
    return _unheads(o).astype(g.dtype), s, new_buf


def _hgrn2_group(q, f_pre, i, g, lb, norm_g, s0):
    lbf = lb.astype(F32)
    x = f_pre.astype(F32)
    log_f = jax.nn.log_sigmoid(x) + jnp.log1p(lbf * jnp.exp(-x))
    k = (1.0 - lbf) * jax.nn.sigmoid(-x)
    q = jax.nn.silu(q.astype(F32))
    o, s = _chunk_gla(_heads(q, H_HG), _heads(k, H_HG), _heads(i.astype(F32), H_HG), _heads(log_f, H_HG), s0.astype(F32))
    o = _rms_norm(o * jax.nn.sigmoid(_heads(g, H_HG).astype(F32)), norm_g)
    return _unheads(o).astype(g.dtype), s


def _hier_moe(x, rg_w, rg_b, re_w, re_b, w1, w3, w2):
    t = x.shape[0]
    pg = jax.nn.softmax((x @ rg_w + rg_b).astype(F32), axis=-1)
    pg_top, g_idx = lax.top_k(pg, 1)
    le = (x @ re_w + re_b).astype(F32).reshape(t, N_GROUPS, EXP_PER_GROUP)
    le_sel = jnp.take_along_axis(le, g_idx[:, :, None], axis=1)[:, 0]
    pe = jax.nn.softmax(le_sel, axis=-1)
    v2, i2 = lax.top_k(pe, TOP_K)
    v2 = v2 / jnp.sum(v2, axis=-1, keepdims=True)
    w_in_grp = jnp.sum(jax.nn.one_hot(i2, EXP_PER_GROUP, dtype=F32) * v2[..., None], axis=-2)
    gate = jax.nn.one_hot(g_idx[:, 0], N_GROUPS, dtype=F32)[:, :, None] * (pg_top * w_in_grp)[:, None, :]
    y = jnp.zeros_like(x)
    for gi in range(N_GROUPS):
        e0 = gi * EXP_PER_GROUP
        e1 = e0 + EXP_PER_GROUP
        h = jax.nn.silu(jnp.einsum('td,edf->tef', x, w1[e0:e1])) * jnp.einsum('td,edf->tef', x, w3[e0:e1])
        h = h * gate[:, gi, :, None].astype(x.dtype)
        y = y + jnp.einsum('tef,efd->td', h, w2[e0:e1])
    return y


def _trunk(x, s_gla, s_dn, s_conv, s_hg,
           norm1_g, w_in, gla_wa2, gla_ba, gla_norm_g, dn_conv_w, dn_a_log, dn_dt_bias, dn_norm_g,
           hg_lb_logits, hg_norm_g, w_out, norm2_g, router_g_w, router_g_b, router_e_w, router_e_b,
           exp_w1, exp_w3, exp_w2, final_norm_g):
    b, l, _ = x.shape
    offs = np.cumsum(IN_SIZES)[:-1].tolist()
    sm = jax.nn.softmax(hg_lb_logits.astype(F32), axis=0)
    lb_all = jnp.maximum(jnp.cumsum(sm, axis=0) - sm[0:1], 0.0)
    out_gla, out_dn, out_conv, out_hg = [], [], [], []
    for li in range(DEPTH):
        h = _rms_norm(x, norm1_g[li])
        (g_q, g_k, g_v, g_a, g_g, d_qkv, d_a, d_b, d_g, h_q, h_f, h_i, h_g) = jnp.split(h @ w_in[li], offs, axis=-1)
        o_a, sa = _gla_group(g_q, g_k, g_v, g_a, g_g, gla_wa2[li], gla_ba[li], gla_norm_g[li], s_gla[li])
        o_b, sb, buf = _deltanet_group(d_qkv, d_a, d_b, d_g, dn_conv_w[li], dn_a_log[li], dn_dt_bias[li],
                                       dn_norm_g[li], s_dn[li], s_conv[li])
        o_c, sc = _hgrn2_group(h_q, h_f, h_i, h_g, lb_all[li], hg_norm_g[li], s_hg[li])
        x = x + jnp.concatenate([o_a, o_b, o_c], axis=-1) @ w_out[li]
        h2 = _rms_norm(x, norm2_g[li]).reshape(b * l, D_MODEL)
        x = x + _hier_moe(h2, router_g_w[li], router_g_b[li], router_e_w[li], router_e_b[li],
                          exp_w1[li], exp_w3[li], exp_w2[li]).reshape(b, l, D_MODEL)
        out_gla.append(sa)
        out_dn.append(sb)
        out_conv.append(buf)
        out_hg.append(sc)
    y = _rms_norm(x, final_norm_g)
    return (y, jnp.stack(out_gla).astype(s_gla.dtype), jnp.stack(out_dn).astype(s_dn.dtype),
            jnp.stack(out_conv).astype(s_conv.dtype), jnp.stack(out_hg).astype(s_hg.dtype))


def setup_inputs(seed: int = 0) -> dict:
    key = jax.random.key(seed)
    ks = iter(jax.random.split(key, 40))

    def nrm(shape, s):
        return jax.random.normal(next(ks), shape, F32) * s

    dt = jnp.exp(jax.random.uniform(next(ks), (DEPTH, H_DN), F32, math.log(1e-3), math.log(1e-1)))
    return {
        'x_prompt': nrm((BATCH, SEQ, D_MODEL), 1.0),
        'x_sample': nrm((DEC_BATCH, DEC_SEQ, D_MODEL), 1.0),
        'state_gla': nrm((DEPTH, DEC_BATCH, H_GLA, DK_GLA, DV_GLA), 0.5),
        'state_dn': nrm((DEPTH, DEC_BATCH, H_DN, DK_DN, DV_DN), 0.3),
        'state_conv': nrm((DEPTH, DEC_BATCH, CONV_W - 1, CONV_DIM), 1.0),
        'state_hgrn': nrm((DEPTH, DEC_BATCH, H_HG, DK_HG, DV_HG), 0.5),
        'norm1_g': 1.0 + nrm((DEPTH, D_MODEL), 0.01),
        'w_in': nrm((DEPTH, D_MODEL, IN_COLS), D_MODEL ** -0.5),
        'gla_wa2': nrm((DEPTH, GLA_RANK, H_GLA * DK_GLA), GLA_RANK ** -0.5),
        'gla_ba': nrm((DEPTH, H_GLA * DK_GLA), 0.1),
        'gla_norm_g': 1.0 + nrm((DEPTH, DV_GLA), 0.01),
        'dn_conv_w': nrm((DEPTH, CONV_W, CONV_DIM), CONV_W ** -0.5),
        'dn_a_log': jnp.log(jax.random.uniform(next(ks), (DEPTH, H_DN), F32, 1.0, 16.0)),
        'dn_dt_bias': dt + jnp.log(-jnp.expm1(-dt)),
        'dn_norm_g': 1.0 + nrm((DEPTH, DV_DN), 0.01),
        'hg_lb_logits': nrm((DEPTH, H_HG * DK_HG), 0.1),
        'hg_norm_g': 1.0 + nrm((DEPTH, DV_HG), 0.01),
        'w_out': nrm((DEPTH, MIX, D_MODEL), MIX ** -0.5),
        'norm2_g': 1.0 + nrm((DEPTH, D_MODEL), 0.01),
        'router_g_w': nrm((DEPTH, D_MODEL, N_GROUPS), D_MODEL ** -0.5),
        'router_g_b': nrm((DEPTH, N_GROUPS), 0.01),
        'router_e_w': nrm((DEPTH, D_MODEL, N_EXPERTS), D_MODEL ** -0.5),
        'router_e_b': nrm((DEPTH, N_EXPERTS), 0.01),
        'exp_w1': nrm((DEPTH, N_EXPERTS, D_MODEL, D_FF_EXPERT), D_MODEL ** -0.5),
        'exp_w3': nrm((DEPTH, N_EXPERTS, D_MODEL, D_FF_EXPERT), D_MODEL ** -0.5),
        'exp_w2': nrm((DEPTH, N_EXPERTS, D_FF_EXPERT, D_MODEL), D_FF_EXPERT ** -0.5),
        'final_norm_g': 1.0 + nrm((D_MODEL,), 0.01),
    }


def reference(x_prompt, x_sample, state_gla, state_dn, state_conv, state_hgrn,
              norm1_g, w_in, gla_wa2, gla_ba, gla_norm_g, dn_conv_w, dn_a_log, dn_dt_bias, dn_norm_g,
              hg_lb_logits, hg_norm_g, w_out, norm2_g, router_g_w, router_g_b, router_e_w, router_e_b,
              exp_w1, exp_w3, exp_w2, final_norm_g):
    weights = (norm1_g, w_in, gla_wa2, gla_ba, gla_norm_g, dn_conv_w, dn_a_log, dn_dt_bias, dn_norm_g,
               hg_lb_logits, hg_norm_g, w_out, norm2_g, router_g_w, router_g_b, router_e_w, router_e_b,
               exp_w1, exp_w3, exp_w2, final_norm_g)
    bp = x_prompt.shape[0]
    z_gla = jnp.zeros((DEPTH, bp) + state_gla.shape[2:], state_gla.dtype)
    z_dn = jnp.zeros((DEPTH, bp) + state_dn.shape[2:], state_dn.dtype)
    z_conv = jnp.zeros((DEPTH, bp) + state_conv.shape[2:], state_conv.dtype)
    z_hg = jnp.zeros((DEPTH, bp) + state_hgrn.shape[2:], state_hgrn.dtype)
    y_prompt, gla_p, dn_p, conv_p, hg_p = _trunk(x_prompt, z_gla, z_dn, z_conv, z_hg, *weights)
    y_sample, gla_s, dn_s, conv_s, hg_s = _trunk(x_sample, state_gla, state_dn, state_conv, state_hgrn, *weights)
    return (y_prompt, y_sample, gla_p, dn_p, conv_p, hg_p, gla_s, dn_s, conv_s, hg_s)
```

```python
import functools
import math

import numpy as np
import jax
import jax.numpy as jnp
from jax import lax
from jax.experimental import pallas as pl
from jax.experimental.pallas import tpu as pltpu

F32 = jnp.float32
BF16 = jnp.bfloat16
EPS = 1e-6

H_GLA, DK_GLA, DV_GLA, GLA_RANK, GLA_TAU = 6, 32, 64, 16, 16.0
H_DN, DK_DN, DV_DN, CONV_W = 6, 64, 64, 4
CONV_DIM = H_DN * (2 * DK_DN + DV_DN)
H_HG, DK_HG, DV_HG = 4, 64, 64
N_GROUPS, EXP_PER_GROUP, TOP_K = 4, 8, 2
N_EXPERTS = N_GROUPS * EXP_PER_GROUP

LANE = 128
CHUNK = 64
VMEM_LIMIT = 56 * 1024 * 1024

GLA_W = 1280
DN_W = 1664
HG_W = 1024
DN_AB = 1536


def _cparams(sem):
    return pltpu.CompilerParams(dimension_semantics=sem, vmem_limit_bytes=VMEM_LIMIT)


def _dot(a, b):
    return jnp.dot(a, b, preferred_element_type=F32)


def _dot_nt(a, b):
    return lax.dot_general(a, b, (((1,), (1,)), ((), ())), preferred_element_type=F32)


def _dot_tn(a, b):
    return lax.dot_general(a, b, (((0,), (0,)), ((), ())), preferred_element_type=F32)


def _split3(x):
    hi = x.astype(BF16)
    r = x - hi.astype(F32)
    mid = r.astype(BF16)
    lo = (r - mid.astype(F32)).astype(BF16)
    return hi, mid, lo


def _split2(x):
    hi = x.astype(BF16)
    lo = (x - hi.astype(F32)).astype(BF16)
    return hi, lo


def _dot_sel_l(m, x):
    hi, mid, lo = _split3(x)
    return _dot(m, hi) + _dot(m, mid) + _dot(m, lo)


def _dot_sel_r(x, m):
    hi, mid, lo = _split3(x)
    return _dot(hi, m) + _dot(mid, m) + _dot(lo, m)


def _dot_hp(a, b, fn=_dot):
    ah, al = _split2(a)
    bh, bl = _split2(b)
    return fn(ah, bh) + fn(ah, bl) + fn(al, bh)


def _rms(x, g):
    return x * lax.rsqrt(jnp.mean(x * x, axis=-1, keepdims=True) + EPS) * g


def _sigmoid(x):
    return 1.0 / (1.0 + jnp.exp(-x))


def _silu(x):
    return x * _sigmoid(x)


def _log_sigmoid(x):
    return jnp.minimum(x, 0.0) - jnp.log1p(jnp.exp(-jnp.abs(x)))


def _softplus(x):
    return jnp.maximum(x, 0.0) + jnp.log1p(jnp.exp(-jnp.abs(x)))


def _inproj_kernel(x_ref, g_ref, w_ref, oa_ref, ob_ref, oc_ref):
    h = _rms(x_ref[...], g_ref[...]).astype(BF16)
    oa_ref[...] = _dot(h, w_ref[:, 0:GLA_W])
    ob_ref[...] = _dot(h, w_ref[:, GLA_W:GLA_W + DN_W])
    oc_ref[...] = _dot(h, w_ref[:, GLA_W + DN_W:GLA_W + DN_W + HG_W])


def _inproj(x, g, w, tm):
    t, d = x.shape
    n = w.shape[1]
    row = lambda i: (i, 0)
    fix = lambda i: (0, 0)
    return pl.pallas_call(
        _inproj_kernel,
        grid=(t // tm,),
        in_specs=[pl.BlockSpec((tm, d), row), pl.BlockSpec((1, d), fix), pl.BlockSpec((d, n), fix)],
        out_specs=[pl.BlockSpec((tm, GLA_W), row), pl.BlockSpec((tm, DN_W), row), pl.BlockSpec((tm, HG_W), row)],
        out_shape=[jax.ShapeDtypeStruct((t, GLA_W), F32), jax.ShapeDtypeStruct((t, DN_W), F32),
                   jax.ShapeDtypeStruct((t, HG_W), F32)],
        compiler_params=_cparams(("parallel",)),
        name="inproj",
    )(x, g, w)


def _perm_w_in(w):
    sizes = (H_GLA * DK_GLA, H_GLA * DK_GLA, H_GLA * DV_GLA, GLA_RANK, H_GLA * DV_GLA,
             CONV_DIM, H_DN, H_DN, H_DN * DV_DN,
             H_HG * DK_HG, H_HG * DK_HG, H_HG * DV_HG, H_HG * DV_HG)
    offs = np.concatenate([[0], np.cumsum(sizes)])
    seg = [w[:, offs[i]:offs[i + 1]] for i in range(len(sizes))]
    g_q, g_k, g_v, g_a, g_g, d_qkv, d_a, d_b, d_g, h_q, h_f, h_i, h_g = seg
    z = lambda n: jnp.zeros((w.shape[0], n), w.dtype)
    cols = [g_q, g_k, g_v, g_g, g_a, z(GLA_W - 1168),
            d_qkv, d_g, d_a, d_b, z(DN_W - 1548),
            h_q, h_f, h_i, h_g]
    return jnp.concatenate(cols, axis=1).astype(BF16)


def _decay_consts(c):
    n = int(math.log2(c))
    idx = np.arange(c)
    lm = (idx[None, :] <= idx[:, None]).astype(np.float32)
    mats, masks = [], []
    for l in range(1, n + 1):
        hs = c >> l
        bs = 2 * hs
        blk = idx // bs
        ref = blk * bs + hs - 1
        mats.append(lm - lm[ref])
        lower = (idx % bs) >= hs
        same = blk[:, None] == blk[None, :]
        masks.append((same & lower[:, None] & (~lower)[None, :]).astype(np.float32))
    mats.append(lm)
    mats.append(1.0 - lm)
    masks.append(np.eye(c, dtype=np.float32))
    return (jnp.asarray(np.concatenate(mats, 0), BF16), jnp.asarray(np.stack(masks), F32), n)


def _gla_chunk_kernel(mode, nh, dk, dv, nlev, slab_ref, mats_ref, masks_ref, p1_ref, p2_ref, ng_ref,
                      o_ref, st_ref, s_scr):
    c = CHUNK
    ci = pl.program_id(1)

    @pl.when(ci == 0)
    def _():
        s_scr[...] = jnp.zeros_like(s_scr)

    hk, hv = nh * dk, nh * dv
    if mode == "gla":
        q = slab_ref[:, 0:hk] * (dk ** -0.5)
        k = slab_ref[:, hk:2 * hk]
        v = slab_ref[:, 2 * hk:2 * hk + hv]
        gate = slab_ref[:, 2 * hk + hv:2 * hk + 2 * hv]
        a_lr = slab_ref[:, 2 * hk + 2 * hv:2 * hk + 2 * hv + GLA_RANK]
        z = _dot_hp(a_lr, p1_ref[...]) + p2_ref[...]
        la = _log_sigmoid(z) * (1.0 / GLA_TAU)
    else:
        q = _silu(slab_ref[:, 0:hk])
        x = slab_ref[:, hk:2 * hk]
        v = slab_ref[:, 2 * hk:2 * hk + hv]
        gate = slab_ref[:, 2 * hk + hv:2 * hk + 2 * hv]
        lb = p1_ref[...]
        la = _log_sigmoid(x) + jnp.log1p(lb * jnp.exp(-x))
        k = (1.0 - lb) * _sigmoid(-x)

    dall = _dot_sel_l(mats_ref[...], la)
    bcum = dall[nlev * c:(nlev + 1) * c]
    e_b = jnp.exp(bcum)
    e_rem = jnp.exp(dall[(nlev + 1) * c:(nlev + 2) * c])
    e_last = jnp.exp(bcum[c - 1:c])
    wl = [jnp.exp(-jnp.abs(dall[l * c:(l + 1) * c])) for l in range(nlev)]
    qs = [q * w for w in wl] + [q]
    ks = [k * w for w in wl] + [k]
    q_in = q * e_b
    k_rem = k * e_rem
    bf = lambda t: t.astype(BF16)

    outs = []
    for h in range(nh):
        sk = slice(h * dk, (h + 1) * dk)
        sv = slice(h * dv, (h + 1) * dv)
        att = jnp.zeros((c, c), F32)
        for l in range(nlev + 1):
            att = att + _dot_nt(bf(qs[l][:, sk]), bf(ks[l][:, sk])) * masks_ref[l]
        s_t = s_scr[h]
        v_h = bf(v[:, sv])
        o = _dot(bf(att), v_h) + _dot_nt(bf(q_in[:, sk]), bf(s_t))
        s_scr[h] = s_t * e_last[:, sk] + _dot_tn(v_h, bf(k_rem[:, sk]))
        g_h = gate[:, sv]
        if mode == "gla":
            o = _rms(o, ng_ref[...]) * _silu(g_h)
        else:
            o = _rms(o * _sigmoid(g_h), ng_ref[...])
        outs.append(o)
    o_ref[...] = jnp.concatenate(outs, axis=-1).astype(o_ref.dtype)

    @pl.when(ci == pl.num_programs(1) - 1)
    def _():
        st_ref[0] = s_scr[...]


def _gla_chunk(mode, slab, nb, seq, p1, p2, ng):
    nh, dk, dv = (H_GLA, DK_GLA, DV_GLA) if mode == "gla" else (H_HG, DK_HG, DV_HG)
    w = slab.shape[1]
    nc = seq // CHUNK
    mats, masks, nlev = _decay_consts(CHUNK)
    fix2 = lambda b, c: (0, 0)
    kern = functools.partial(_gla_chunk_kernel, mode, nh, dk, dv, nlev)
    o, st = pl.pallas_call(
        kern,
        grid=(nb, nc),
        in_specs=[pl.BlockSpec((CHUNK, w), lambda b, c: (b * nc + c, 0)),
                  pl.BlockSpec(mats.shape, fix2),
                  pl.BlockSpec(masks.shape, lambda b, c: (0, 0, 0)),
                  pl.BlockSpec(p1.shape, fix2), pl.BlockSpec(p2.shape, fix2), pl.BlockSpec(ng.shape, fix2)],
        out_specs=[pl.BlockSpec((CHUNK, nh * dv), lambda b, c: (b * nc + c, 0)),
                   pl.BlockSpec((1, nh, dv, dk), lambda b, c: (b, 0, 0, 0))],
        out_shape=[jax.ShapeDtypeStruct((nb * seq, nh * dv), BF16),
                   jax.ShapeDtypeStruct((nb, nh, dv, dk), F32)],
        scratch_shapes=[pltpu.VMEM((nh, dv, dk), F32)],
        compiler_params=_cparams(("parallel", "arbitrary")),
        name="chunk_" + mode,
    )(slab, mats, masks, p1, p2, ng)
    return o, jnp.swapaxes(st, -1, -2)


def _dn_chunk_kernel(slab_ref, abr_ref, lm_ref, um_ref, cw_ref, pcol_ref, prow_ref, ng_ref,
                     o_ref, st_ref, s_scr, ext_scr):
    c = CHUNK
    nh, dk, dv = H_DN, DK_DN, DV_DN
    ci = pl.program_id(1)

    @pl.when(ci == 0)
    def _():
        s_scr[...] = jnp.zeros_like(s_scr)
        ext_scr[0:8, :] = jnp.zeros((8, CONV_DIM), F32)

    ext_scr[8:8 + c, :] = slab_ref[:, 0:CONV_DIM]
    conv = ext_scr[5:5 + c, :] * cw_ref[0:1, :]
    for i in range(1, CONV_W):
        conv = conv + ext_scr[5 + i:5 + i + c, :] * cw_ref[i:i + 1, :]
    ext_scr[0:8, :] = ext_scr[c:c + 8, :]
    qkv = _silu(conv)
    gate = slab_ref[:, CONV_DIM:DN_AB]

    ab_c = slab_ref[:, DN_AB:DN_AB + LANE]
    g_c = -jnp.exp(pcol_ref[0:1, :]) * _softplus(ab_c + pcol_ref[1:2, :])
    beta_c = _sigmoid(ab_c)
    gcum_c = _dot_sel_l(lm_ref[...], g_c)
    ab_r = abr_ref[0, 0]
    g_r = -jnp.exp(prow_ref[0]) * _softplus(ab_r + prow_ref[1])
    gcum_r = _dot_sel_r(g_r, um_ref[...])

    ri = lax.broadcasted_iota(jnp.int32, (c, c), 0)
    cj = lax.broadcasted_iota(jnp.int32, (c, c), 1)
    tri = ri >= cj
    strict = ri > cj
    eye = (ri == cj).astype(F32)

    outs = []
    for h in range(nh):
        q = qkv[:, h * dk:(h + 1) * dk]
        k = qkv[:, nh * dk + h * dk:nh * dk + (h + 1) * dk]
        v = qkv[:, 2 * nh * dk + h * dv:2 * nh * dk + (h + 1) * dv]
        q = q * lax.rsqrt(jnp.sum(q * q, axis=-1, keepdims=True) + EPS) * (dk ** -0.5)
        k = k * lax.rsqrt(jnp.sum(k * k, axis=-1, keepdims=True) + EPS)
        gc = gcum_c[:, h:h + 1]
        gr = gcum_r[h:h + 1, :]
        beta = beta_c[:, nh + h:nh + h + 1]
        g_last = gcum_c[c - 1:c, h:h + 1]
        gam = jnp.where(tri, jnp.exp(jnp.where(tri, gc - gr, 0.0)), 0.0)
        kb = k * beta
        m = jnp.where(strict, _dot_hp(kb, k, _dot_nt) * gam, 0.0)
        tinv = eye - m
        p = m
        for _ in range(int(math.log2(c)) - 1):
            p = _dot_hp(p, p)
            tinv = tinv + _dot_hp(tinv, p)
        rhs = jnp.concatenate([v * beta, kb * jnp.exp(gc)], axis=-1)
        uw = _dot_hp(tinv, rhs)
        s = s_scr[h]
        sb = s.astype(BF16)
        v_new = uw[:, 0:dv] - _dot(uw[:, dv:dv + dk].astype(BF16), sb)
        att = _dot_nt(q.astype(BF16), k.astype(BF16)) * gam
        vnb = v_new.astype(BF16)
        o = _dot((q * jnp.exp(gc)).astype(BF16), sb) + _dot(att.astype(BF16), vnb)
        s_scr[h] = s * jnp.exp(g_last) + _dot_tn((k * jnp.exp(g_last - gc)).astype(BF16), vnb)
        o = _rms(o, ng_ref[...]) * _silu(gate[:, h * dv:(h + 1) * dv])
        outs.append(o)
    o_ref[...] = jnp.concatenate(outs, axis=-1).astype(o_ref.dtype)

    @pl.when(ci == pl.num_programs(1) - 1)
    def _():
        st_ref[0] = s_scr[...]


def _dn_chunk(slab, ab_rows, nb, seq, conv_w, a_log, dt_bias, ng):
    c = CHUNK
    nc = seq // c
    idx = np.arange(c)
    lm = jnp.asarray((idx[None, :] <= idx[:, None]), BF16)
    um = jnp.asarray((idx[:, None] <= idx[None, :]), BF16)
    cw = jnp.pad(conv_w, ((0, 8 - CONV_W), (0, 0)))
    pad = lambda p: jnp.pad(p, (0, LANE - H_DN))
    pcol = jnp.zeros((8, LANE), F32).at[0].set(pad(a_log)).at[1].set(pad(dt_bias))
    prow = jnp.zeros((2, 16, c), F32)
    prow = prow.at[0, 0:H_DN].set(jnp.broadcast_to(a_log[:, None], (H_DN, c)))
    prow = prow.at[1, 0:H_DN].set(jnp.broadcast_to(dt_bias[:, None], (H_DN, c)))
    fix2 = lambda b, c_: (0, 0)
    o, st = pl.pallas_call(
        _dn_chunk_kernel,
        grid=(nb, nc),
        in_specs=[pl.BlockSpec((c, DN_W), lambda b, c_: (b * nc + c_, 0)),
                  pl.BlockSpec((1, 1, 16, c), lambda b, c_: (b, c_, 0, 0)),
                  pl.BlockSpec((c, c), fix2), pl.BlockSpec((c, c), fix2),
                  pl.BlockSpec((8, CONV_DIM), fix2), pl.BlockSpec((8, LANE), fix2),
                  pl.BlockSpec((2, 16, c), lambda b, c_: (0, 0, 0)), pl.BlockSpec((1, DV_DN), fix2)],
        out_specs=[pl.BlockSpec((c, H_DN * DV_DN), lambda b, c_: (b * nc + c_, 0)),
                   pl.BlockSpec((1, H_DN, DK_DN, DV_DN), lambda b, c_: (b, 0, 0, 0))],
        out_shape=[jax.ShapeDtypeStruct((nb * seq, H_DN * DV_DN), BF16),
                   jax.ShapeDtypeStruct((nb, H_DN, DK_DN, DV_DN), F32)],
        scratch_shapes=[pltpu.VMEM((H_DN, DK_DN, DV_DN), F32), pltpu.VMEM((c + 8, CONV_DIM), F32)],
        compiler_params=_cparams(("parallel", "arbitrary")),
        name="chunk_dn",
    )(slab, ab_rows, lm, um, cw, pcol, prow, ng)
    return o, st


def _decode_gla_kernel(q_ref, k_ref, v_ref, g_ref, alr_ref, wa2t_ref, ba_ref, ng_ref, s_ref, o_ref, so_ref):
    dk = q_ref.shape[1]
    z = _dot_hp(wa2t_ref[0], alr_ref[...]) + ba_ref[0]
    dec = jnp.exp(_log_sigmoid(z) * (1.0 / GLA_TAU))
    q = q_ref[0] * (dk ** -0.5)
    k = k_ref[0]
    v = v_ref[0]
    acc = jnp.zeros_like(v)
    for d in range(dk):
        s_new = s_ref[0, d] * dec[d:d + 1, :] + k[d:d + 1, :] * v
        so_ref[0, d] = s_new
        acc = acc + q[d:d + 1, :] * s_new
    ms = jnp.mean(acc * acc, axis=0, keepdims=True)
    o_ref[0] = acc * lax.rsqrt(ms + EPS) * ng_ref[...] * _silu(g_ref[0])


def _decode_dn_kernel(x_ref, cb_ref, cw_ref, a_ref, b_ref, p_ref, g_ref, ng_ref, s_ref, o_ref, so_ref):
    dk = s_ref.shape[1]
    conv = x_ref[:, 0] * cw_ref[CONV_W - 1, :, 0]
    for i in range(CONV_W - 1):
        conv = conv + cb_ref[i, :, 0] * cw_ref[i, :, 0]
    qkv = _silu(conv)
    q, k, v = qkv[0], qkv[1], qkv[2]
    q = q * lax.rsqrt(jnp.sum(q * q, axis=0, keepdims=True) + EPS) * (dk ** -0.5)
    k = k * lax.rsqrt(jnp.sum(k * k, axis=0, keepdims=True) + EPS)
    eg = jnp.exp(-jnp.exp(p_ref[0, 0:1, :]) * _softplus(a_ref[0] + p_ref[0, 1:2, :]))
    beta = _sigmoid(b_ref[0])
    ks = jnp.zeros_like(v)
    for d in range(dk):
        ks = ks + k[d:d + 1, :] * s_ref[0, d]
    v_new = beta * (v - eg * ks)
    acc = jnp.zeros_like(v)
    for d in range(dk):
        s_new = s_ref[0, d] * eg + k[d:d + 1, :] * v_new
        so_ref[0, d] = s_new
        acc = acc + q[d:d + 1, :] * s_new
    ms = jnp.mean(acc * acc, axis=0, keepdims=True)
    o_ref[0] = acc * lax.rsqrt(ms + EPS) * ng_ref[...] * _silu(g_ref[0])


def _decode_hg_kernel(q_ref, f_ref, v_ref, g_ref, lb_ref, ng_ref, s_ref, o_ref, so_ref):
    dk = q_ref.shape[1]
    x = f_ref[0]
    lb = lb_ref[0]
    f = jnp.exp(_log_sigmoid(x) + jnp.log1p(lb * jnp.exp(-x)))
    k = (1.0 - lb) * _sigmoid(-x)
    q = _silu(q_ref[0])
    v = v_ref[0]
    acc = jnp.zeros_like(v)
    for d in range(dk):
        s_new = s_ref[0, d] * f[d:d + 1, :] + k[d:d + 1, :] * v
        so_ref[0, d] = s_new
        acc = acc + q[d:d + 1, :] * s_new
    acc = acc * _sigmoid(g_ref[0])
    ms = jnp.mean(acc * acc, axis=0, keepdims=True)
    o_ref[0] = acc * lax.rsqrt(ms + EPS) * ng_ref[...]


def _head_call(kern, name, nh, dk, dv, nb, args, specs):
    o, s = pl.pallas_call(
        kern,
        grid=(nh,),
        in_specs=specs + [pl.BlockSpec((1, dk, dv, nb), lambda h: (h, 0, 0, 0))],
        out_specs=[pl.BlockSpec((1, dv, nb), lambda h: (h, 0, 0)),
                   pl.BlockSpec((1, dk, dv, nb), lambda h: (h, 0, 0, 0))],
        out_shape=[jax.ShapeDtypeStruct((nh, dv, nb), F32), jax.ShapeDtypeStruct((nh, dk, dv, nb), F32)],
        compiler_params=_cparams(("parallel",)),
        name=name,
    )(*args)
    return o.reshape(nh * dv, nb), jnp.transpose(s, (3, 0, 1, 2))


def _decode(pa, pb, pc, s_gla, s_dn, s_conv, s_hg, wa2, ba, conv_w, a_log, dt_bias, lb, nga, ngb, ngc):
    nb = pa.shape[0]
    bl = lambda p, *shape: jnp.broadcast_to(p.reshape(shape + (1,)), shape + (nb,))
    tr = lambda s: jnp.transpose(s, (1, 2, 3, 0))
    byh = lambda n: pl.BlockSpec((1, n, nb), lambda h: (h, 0, 0))
    fixed = lambda shape: pl.BlockSpec(shape, lambda h: (0,) * len(shape))

    nh, dk, dv = H_GLA, DK_GLA, DV_GLA
    hk, hv = nh * dk, nh * dv
    pt = pa.T
    args = (pt[0:hk].reshape(nh, dk, nb), pt[hk:2 * hk].reshape(nh, dk, nb),
            pt[2 * hk:2 * hk + hv].reshape(nh, dv, nb), pt[2 * hk + hv:2 * hk + 2 * hv].reshape(nh, dv, nb),
            pt[2 * hk + 2 * hv:2 * hk + 2 * hv + GLA_RANK], wa2.T.reshape(nh, dk, GLA_RANK),
            bl(ba, nh, dk), bl(nga, dv), tr(s_gla))
    specs = [byh(dk), byh(dk), byh(dv), byh(dv), fixed((GLA_RANK, nb)),
             pl.BlockSpec((1, dk, GLA_RANK), lambda h: (h, 0, 0)), byh(dk), fixed((dv, nb))]
    o_a, sa = _head_call(_decode_gla_kernel, "decode_gla", nh, dk, dv, nb, args, specs)

    nh, dk, dv = H_DN, DK_DN, DV_DN
    pt = pb.T
    x = pt[0:CONV_DIM].reshape(3, nh, dk, nb)
    cb = jnp.transpose(s_conv, (1, 2, 0)).reshape(CONV_W - 1, 3, nh, dk, nb)
    cw = bl(conv_w, CONV_W, 3, nh, dk)
    prm = jnp.stack([bl(a_log, nh), bl(dt_bias, nh)], axis=1)
    args = (x, cb, cw, pt[DN_AB:DN_AB + nh].reshape(nh, 1, nb), pt[DN_AB + nh:DN_AB + 2 * nh].reshape(nh, 1, nb),
            prm, pt[CONV_DIM:DN_AB].reshape(nh, dv, nb), bl(ngb, dv), tr(s_dn))
    specs = [pl.BlockSpec((3, 1, dk, nb), lambda h: (0, h, 0, 0)),
             pl.BlockSpec((CONV_W - 1, 3, 1, dk, nb), lambda h: (0, 0, h, 0, 0)),
             pl.BlockSpec((CONV_W, 3, 1, dk, nb), lambda h: (0, 0, h, 0, 0)),
             byh(1), byh(1), byh(2), byh(dv), fixed((dv, nb))]
    o_b, sb = _head_call(_decode_dn_kernel, "decode_dn", nh, dk, dv, nb, args, specs)

    nh, dk, dv = H_HG, DK_HG, DV_HG
    hk = nh * dk
    pt = pc.T
    args = (pt[0:hk].reshape(nh, dk, nb), pt[hk:2 * hk].reshape(nh, dk, nb),
            pt[2 * hk:3 * hk].reshape(nh, dv, nb), pt[3 * hk:4 * hk].reshape(nh, dv, nb),
            bl(lb, nh, dk), bl(ngc, dv), tr(s_hg))
    specs = [byh(dk), byh(dk), byh(dv), byh(dv), byh(dk), fixed((dv, nb))]
    o_c, sc = _head_call(_decode_hg_kernel, "decode_hgrn2", nh, dk, dv, nb, args, specs)

    return jnp.concatenate([o_a, o_b, o_c], axis=0).T.astype(BF16), sa, sb, sc


def _outproj_router_kernel(x_ref, oa_ref, ob_ref, oc_ref, w_ref, g_ref, rw_ref, rb_ref,
                           x1_ref, h2_ref, gate_ref):
    na, nb_ = oa_ref.shape[1], ob_ref.shape[1]
    x1 = (x_ref[...] + _dot(oa_ref[...], w_ref[0:na, :]) + _dot(ob_ref[...], w_ref[na:na + nb_, :])
          + _dot(oc_ref[...], w_ref[na + nb_:, :]))
    x1_ref[...] = x1
    h2 = _rms(x1, g_ref[...]).astype(BF16)
    h2_ref[...] = h2
    logits = _dot(h2, rw_ref[...]) + rb_ref[...]
    lane = lax.broadcasted_iota(jnp.int32, logits.shape, 1)
    neg = jnp.float32(-jnp.inf)
    big = jnp.int32(1 << 20)
    is_g = (lane >= N_EXPERTS) & (lane < N_EXPERTS + N_GROUPS)
    lg = jnp.where(is_g, logits, neg)
    mg = jnp.max(lg, axis=-1, keepdims=True)
    pg_top = 1.0 / jnp.sum(jnp.where(is_g, jnp.exp(lg - mg), 0.0), axis=-1, keepdims=True)
    g_idx = jnp.min(jnp.where(lg == mg, lane, big), axis=-1, keepdims=True) - N_EXPERTS
    in_grp = (lane >= g_idx * EXP_PER_GROUP) & (lane < (g_idx + 1) * EXP_PER_GROUP)
    le = jnp.where(in_grp, logits, neg)
    me = jnp.max(le, axis=-1, keepdims=True)
    ex = jnp.where(in_grp, jnp.exp(le - me), 0.0)
    pe = ex / jnp.sum(ex, axis=-1, keepdims=True)
    pe = jnp.where(in_grp, pe, -1.0)
    v1 = jnp.max(pe, axis=-1, keepdims=True)
    i1 = jnp.min(jnp.where(pe == v1, lane, big), axis=-1, keepdims=True)
    pe2 = jnp.where(lane == i1, -1.0, pe)
    v2 = jnp.max(pe2, axis=-1, keepdims=True)
    i2 = jnp.min(jnp.where(pe2 == v2, lane, big), axis=-1, keepdims=True)
    tot = v1 + v2
    gate = jnp.where(lane == i1, v1 / tot, 0.0) + jnp.where(lane == i2, v2 / tot, 0.0)
    gate_ref[...] = pg_top * gate


def _outproj_router(x, oa, ob, oc, w, g, rw, rb, tm):
    t, d = x.shape
    row = lambda i: (i, 0)
    fix = lambda i: (0, 0)
    return pl.pallas_call(
        _outproj_router_kernel,
        grid=(t // tm,),
        in_specs=[pl.BlockSpec((tm, d), row), pl.BlockSpec((tm, oa.shape[1]), row),
                  pl.BlockSpec((tm, ob.shape[1]), row), pl.BlockSpec((tm, oc.shape[1]), row),
                  pl.BlockSpec(w.shape, fix), pl.BlockSpec((1, d), fix),
                  pl.BlockSpec(rw.shape, fix), pl.BlockSpec((1, LANE), fix)],
        out_specs=[pl.BlockSpec((tm, d), row), pl.BlockSpec((tm, d), row), pl.BlockSpec((tm, LANE), row)],
        out_shape=[jax.ShapeDtypeStruct((t, d), F32), jax.ShapeDtypeStruct((t, d), BF16),
                   jax.ShapeDtypeStruct((t, LANE), F32)],
        compiler_params=_cparams(("parallel",)),
        name="outproj_router",
    )(x, oa, ob, oc, w, g, rw, rb)


def _moe_kernel(final, x1_ref, h2_ref, gate_ref, w1_ref, w3_ref, w2_ref, fg_ref, y_ref):
    e = pl.program_id(1)

    @pl.when(e == 0)
    def _():
        y_ref[...] = x1_ref[...]

    h2 = h2_ref[...]
    lane = lax.broadcasted_iota(jnp.int32, gate_ref.shape, 1)
    ge = jnp.sum(jnp.where(lane == e, gate_ref[...], 0.0), axis=-1, keepdims=True)
    hid = _silu(_dot(h2, w1_ref[0])) * _dot(h2, w3_ref[0]) * ge
    y_ref[...] += _dot(hid.astype(BF16), w2_ref[0])

    if final:
        @pl.when(e == pl.num_programs(1) - 1)
        def _():
            y_ref[...] = _rms(y_ref[...], fg_ref[...])


def _moe(x1, h2, gate, w1, w3, w2, fg, final, tm):
    t, d = x1.shape
    ne, _, f = w1.shape
    row = lambda i, e: (i, 0)
    return pl.pallas_call(
        functools.partial(_moe_kernel, final),
        grid=(t // tm, ne),
        in_specs=[pl.BlockSpec((tm, d), row), pl.BlockSpec((tm, d), row), pl.BlockSpec((tm, LANE), row),
                  pl.BlockSpec((1, d, f), lambda i, e: (e, 0, 0)), pl.BlockSpec((1, d, f), lambda i, e: (e, 0, 0)),
                  pl.BlockSpec((1, f, d), lambda i, e: (e, 0, 0)), pl.BlockSpec((1, d), lambda i, e: (0, 0))],
        out_specs=pl.BlockSpec((tm, d), row),
        out_shape=jax.ShapeDtypeStruct((t, d), F32),
        compiler_params=_cparams(("parallel", "arbitrary")),
        name="moe",
    )(x1, h2, gate, w1, w3, w2, fg)


def kernel(x_prompt, x_sample, state_gla, state_dn, state_conv, state_hgrn, norm1_g, w_in, gla_wa2, gla_ba, gla_norm_g, dn_conv_w, dn_a_log, dn_dt_bias, dn_norm_g, hg_lb_logits, hg_norm_g, w_out, norm2_g, router_g_w, router_g_b, router_e_w, router_e_b, exp_w1, exp_w3, exp_w2, final_norm_g):
    nbp, seq, d = x_prompt.shape
    nbs = x_sample.shape[0]
    depth = w_in.shape[0]
    assert x_sample.shape[1] == 1 and seq % CHUNK == 0
    tp = nbp * seq
    xp = x_prompt.reshape(tp, d)
    xs = x_sample.reshape(nbs, d)

    sm = jax.nn.softmax(hg_lb_logits.astype(F32), axis=0)
    lb_all = jnp.maximum(jnp.cumsum(sm, axis=0) - sm[0:1], 0.0)

    tm_p = min(512, tp)
    tm_moe = min(1024, tp)
    row2 = lambda v: v.reshape(1, -1)
    gla_p, dn_p, conv_p, hg_p, gla_s, dn_s, conv_s, hg_s = ([] for _ in range(8))
    for li in range(depth):
        w_in_b = _perm_w_in(w_in[li])
        w_out_b = w_out[li].astype(BF16)
        rw = jnp.pad(jnp.concatenate([router_e_w[li], router_g_w[li]], axis=1),
                     ((0, 0), (0, LANE - N_EXPERTS - N_GROUPS))).astype(BF16)
        rb = jnp.pad(jnp.concatenate([router_e_b[li], router_g_b[li]]), (0, LANE - N_EXPERTS - N_GROUPS)).reshape(1, LANE)
        w1b, w3b, w2b = exp_w1[li].astype(BF16), exp_w3[li].astype(BF16), exp_w2[li].astype(BF16)
        final = li == depth - 1
        g1, g2, fg = row2(norm1_g[li]), row2(norm2_g[li]), row2(final_norm_g)

        pa, pb, pc = _inproj(xp, g1, w_in_b, tm_p)
        oa, sa = _gla_chunk("gla", pa, nbp, seq, gla_wa2[li], row2(gla_ba[li]), row2(gla_norm_g[li]))
        ab_rows = jnp.pad(pb[:, DN_AB:DN_AB + 2 * H_DN], ((0, 0), (0, 16 - 2 * H_DN)))
        ab_rows = ab_rows.reshape(nbp, seq // CHUNK, CHUNK, 16).transpose(0, 1, 3, 2)
        ob, sb = _dn_chunk(pb, ab_rows, nbp, seq, dn_conv_w[li], dn_a_log[li], dn_dt_bias[li], row2(dn_norm_g[li]))
        oc, sc = _gla_chunk("hgrn2", pc, nbp, seq, row2(lb_all[li]), row2(lb_all[li]), row2(hg_norm_g[li]))
        gla_p.append(sa)
        dn_p.append(sb)
        conv_p.append(pb.reshape(nbp, seq, DN_W)[:, seq - (CONV_W - 1):, 0:CONV_DIM])
        hg_p.append(sc)
        x1, h2, gate = _outproj_router(xp, oa, ob, oc, w_out_b, g2, rw, rb, tm_p)
        xp = _moe(x1, h2, gate, w1b, w3b, w2b, fg, final, tm_moe)

        qa, qb, qc = _inproj(xs, g1, w_in_b, nbs)
        o_s, sa, sb, sc = _decode(qa, qb, qc, state_gla[li], state_dn[li], state_conv[li], state_hgrn[li],
                                  gla_wa2[li], gla_ba[li], dn_conv_w[li], dn_a_log[li], dn_dt_bias[li],
                                  lb_all[li], gla_norm_g[li], dn_norm_g[li], hg_norm_g[li])
        gla_s.append(sa)
        dn_s.append(sb)
        conv_s.append(jnp.concatenate([state_conv[li][:, 1:], qb[:, None, 0:CONV_DIM]], axis=1))
        hg_s.append(sc)
        na, nb_ = H_GLA * DV_GLA, H_DN * DV_DN
        x1, h2, gate = _outproj_router(xs, o_s[:, 0:na], o_s[:, na:na + nb_], o_s[:, na + nb_:], w_out_b, g2, rw, rb, nbs)
        xs = _moe(x1, h2, gate, w1b, w3b, w2b, fg, final, nbs)

    st = lambda xs_, ref: jnp.stack(xs_).astype(ref.dtype)
    return (xp.reshape(nbp, seq, d), xs.reshape(nbs, 1, d),
            st(gla_p, state_gla), st(dn_p, state_dn), st(conv_p, state_conv), st(hg_p, state_hgrn),
            st(gla_s, state_gla), st(dn_s, state_dn), st(conv_s, state_conv), st(hg_s, state_hgrn))
```

```python
import functools
import math

import numpy as np
import jax
import jax.numpy as jnp
from jax import lax
from jax.experimental import pallas as pl
from jax.experimental.pallas import tpu as pltpu

F32 = jnp.float32
BF16 = jnp.bfloat16
EPS = 1e-6

H_GLA, DK_GLA, DV_GLA, GLA_RANK, GLA_TAU = 6, 32, 64, 16, 16.0
H_DN, DK_DN, DV_DN, CONV_W = 6, 64, 64, 4
CONV_DIM = H_DN * (2 * DK_DN + DV_DN)
H_HG, DK_HG, DV_HG = 4, 64, 64
N_GROUPS, EXP_PER_GROUP, TOP_K = 4, 8, 2
N_EXPERTS = N_GROUPS * EXP_PER_GROUP

LANE = 128
CHUNK = 64
GLA_CHUNKS_PER_STEP = 4
DN_CHUNKS_PER_STEP = 2
VMEM_LIMIT = 56 * 1024 * 1024

GLA_W = 1280
DN_W = 1664
HG_W = 1024
DN_AB = 1536


def _cparams(sem):
    return pltpu.CompilerParams(dimension_semantics=sem, vmem_limit_bytes=VMEM_LIMIT)


def _dot(a, b):
    return jnp.dot(a, b, preferred_element_type=F32)


def _dot_nt(a, b):
    return lax.dot_general(a, b, (((1,), (1,)), ((), ())), preferred_element_type=F32)


def _dot_tn(a, b):
    return lax.dot_general(a, b, (((0,), (0,)), ((), ())), preferred_element_type=F32)


def _split3(x):
    hi = x.astype(BF16)
    r = x - hi.astype(F32)
    mid = r.astype(BF16)
    lo = (r - mid.astype(F32)).astype(BF16)
    return hi, mid, lo


def _split2(x):
    hi = x.astype(BF16)
    lo = (x - hi.astype(F32)).astype(BF16)
    return hi, lo


def _dot_sel_l(m, x):
    hi, mid, lo = _split3(x)
    return _dot(m, hi) + _dot(m, mid) + _dot(m, lo)


def _dot_sel_r(x, m):
    hi, mid, lo = _split3(x)
    return _dot(hi, m) + _dot(mid, m) + _dot(lo, m)


def _dot_hp(a, b, fn=_dot):
    ah, al = _split2(a)
    bh, bl = _split2(b)
    return fn(ah, bh) + fn(ah, bl) + fn(al, bh)


def _rms(x, g):
    return x * lax.rsqrt(jnp.mean(x * x, axis=-1, keepdims=True) + EPS) * g


def _sigmoid(x):
    return 1.0 / (1.0 + jnp.exp(-x))


def _silu(x):
    return x * _sigmoid(x)


def _log_sigmoid(x):
    return jnp.minimum(x, 0.0) - jnp.log1p(jnp.exp(-jnp.abs(x)))


def _softplus(x):
    return jnp.maximum(x, 0.0) + jnp.log1p(jnp.exp(-jnp.abs(x)))


def _inproj_kernel(x_ref, g_ref, w_ref, oa_ref, ob_ref, oc_ref):
    h = _rms(x_ref[...], g_ref[...]).astype(BF16)
    oa_ref[...] = _dot(h, w_ref[:, 0:GLA_W])
    ob_ref[...] = _dot(h, w_ref[:, GLA_W:GLA_W + DN_W])
    oc_ref[...] = _dot(h, w_ref[:, GLA_W + DN_W:GLA_W + DN_W + HG_W])


def _inproj(x, g, w, tm):
    t, d = x.shape
    n = w.shape[1]
    row = lambda i: (i, 0)
    fix = lambda i: (0, 0)
    return pl.pallas_call(
        _inproj_kernel,
        grid=(t // tm,),
        in_specs=[pl.BlockSpec((tm, d), row), pl.BlockSpec((1, d), fix), pl.BlockSpec((d, n), fix)],
        out_specs=[pl.BlockSpec((tm, GLA_W), row), pl.BlockSpec((tm, DN_W), row), pl.BlockSpec((tm, HG_W), row)],
        out_shape=[jax.ShapeDtypeStruct((t, GLA_W), F32), jax.ShapeDtypeStruct((t, DN_W), F32),
                   jax.ShapeDtypeStruct((t, HG_W), F32)],
        compiler_params=_cparams(("parallel",)),
        name="inproj",
    )(x, g, w)


def _perm_w_in(w):
    sizes = (H_GLA * DK_GLA, H_GLA * DK_GLA, H_GLA * DV_GLA, GLA_RANK, H_GLA * DV_GLA,
             CONV_DIM, H_DN, H_DN, H_DN * DV_DN,
             H_HG * DK_HG, H_HG * DK_HG, H_HG * DV_HG, H_HG * DV_HG)
    offs = np.concatenate([[0], np.cumsum(sizes)])
    seg = [w[:, offs[i]:offs[i + 1]] for i in range(len(sizes))]
    g_q, g_k, g_v, g_a, g_g, d_qkv, d_a, d_b, d_g, h_q, h_f, h_i, h_g = seg
    z = lambda n: jnp.zeros((w.shape[0], n), w.dtype)
    cols = [g_q, g_k, g_v, g_g, g_a, z(GLA_W - 1168),
            d_qkv, d_g, d_a, d_b, z(DN_W - 1548),
            h_q, h_f, h_i, h_g]
    return jnp.concatenate(cols, axis=1).astype(BF16)


def _decay_consts(c):
    n = int(math.log2(c))
    idx = np.arange(c)
    lm = (idx[None, :] <= idx[:, None]).astype(np.float32)
    mats, masks = [], []
    for l in range(1, n + 1):
        hs = c >> l
        bs = 2 * hs
        blk = idx // bs
        ref = blk * bs + hs - 1
        mats.append(lm - lm[ref])
        lower = (idx % bs) >= hs
        same = blk[:, None] == blk[None, :]
        masks.append((same & lower[:, None] & (~lower)[None, :]).astype(np.float32))
    mats.append(lm)
    mats.append(1.0 - lm)
    masks.append(np.eye(c, dtype=np.float32))
    masks = np.stack(masks)
    return (jnp.asarray(np.concatenate(mats, 0), BF16), jnp.asarray(np.concatenate([masks, masks], axis=1), F32), n)


def _seg_ind(nh, dv):
    h = np.arange(nh * dv) // dv
    return jnp.asarray(h[:, None] == h[None, :], BF16)


def _seg_sum(x, ind):
    hi, lo = _split2(x)
    return _dot(hi, ind) + _dot(lo, ind)


def _pair_geometry(nh, dk):
    geo = []
    for p in range(nh // 2):
        start = (2 * p * dk) // LANE * LANE
        width = min(LANE, nh * dk - start)
        geo.append((start, width, (2 * p * dk - start, (2 * p + 1) * dk - start)))
    return geo


def _gla_chunk_kernel(mode, nh, dk, dv, nlev, nch, slab_ref, mats_ref, masks_ref, ind_ref, p1_ref, p2_ref, ng_ref,
                      o_ref, st_ref, s_scr):
    c = CHUNK
    ci = pl.program_id(1)

    @pl.when(ci == 0)
    def _():
        s_scr[...] = jnp.zeros_like(s_scr)

    hk, hv = nh * dk, nh * dv
    if mode == "gla":
        q = slab_ref[:, 0:hk] * (dk ** -0.5)
        k = slab_ref[:, hk:2 * hk]
        v = slab_ref[:, 2 * hk:2 * hk + hv]
        gate = slab_ref[:, 2 * hk + hv:2 * hk + 2 * hv]
        a_lr = slab_ref[:, 2 * hk + 2 * hv:2 * hk + 2 * hv + GLA_RANK]
        z = _dot_hp(a_lr, p1_ref[...]) + p2_ref[...]
        la = _log_sigmoid(z) * (1.0 / GLA_TAU)
    else:
        q = _silu(slab_ref[:, 0:hk])
        x = slab_ref[:, hk:2 * hk]
        v = slab_ref[:, 2 * hk:2 * hk + hv]
        gate = slab_ref[:, 2 * hk + hv:2 * hk + 2 * hv]
        lb = p1_ref[...]
        la = _log_sigmoid(x) + jnp.log1p(lb * jnp.exp(-x))
        k = (1.0 - lb) * _sigmoid(-x)

    bf = lambda t: t.astype(BF16)
    geo = _pair_geometry(nh, dk)
    npair = len(geo)
    chunks = range(nch)
    rows = [slice(i * c, (i + 1) * c) for i in chunks]
    vb = bf(v)

    qs, ks, q_in, k_rem, e_last = [], [], [], [], []
    for i in chunks:
        dall = _dot_sel_l(mats_ref[...], la[rows[i]])
        bcum = dall[nlev * c:(nlev + 1) * c]
        wl = [jnp.exp(-jnp.abs(dall[l * c:(l + 1) * c])) for l in range(nlev)]
        qc, kc = q[rows[i]], k[rows[i]]
        qs.append([bf(qc * w) for w in wl] + [bf(qc)])
        ks.append([bf(kc * w) for w in wl] + [bf(kc)])
        q_in.append(bf(qc * jnp.exp(bcum)))
        k_rem.append(bf(kc * jnp.exp(dall[(nlev + 1) * c:(nlev + 2) * c])))
        e_last.append(jnp.exp(bcum[c - 1:c]))

    out_cols = []
    for p in range(npair):
        start, width, offs = geo[p]
        blk = slice(start, start + width)
        vcol = slice(p * LANE, (p + 1) * LANE)
        lane = lax.broadcasted_iota(jnp.int32, (c, width), 1)
        hmask = [(lane >= o) & (lane < o + dk) for o in offs]
        zero = jnp.zeros((c, width), BF16)
        atts = []
        for i in chunks:
            att = None
            for l in range(nlev + 1):
                qb = qs[i][l][:, blk]
                lhs = jnp.concatenate([jnp.where(hmask[0], qb, zero), jnp.where(hmask[1], qb, zero)], axis=0)
                term = _dot_nt(lhs, ks[i][l][:, blk]) * masks_ref[l]
                att = term if att is None else att + term
            atts.append(bf(att))
        lane_v = lax.broadcasted_iota(jnp.int32, (c, LANE), 1)
        first = lane_v < dv
        intra = []
        for i in chunks:
            r2 = _dot(atts[i], vb[rows[i], vcol])
            intra.append(jnp.where(first, r2[0:c], r2[c:2 * c]))
        ri = lax.broadcasted_iota(jnp.int32, (2 * dv, width), 0)
        li = lax.broadcasted_iota(jnp.int32, (2 * dv, width), 1)
        smask = ((ri < dv) & (li >= offs[0]) & (li < offs[0] + dk)) | ((ri >= dv) & (li >= offs[1]) & (li < offs[1] + dk))
        upd = [jnp.where(smask, _dot_tn(vb[rows[i], vcol], k_rem[i][:, blk]), 0.0) for i in chunks]
        s = s_scr[p, :, 0:width]
        states = []
        for i in chunks:
            states.append(bf(s))
            s = s * e_last[i][:, blk] + upd[i]
        s_scr[p, :, 0:width] = s
        out_cols.append(jnp.concatenate(
            [intra[i] + _dot_nt(q_in[i][:, blk], states[i]) for i in chunks], axis=0))
    o = jnp.concatenate(out_cols, axis=-1)

    if mode == "gla":
        ms = _seg_sum(o * o, ind_ref[...]) * (1.0 / dv)
        o = o * lax.rsqrt(ms + EPS) * ng_ref[...] * _silu(gate)
    else:
        o = o * _sigmoid(gate)
        ms = _seg_sum(o * o, ind_ref[...]) * (1.0 / dv)
        o = o * lax.rsqrt(ms + EPS) * ng_ref[...]
    o_ref[...] = o.astype(o_ref.dtype)

    @pl.when(ci == pl.num_programs(1) - 1)
    def _():
        st_ref[0] = s_scr[...]


def _gla_chunk(mode, slab, nb, seq, p1, p2, ng):
    nh, dk, dv = (H_GLA, DK_GLA, DV_GLA) if mode == "gla" else (H_HG, DK_HG, DV_HG)
    w = slab.shape[1]
    nch = math.gcd(seq // CHUNK, GLA_CHUNKS_PER_STEP)
    sc = nch * CHUNK
    ns = seq // sc
    mats, masks, nlev = _decay_consts(CHUNK)
    ind = _seg_ind(nh, dv)
    ngt = jnp.tile(ng, (1, nh))
    geo = _pair_geometry(nh, dk)
    fix2 = lambda b, c: (0, 0)
    kern = functools.partial(_gla_chunk_kernel, mode, nh, dk, dv, nlev, nch)
    o, st = pl.pallas_call(
        kern,
        grid=(nb, ns),
        in_specs=[pl.BlockSpec((sc, w), lambda b, c: (b * ns + c, 0)),
                  pl.BlockSpec(mats.shape, fix2),
                  pl.BlockSpec(masks.shape, lambda b, c: (0, 0, 0)),
                  pl.BlockSpec(ind.shape, fix2),
                  pl.BlockSpec(p1.shape, fix2), pl.BlockSpec(p2.shape, fix2), pl.BlockSpec(ngt.shape, fix2)],
        out_specs=[pl.BlockSpec((sc, nh * dv), lambda b, c: (b * ns + c, 0)),
                   pl.BlockSpec((1, len(geo), 2 * dv, LANE), lambda b, c: (b, 0, 0, 0))],
        out_shape=[jax.ShapeDtypeStruct((nb * seq, nh * dv), BF16),
                   jax.ShapeDtypeStruct((nb, len(geo), 2 * dv, LANE), F32)],
        scratch_shapes=[pltpu.VMEM((len(geo), 2 * dv, LANE), F32)],
        compiler_params=_cparams(("parallel", "arbitrary")),
        name="chunk_" + mode,
    )(slab, mats, masks, ind, p1, p2, ngt)
    heads = []
    for p, (_, _, offs) in enumerate(geo):
        for j in range(2):
            heads.append(jnp.swapaxes(st[:, p, j * dv:(j + 1) * dv, offs[j]:offs[j] + dk], -1, -2))
    return o, jnp.stack(heads, axis=1)


def _dn_chunk_kernel(nch, slab_ref, abr_ref, lm_ref, umb_ref, ind_ref, cw_ref, pcol_ref, prow_ref, ng_ref,
                     o_ref, st_ref, s_scr, ext_scr):
    c = CHUNK
    sc = nch * c
    nh, dk, dv = H_DN, DK_DN, DV_DN
    npair = nh // 2
    ci = pl.program_id(1)

    @pl.when(ci == 0)
    def _():
        s_scr[...] = jnp.zeros_like(s_scr)
        ext_scr[0:8, :] = jnp.zeros((8, CONV_DIM), F32)

    ext_scr[8:8 + sc, :] = slab_ref[:, 0:CONV_DIM]
    conv = ext_scr[5:5 + sc, :] * cw_ref[0:1, :]
    for i in range(1, CONV_W):
        conv = conv + ext_scr[5 + i:5 + i + sc, :] * cw_ref[i:i + 1, :]
    ext_scr[0:8, :] = ext_scr[sc:sc + 8, :]
    qkv = _silu(conv)
    gate = slab_ref[:, CONV_DIM:DN_AB]
    hk = nh * dk
    q = qkv[:, 0:hk]
    k = qkv[:, hk:2 * hk]
    v = qkv[:, 2 * hk:2 * hk + nh * dv]
    q = q * lax.rsqrt(_seg_sum(q * q, ind_ref[...]) + EPS) * (dk ** -0.5)
    k = k * lax.rsqrt(_seg_sum(k * k, ind_ref[...]) + EPS)

    ab_c = slab_ref[:, DN_AB:DN_AB + LANE]
    g_c = -jnp.exp(pcol_ref[0:1, :]) * _softplus(ab_c + pcol_ref[1:2, :])
    beta_c = _sigmoid(ab_c)
    g_r = -jnp.exp(prow_ref[0]) * _softplus(abr_ref[0, 0] + prow_ref[1])
    gcum_r = _dot_sel_r(g_r, umb_ref[...])

    bf = lambda t: t.astype(BF16)
    rows = [slice(i * c, (i + 1) * c) for i in range(nch)]
    ri = lax.broadcasted_iota(jnp.int32, (2 * c, 2 * c), 0)
    cj = lax.broadcasted_iota(jnp.int32, (2 * c, 2 * c), 1)
    same = (ri < c) == (cj < c)
    tri = same & (ri >= cj)
    strict = same & (ri > cj)
    eye = (ri == cj).astype(F32)
    lane = lax.broadcasted_iota(jnp.int32, (c, LANE), 1)
    first = lane < dk
    zero = jnp.zeros((c, LANE), BF16)
    gcum_cs = [_dot_sel_l(lm_ref[...], g_c[rows[i]]) for i in range(nch)]

    def lanes2(col0, col1):
        return jnp.where(first, jnp.broadcast_to(col0, (c, LANE)), jnp.broadcast_to(col1, (c, LANE)))

    def stack2(col0, col1):
        return jnp.concatenate([jnp.broadcast_to(col0, (c, LANE)), jnp.broadcast_to(col1, (c, LANE))], axis=0)

    def rows2(x):
        return jnp.concatenate([jnp.where(first, x, zero), jnp.where(first, zero, x)], axis=0)

    units = [(i, p) for i in range(nch) for p in range(npair)]
    pre = {}
    for (i, p) in units:
        h0, h1 = 2 * p, 2 * p + 1
        blk = slice(p * LANE, (p + 1) * LANE)
        gcc = gcum_cs[i]
        gc0, gc1 = gcc[:, h0:h0 + 1], gcc[:, h1:h1 + 1]
        gl0, gl1 = gcc[c - 1:c, h0:h0 + 1], gcc[c - 1:c, h1:h1 + 1]
        b0, b1 = beta_c[rows[i], nh + h0:nh + h0 + 1], beta_c[rows[i], nh + h1:nh + h1 + 1]
        gcr = gcum_r[p:p + 1, i * LANE:(i + 1) * LANE]
        gam = jnp.where(tri, jnp.exp(jnp.where(tri, stack2(gc0, gc1) - gcr, 0.0)), 0.0)
        kc, qc, vc = k[rows[i], blk], q[rows[i], blk], v[rows[i], blk]
        beta_l = lanes2(b0, b1)
        egc_l = jnp.exp(lanes2(gc0, gc1))
        kb = kc * beta_l
        pre[(i, p)] = dict(
            gam=gam, k2=rows2(bf(kc)), kb2=rows2(bf(kb)), q2=rows2(bf(qc)), vb2=rows2(bf(vc * beta_l)),
            ke2=rows2(bf(kb * egc_l)), qe=bf(qc * egc_l),
            kd2=rows2(bf(kc * jnp.exp(lanes2(gl0, gl1) - lanes2(gc0, gc1)))),
            eglast=jnp.exp(stack2(gl0, gl1)))
    ms = {u: jnp.where(strict, _dot_nt(pre[u]["kb2"], pre[u]["k2"]) * pre[u]["gam"], 0.0) for u in units}
    tinv = {u: eye - ms[u] for u in units}
    pw = {u: bf(ms[u]) for u in units}
    for _ in range(int(math.log2(c)) - 1):
        pw = {u: bf(_dot(pw[u], pw[u])) for u in units}
        tinv = {u: tinv[u] + _dot(bf(tinv[u]), pw[u]) for u in units}
    tb = {u: bf(tinv[u]) for u in units}
    uu = {u: _dot(tb[u], pre[u]["vb2"]) for u in units}
    ww = {u: bf(_dot(tb[u], pre[u]["ke2"])) for u in units}
    att = {u: bf(_dot_nt(pre[u]["q2"], pre[u]["k2"]) * pre[u]["gam"]) for u in units}

    out_cols = []
    for p in range(npair):
        s = s_scr[p]
        pieces = []
        for i in range(nch):
            u = (i, p)
            sb = bf(s)
            vn = bf(uu[u] - _dot(ww[u], sb))
            o2 = _dot(att[u], vn)
            pieces.append(_dot(pre[u]["qe"], sb) + o2[0:c] + o2[c:2 * c])
            s = s * pre[u]["eglast"] + _dot_tn(pre[u]["kd2"], vn)
        s_scr[p] = s
        out_cols.append(jnp.concatenate(pieces, axis=0))
    o = jnp.concatenate(out_cols, axis=-1)
    ms_o = _seg_sum(o * o, ind_ref[...]) * (1.0 / dv)
    o_ref[...] = (o * lax.rsqrt(ms_o + EPS) * ng_ref[...] * _silu(gate)).astype(o_ref.dtype)

    @pl.when(ci == pl.num_programs(1) - 1)
    def _():
        st_ref[0] = s_scr[...]


def _dn_chunk(slab, nb, seq, conv_w, a_log, dt_bias, ng):
    c = CHUNK
    nch = math.gcd(seq // c, DN_CHUNKS_PER_STEP)
    sc = nch * c
    ns = seq // sc
    npair = H_DN // 2
    nrow = 16
    idx = np.arange(c)
    lm = jnp.asarray((idx[None, :] <= idx[:, None]), BF16)
    um = (idx[:, None] <= idx[None, :]).astype(np.float32)
    umb = jnp.asarray(np.kron(np.eye(2 * nch, dtype=np.float32), um), BF16)
    ind = _seg_ind(H_DN, DK_DN)
    cw = jnp.pad(conv_w, ((0, 8 - CONV_W), (0, 0)))
    pad = lambda p: jnp.pad(p, (0, LANE - H_DN))
    pcol = jnp.zeros((8, LANE), F32).at[0].set(pad(a_log)).at[1].set(pad(dt_bias))
    a_cols = slab[:, DN_AB:DN_AB + H_DN].reshape(nb, ns, nch, c, npair, 2)
    a_rows = jnp.transpose(a_cols, (0, 1, 4, 2, 5, 3)).reshape(nb, ns, npair, nch * 2 * c)
    a_rows = jnp.pad(a_rows, ((0, 0), (0, 0), (0, nrow - npair), (0, 0)))
    rowp = lambda p: jnp.pad(jnp.broadcast_to(p.reshape(npair, 1, 2, 1), (npair, nch, 2, c)).reshape(npair, nch * 2 * c),
                             ((0, nrow - npair), (0, 0)))
    prow = jnp.stack([rowp(a_log), rowp(dt_bias)])
    ngt = jnp.tile(ng, (1, H_DN))
    fix2 = lambda b, c_: (0, 0)
    o, st = pl.pallas_call(
        functools.partial(_dn_chunk_kernel, nch),
        grid=(nb, ns),
        in_specs=[pl.BlockSpec((sc, DN_W), lambda b, c_: (b * ns + c_, 0)),
                  pl.BlockSpec((1, 1, nrow, nch * 2 * c), lambda b, c_: (b, c_, 0, 0)),
                  pl.BlockSpec((c, c), fix2), pl.BlockSpec(umb.shape, fix2), pl.BlockSpec(ind.shape, fix2),
                  pl.BlockSpec((8, CONV_DIM), fix2), pl.BlockSpec((8, LANE), fix2),
                  pl.BlockSpec(prow.shape, lambda b, c_: (0, 0, 0)), pl.BlockSpec(ngt.shape, fix2)],
        out_specs=[pl.BlockSpec((sc, H_DN * DV_DN), lambda b, c_: (b * ns + c_, 0)),
                   pl.BlockSpec((1, npair, 2 * DK_DN, 2 * DV_DN), lambda b, c_: (b, 0, 0, 0))],
        out_shape=[jax.ShapeDtypeStruct((nb * seq, H_DN * DV_DN), BF16),
                   jax.ShapeDtypeStruct((nb, npair, 2 * DK_DN, 2 * DV_DN), F32)],
        scratch_shapes=[pltpu.VMEM((npair, 2 * DK_DN, 2 * DV_DN), F32), pltpu.VMEM((sc + 8, CONV_DIM), F32)],
        compiler_params=_cparams(("parallel", "arbitrary")),
        name="chunk_dn",
    )(slab, a_rows, lm, umb, ind, cw, pcol, prow, ngt)
    heads = [st[:, p, j * DK_DN:(j + 1) * DK_DN, j * DV_DN:(j + 1) * DV_DN] for p in range(npair) for j in range(2)]
    return o, jnp.stack(heads, axis=1)


def _decode_gla_kernel(q_ref, k_ref, v_ref, g_ref, alr_ref, wa2t_ref, ba_ref, ng_ref, s_ref, o_ref, so_ref):
    dk = q_ref.shape[1]
    z = _dot_hp(wa2t_ref[0], alr_ref[...]) + ba_ref[0]
    dec = jnp.exp(_log_sigmoid(z) * (1.0 / GLA_TAU))
    q = q_ref[0] * (dk ** -0.5)
    k = k_ref[0]
    v = v_ref[0]
    acc = jnp.zeros_like(v)
    for d in range(dk):
        s_new = s_ref[0, d] * dec[d:d + 1, :] + k[d:d + 1, :] * v
        so_ref[0, d] = s_new
        acc = acc + q[d:d + 1, :] * s_new
    ms = jnp.mean(acc * acc, axis=0, keepdims=True)
    o_ref[0] = acc * lax.rsqrt(ms + EPS) * ng_ref[...] * _silu(g_ref[0])


def _decode_dn_kernel(x_ref, cb_ref, cw_ref, a_ref, b_ref, p_ref, g_ref, ng_ref, s_ref, o_ref, so_ref):
    dk = s_ref.shape[1]
    conv = x_ref[:, 0] * cw_ref[CONV_W - 1, :, 0]
    for i in range(CONV_W - 1):
        conv = conv + cb_ref[i, :, 0] * cw_ref[i, :, 0]
    qkv = _silu(conv)
    q, k, v = qkv[0], qkv[1], qkv[2]
    q = q * lax.rsqrt(jnp.sum(q * q, axis=0, keepdims=True) + EPS) * (dk ** -0.5)
    k = k * lax.rsqrt(jnp.sum(k * k, axis=0, keepdims=True) + EPS)
    eg = jnp.exp(-jnp.exp(p_ref[0, 0:1, :]) * _softplus(a_ref[0] + p_ref[0, 1:2, :]))
    beta = _sigmoid(b_ref[0])
    ks = jnp.zeros_like(v)
    for d in range(dk):
        ks = ks + k[d:d + 1, :] * s_ref[0, d]
    v_new = beta * (v - eg * ks)
    acc = jnp.zeros_like(v)
    for d in range(dk):
        s_new = s_ref[0, d] * eg + k[d:d + 1, :] * v_new
        so_ref[0, d] = s_new
        acc = acc + q[d:d + 1, :] * s_new
    ms = jnp.mean(acc * acc, axis=0, keepdims=True)
    o_ref[0] = acc * lax.rsqrt(ms + EPS) * ng_ref[...] * _silu(g_ref[0])


def _decode_hg_kernel(q_ref, f_ref, v_ref, g_ref, lb_ref, ng_ref, s_ref, o_ref, so_ref):
    dk = q_ref.shape[1]
    x = f_ref[0]
    lb = lb_ref[0]
    f = jnp.exp(_log_sigmoid(x) + jnp.log1p(lb * jnp.exp(-x)))
    k = (1.0 - lb) * _sigmoid(-x)
    q = _silu(q_ref[0])
    v = v_ref[0]
    acc = jnp.zeros_like(v)
    for d in range(dk):
        s_new = s_ref[0, d] * f[d:d + 1, :] + k[d:d + 1, :] * v
        so_ref[0, d] = s_new
        acc = acc + q[d:d + 1, :] * s_new
    acc = acc * _sigmoid(g_ref[0])
    ms = jnp.mean(acc * acc, axis=0, keepdims=True)
    o_ref[0] = acc * lax.rsqrt(ms + EPS) * ng_ref[...]


def _head_call(kern, name, nh, dk, dv, nb, args, specs):
    o, s = pl.pallas_call(
        kern,
        grid=(nh,),
        in_specs=specs + [pl.BlockSpec((1, dk, dv, nb), lambda h: (h, 0, 0, 0))],
        out_specs=[pl.BlockSpec((1, dv, nb), lambda h: (h, 0, 0)),
                   pl.BlockSpec((1, dk, dv, nb), lambda h: (h, 0, 0, 0))],
        out_shape=[jax.ShapeDtypeStruct((nh, dv, nb), F32), jax.ShapeDtypeStruct((nh, dk, dv, nb), F32)],
        compiler_params=_cparams(("parallel",)),
        name=name,
    )(*args)
    return o.reshape(nh * dv, nb), jnp.transpose(s, (3, 0, 1, 2))


def _decode(pa, pb, pc, s_gla, s_dn, s_conv, s_hg, wa2, ba, conv_w, a_log, dt_bias, lb, nga, ngb, ngc):
    nb = pa.shape[0]
    bl = lambda p, *shape: jnp.broadcast_to(p.reshape(shape + (1,)), shape + (nb,))
    tr = lambda s: jnp.transpose(s, (1, 2, 3, 0))
    byh = lambda n: pl.BlockSpec((1, n, nb), lambda h: (h, 0, 0))
    fixed = lambda shape: pl.BlockSpec(shape, lambda h: (0,) * len(shape))

    nh, dk, dv = H_GLA, DK_GLA, DV_GLA
    hk, hv = nh * dk, nh * dv
    pt = pa.T
    args = (pt[0:hk].reshape(nh, dk, nb), pt[hk:2 * hk].reshape(nh, dk, nb),
            pt[2 * hk:2 * hk + hv].reshape(nh, dv, nb), pt[2 * hk + hv:2 * hk + 2 * hv].reshape(nh, dv, nb),
            pt[2 * hk + 2 * hv:2 * hk + 2 * hv + GLA_RANK], wa2.T.reshape(nh, dk, GLA_RANK),
            bl(ba, nh, dk), bl(nga, dv), tr(s_gla))
    specs = [byh(dk), byh(dk), byh(dv), byh(dv), fixed((GLA_RANK, nb)),
             pl.BlockSpec((1, dk, GLA_RANK), lambda h: (h, 0, 0)), byh(dk), fixed((dv, nb))]
    o_a, sa = _head_call(_decode_gla_kernel, "decode_gla", nh, dk, dv, nb, args, specs)

    nh, dk, dv = H_DN, DK_DN, DV_DN
    pt = pb.T
    x = pt[0:CONV_DIM].reshape(3, nh, dk, nb)
    cb = jnp.transpose(s_conv, (1, 2, 0)).reshape(CONV_W - 1, 3, nh, dk, nb)
    cw = bl(conv_w, CONV_W, 3, nh, dk)
    prm = jnp.stack([bl(a_log, nh), bl(dt_bias, nh)], axis=1)
    args = (x, cb, cw, pt[DN_AB:DN_AB + nh].reshape(nh, 1, nb), pt[DN_AB + nh:DN_AB + 2 * nh].reshape(nh, 1, nb),
            prm, pt[CONV_DIM:DN_AB].reshape(nh, dv, nb), bl(ngb, dv), tr(s_dn))
    specs = [pl.BlockSpec((3, 1, dk, nb), lambda h: (0, h, 0, 0)),
             pl.BlockSpec((CONV_W - 1, 3, 1, dk, nb), lambda h: (0, 0, h, 0, 0)),
             pl.BlockSpec((CONV_W, 3, 1, dk, nb), lambda h: (0, 0, h, 0, 0)),
             byh(1), byh(1), byh(2), byh(dv), fixed((dv, nb))]
    o_b, sb = _head_call(_decode_dn_kernel, "decode_dn", nh, dk, dv, nb, args, specs)

    nh, dk, dv = H_HG, DK_HG, DV_HG
    hk = nh * dk
    pt = pc.T
    args = (pt[0:hk].reshape(nh, dk, nb), pt[hk:2 * hk].reshape(nh, dk, nb),
            pt[2 * hk:3 * hk].reshape(nh, dv, nb), pt[3 * hk:4 * hk].reshape(nh, dv, nb),
            bl(lb, nh, dk), bl(ngc, dv), tr(s_hg))
    specs = [byh(dk), byh(dk), byh(dv), byh(dv), byh(dk), fixed((dv, nb))]
    o_c, sc = _head_call(_decode_hg_kernel, "decode_hgrn2", nh, dk, dv, nb, args, specs)

    return jnp.concatenate([o_a, o_b, o_c], axis=0).T.astype(BF16), sa, sb, sc


def _outproj_router_kernel(x_ref, oa_ref, ob_ref, oc_ref, w_ref, g_ref, rw_ref, rb_ref,
                           x1_ref, h2_ref, gate_ref):
    na, nb_ = oa_ref.shape[1], ob_ref.shape[1]
    x1 = (x_ref[...] + _dot(oa_ref[...], w_ref[0:na, :]) + _dot(ob_ref[...], w_ref[na:na + nb_, :])
          + _dot(oc_ref[...], w_ref[na + nb_:, :]))
    x1_ref[...] = x1
    h2 = _rms(x1, g_ref[...]).astype(BF16)
    h2_ref[...] = h2
    logits = _dot(h2, rw_ref[...]) + rb_ref[...]
    lane = lax.broadcasted_iota(jnp.int32, logits.shape, 1)
    neg = jnp.float32(-jnp.inf)
    big = jnp.int32(1 << 20)
    is_g = (lane >= N_EXPERTS) & (lane < N_EXPERTS + N_GROUPS)
    lg = jnp.where(is_g, logits, neg)
    mg = jnp.max(lg, axis=-1, keepdims=True)
    pg_top = 1.0 / jnp.sum(jnp.where(is_g, jnp.exp(lg - mg), 0.0), axis=-1, keepdims=True)
    g_idx = jnp.min(jnp.where(lg == mg, lane, big), axis=-1, keepdims=True) - N_EXPERTS
    in_grp = (lane >= g_idx * EXP_PER_GROUP) & (lane < (g_idx + 1) * EXP_PER_GROUP)
    le = jnp.where(in_grp, logits, neg)
    me = jnp.max(le, axis=-1, keepdims=True)
    ex = jnp.where(in_grp, jnp.exp(le - me), 0.0)
    pe = ex / jnp.sum(ex, axis=-1, keepdims=True)
    pe = jnp.where(in_grp, pe, -1.0)
    v1 = jnp.max(pe, axis=-1, keepdims=True)
    i1 = jnp.min(jnp.where(pe == v1, lane, big), axis=-1, keepdims=True)
    pe2 = jnp.where(lane == i1, -1.0, pe)
    v2 = jnp.max(pe2, axis=-1, keepdims=True)
    i2 = jnp.min(jnp.where(pe2 == v2, lane, big), axis=-1, keepdims=True)
    tot = v1 + v2
    gate = jnp.where(lane == i1, v1 / tot, 0.0) + jnp.where(lane == i2, v2 / tot, 0.0)
    gate_ref[...] = pg_top * gate


def _outproj_router(x, oa, ob, oc, w, g, rw, rb, tm):
    t, d = x.shape
    row = lambda i: (i, 0)
    fix = lambda i: (0, 0)
    return pl.pallas_call(
        _outproj_router_kernel,
        grid=(t // tm,),
        in_specs=[pl.BlockSpec((tm, d), row), pl.BlockSpec((tm, oa.shape[1]), row),
                  pl.BlockSpec((tm, ob.shape[1]), row), pl.BlockSpec((tm, oc.shape[1]), row),
                  pl.BlockSpec(w.shape, fix), pl.BlockSpec((1, d), fix),
                  pl.BlockSpec(rw.shape, fix), pl.BlockSpec((1, LANE), fix)],
        out_specs=[pl.BlockSpec((tm, d), row), pl.BlockSpec((tm, d), row), pl.BlockSpec((tm, LANE), row)],
        out_shape=[jax.ShapeDtypeStruct((t, d), F32), jax.ShapeDtypeStruct((t, d), BF16),
                   jax.ShapeDtypeStruct((t, LANE), F32)],
        compiler_params=_cparams(("parallel",)),
        name="outproj_router",
    )(x, oa, ob, oc, w, g, rw, rb)


def _moe_kernel(final, x1_ref, h2_ref, gate_ref, w1_ref, w3_ref, w2_ref, fg_ref, y_ref):
    e = pl.program_id(1)

    @pl.when(e == 0)
    def _():
        y_ref[...] = x1_ref[...]

    h2 = h2_ref[...]
    lane = lax.broadcasted_iota(jnp.int32, gate_ref.shape, 1)
    ge = jnp.sum(jnp.where(lane == e, gate_ref[...], 0.0), axis=-1, keepdims=True)
    hid = _silu(_dot(h2, w1_ref[0])) * _dot(h2, w3_ref[0]) * ge
    y_ref[...] += _dot(hid.astype(BF16), w2_ref[0])

    if final:
        @pl.when(e == pl.num_programs(1) - 1)
        def _():
            y_ref[...] = _rms(y_ref[...], fg_ref[...])


def _moe(x1, h2, gate, w1, w3, w2, fg, final, tm):
    t, d = x1.shape
    ne, _, f = w1.shape
    row = lambda i, e: (i, 0)
    return pl.pallas_call(
        functools.partial(_moe_kernel, final),
        grid=(t // tm, ne),
        in_specs=[pl.BlockSpec((tm, d), row), pl.BlockSpec((tm, d), row), pl.BlockSpec((tm, LANE), row),
                  pl.BlockSpec((1, d, f), lambda i, e: (e, 0, 0)), pl.BlockSpec((1, d, f), lambda i, e: (e, 0, 0)),
                  pl.BlockSpec((1, f, d), lambda i, e: (e, 0, 0)), pl.BlockSpec((1, d), lambda i, e: (0, 0))],
        out_specs=pl.BlockSpec((tm, d), row),
        out_shape=jax.ShapeDtypeStruct((t, d), F32),
        compiler_params=_cparams(("parallel", "arbitrary")),
        name="moe",
    )(x1, h2, gate, w1, w3, w2, fg)


def kernel(x_prompt, x_sample, state_gla, state_dn, state_conv, state_hgrn, norm1_g, w_in, gla_wa2, gla_ba, gla_norm_g, dn_conv_w, dn_a_log, dn_dt_bias, dn_norm_g, hg_lb_logits, hg_norm_g, w_out, norm2_g, router_g_w, router_g_b, router_e_w, router_e_b, exp_w1, exp_w3, exp_w2, final_norm_g):
    nbp, seq, d = x_prompt.shape
    nbs = x_sample.shape[0]
    depth = w_in.shape[0]
    assert x_sample.shape[1] == 1 and seq % CHUNK == 0
    tp = nbp * seq
    xp = x_prompt.reshape(tp, d)
    xs = x_sample.reshape(nbs, d)

    sm = jax.nn.softmax(hg_lb_logits.astype(F32), axis=0)
    lb_all = jnp.maximum(jnp.cumsum(sm, axis=0) - sm[0:1], 0.0)

    tm_p = min(512, tp)
    tm_moe = min(1024, tp)
    row2 = lambda v: v.reshape(1, -1)
    gla_p, dn_p, conv_p, hg_p, gla_s, dn_s, conv_s, hg_s = ([] for _ in range(8))
    for li in range(depth):
        w_in_b = _perm_w_in(w_in[li])
        w_out_b = w_out[li].astype(BF16)
        rw = jnp.pad(jnp.concatenate([router_e_w[li], router_g_w[li]], axis=1),
                     ((0, 0), (0, LANE - N_EXPERTS - N_GROUPS))).astype(BF16)
        rb = jnp.pad(jnp.concatenate([router_e_b[li], router_g_b[li]]), (0, LANE - N_EXPERTS - N_GROUPS)).reshape(1, LANE)
        w1b, w3b, w2b = exp_w1[li].astype(BF16), exp_w3[li].astype(BF16), exp_w2[li].astype(BF16)
        final = li == depth - 1
        g1, g2, fg = row2(norm1_g[li]), row2(norm2_g[li]), row2(final_norm_g)

        pa, pb, pc = _inproj(xp, g1, w_in_b, tm_p)
        oa, sa = _gla_chunk("gla", pa, nbp, seq, gla_wa2[li], row2(gla_ba[li]), row2(gla_norm_g[li]))
        ob, sb = _dn_chunk(pb, nbp, seq, dn_conv_w[li], dn_a_log[li], dn_dt_bias[li], row2(dn_norm_g[li]))
        oc, sc = _gla_chunk("hgrn2", pc, nbp, seq, row2(lb_all[li]), row2(lb_all[li]), row2(hg_norm_g[li]))
        gla_p.append(sa)
        dn_p.append(sb)
        conv_p.append(pb.reshape(nbp, seq, DN_W)[:, seq - (CONV_W - 1):, 0:CONV_DIM])
        hg_p.append(sc)
        x1, h2, gate = _outproj_router(xp, oa, ob, oc, w_out_b, g2, rw, rb, tm_p)
        xp = _moe(x1, h2, gate, w1b, w3b, w2b, fg, final, tm_moe)

        qa, qb, qc = _inproj(xs, g1, w_in_b, nbs)
        o_s, sa, sb, sc = _decode(qa, qb, qc, state_gla[li], state_dn[li], state_conv[li], state_hgrn[li],
                                  gla_wa2[li], gla_ba[li], dn_conv_w[li], dn_a_log[li], dn_dt_bias[li],
                                  lb_all[li], gla_norm_g[li], dn_norm_g[li], hg_norm_g[li])
        gla_s.append(sa)
        dn_s.append(sb)
        conv_s.append(jnp.concatenate([state_conv[li][:, 1:], qb[:, None, 0:CONV_DIM]], axis=1))
        hg_s.append(sc)
        na, nb_ = H_GLA * DV_GLA, H_DN * DV_DN
        x1, h2, gate = _outproj_router(xs, o_s[:, 0:na], o_s[:, na:na + nb_], o_s[:, na + nb_:], w_out_b, g2, rw, rb, nbs)
        xs = _moe(x1, h2, gate, w1b, w3b, w2b, fg, final, nbs)

    st = lambda xs_, ref: jnp.stack(xs_).astype(ref.dtype)
    return (xp.reshape(nbp, seq, d), xs.reshape(nbs, 1, d),
            st(gla_p, state_gla), st(dn_p, state_dn), st(conv_p, state_conv), st(hg_p, state_hgrn),
            st(gla_s, state_gla), st(dn_s, state_dn), st(conv_s, state_conv), st(hg_s, state_hgrn))
```

```python
import functools
import math

import numpy as np
import jax
import jax.numpy as jnp
from jax import lax
from jax.experimental import pallas as pl
from jax.experimental.pallas import tpu as pltpu

F32 = jnp.float32
BF16 = jnp.bfloat16
EPS = 1e-6

H_GLA, DK_GLA, DV_GLA, GLA_RANK, GLA_TAU = 6, 32, 64, 16, 16.0
H_DN, DK_DN, DV_DN, CONV_W = 6, 64, 64, 4
CONV_DIM = H_DN * (2 * DK_DN + DV_DN)
H_HG, DK_HG, DV_HG = 4, 64, 64
N_GROUPS, EXP_PER_GROUP, TOP_K = 4, 8, 2
N_EXPERTS = N_GROUPS * EXP_PER_GROUP

LANE = 128
CHUNK = 64
GLA_CHUNKS_PER_STEP = 4
DN_CHUNKS_PER_STEP = 2
MOE_TILE = 1024
MOE_ROWS_FIRST = 320
MOE_ROWS_EXTRA = 128
VMEM_LIMIT = 56 * 1024 * 1024

GLA_W = 1280
DN_W = 1664
HG_W = 1024
DN_AB = 1536


def _cparams(sem):
    return pltpu.CompilerParams(dimension_semantics=sem, vmem_limit_bytes=VMEM_LIMIT)


def _dot(a, b):
    return jnp.dot(a, b, preferred_element_type=F32)


def _dot_nt(a, b):
    return lax.dot_general(a, b, (((1,), (1,)), ((), ())), preferred_element_type=F32)


def _dot_tn(a, b):
    return lax.dot_general(a, b, (((0,), (0,)), ((), ())), preferred_element_type=F32)


def _split3(x):
    hi = x.astype(BF16)
    r = x - hi.astype(F32)
    mid = r.astype(BF16)
    lo = (r - mid.astype(F32)).astype(BF16)
    return hi, mid, lo


def _split2(x):
    hi = x.astype(BF16)
    lo = (x - hi.astype(F32)).astype(BF16)
    return hi, lo


def _dot_sel_l(m, x):
    hi, mid, lo = _split3(x)
    return _dot(m, hi) + _dot(m, mid) + _dot(m, lo)


def _dot_sel_r(x, m):
    hi, mid, lo = _split3(x)
    return _dot(hi, m) + _dot(mid, m) + _dot(lo, m)


def _dot_hp(a, b, fn=_dot):
    ah, al = _split2(a)
    bh, bl = _split2(b)
    return fn(ah, bh) + fn(ah, bl) + fn(al, bh)


def _rms(x, g):
    return x * lax.rsqrt(jnp.mean(x * x, axis=-1, keepdims=True) + EPS) * g


def _sigmoid(x):
    return 1.0 / (1.0 + jnp.exp(-x))


def _silu(x):
    return x * _sigmoid(x)


def _log_sigmoid(x):
    return jnp.minimum(x, 0.0) - jnp.log1p(jnp.exp(-jnp.abs(x)))


def _softplus(x):
    return jnp.maximum(x, 0.0) + jnp.log1p(jnp.exp(-jnp.abs(x)))


def _inproj_kernel(x_ref, g_ref, w_ref, oa_ref, ob_ref, oc_ref):
    h = _rms(x_ref[...], g_ref[...]).astype(BF16)
    oa_ref[...] = _dot(h, w_ref[:, 0:GLA_W])
    ob_ref[...] = _dot(h, w_ref[:, GLA_W:GLA_W + DN_W])
    oc_ref[...] = _dot(h, w_ref[:, GLA_W + DN_W:GLA_W + DN_W + HG_W])


def _inproj(x, g, w, tm):
    t, d = x.shape
    n = w.shape[1]
    row = lambda i: (i, 0)
    fix = lambda i: (0, 0)
    return pl.pallas_call(
        _inproj_kernel,
        grid=(t // tm,),
        in_specs=[pl.BlockSpec((tm, d), row), pl.BlockSpec((1, d), fix), pl.BlockSpec((d, n), fix)],
        out_specs=[pl.BlockSpec((tm, GLA_W), row), pl.BlockSpec((tm, DN_W), row), pl.BlockSpec((tm, HG_W), row)],
        out_shape=[jax.ShapeDtypeStruct((t, GLA_W), F32), jax.ShapeDtypeStruct((t, DN_W), F32),
                   jax.ShapeDtypeStruct((t, HG_W), F32)],
        compiler_params=_cparams(("parallel",)),
        name="inproj",
    )(x, g, w)


def _perm_w_in(w):
    sizes = (H_GLA * DK_GLA, H_GLA * DK_GLA, H_GLA * DV_GLA, GLA_RANK, H_GLA * DV_GLA,
             CONV_DIM, H_DN, H_DN, H_DN * DV_DN,
             H_HG * DK_HG, H_HG * DK_HG, H_HG * DV_HG, H_HG * DV_HG)
    offs = np.concatenate([[0], np.cumsum(sizes)])
    seg = [w[:, offs[i]:offs[i + 1]] for i in range(len(sizes))]
    g_q, g_k, g_v, g_a, g_g, d_qkv, d_a, d_b, d_g, h_q, h_f, h_i, h_g = seg
    z = lambda n: jnp.zeros((w.shape[0], n), w.dtype)
    cols = [g_q, g_k, g_v, g_g, g_a, z(GLA_W - 1168),
            d_qkv, d_g, d_a, d_b, z(DN_W - 1548),
            h_q, h_f, h_i, h_g]
    return jnp.concatenate(cols, axis=1).astype(BF16)


def _decay_consts(c):
    n = int(math.log2(c))
    idx = np.arange(c)
    lm = (idx[None, :] <= idx[:, None]).astype(np.float32)
    mats, masks = [], []
    for l in range(1, n + 1):
        hs = c >> l
        bs = 2 * hs
        blk = idx // bs
        ref = blk * bs + hs - 1
        mats.append(lm - lm[ref])
        lower = (idx % bs) >= hs
        same = blk[:, None] == blk[None, :]
        masks.append((same & lower[:, None] & (~lower)[None, :]).astype(np.float32))
    mats.append(lm)
    mats.append(1.0 - lm)
    masks.append(np.eye(c, dtype=np.float32))
    masks = np.stack(masks)
    return (jnp.asarray(np.concatenate(mats, 0), BF16), jnp.asarray(np.concatenate([masks, masks], axis=1), F32), n)


def _seg_ind(nh, dv):
    h = np.arange(nh * dv) // dv
    return jnp.asarray(h[:, None] == h[None, :], BF16)


def _seg_sum(x, ind):
    hi, lo = _split2(x)
    return _dot(hi, ind) + _dot(lo, ind)


def _pair_geometry(nh, dk):
    geo = []
    for p in range(nh // 2):
        start = (2 * p * dk) // LANE * LANE
        width = min(LANE, nh * dk - start)
        geo.append((start, width, (2 * p * dk - start, (2 * p + 1) * dk - start)))
    return geo


def _gla_chunk_kernel(mode, nh, dk, dv, nlev, nch, slab_ref, mats_ref, masks_ref, ind_ref, p1_ref, p2_ref, ng_ref,
                      o_ref, st_ref, s_scr):
    c = CHUNK
    ci = pl.program_id(1)

    @pl.when(ci == 0)
    def _():
        s_scr[...] = jnp.zeros_like(s_scr)

    hk, hv = nh * dk, nh * dv
    if mode == "gla":
        q = slab_ref[:, 0:hk] * (dk ** -0.5)
        k = slab_ref[:, hk:2 * hk]
        v = slab_ref[:, 2 * hk:2 * hk + hv]
        gate = slab_ref[:, 2 * hk + hv:2 * hk + 2 * hv]
        a_lr = slab_ref[:, 2 * hk + 2 * hv:2 * hk + 2 * hv + GLA_RANK]
        z = _dot_hp(a_lr, p1_ref[...]) + p2_ref[...]
        la = _log_sigmoid(z) * (1.0 / GLA_TAU)
    else:
        q = _silu(slab_ref[:, 0:hk])
        x = slab_ref[:, hk:2 * hk]
        v = slab_ref[:, 2 * hk:2 * hk + hv]
        gate = slab_ref[:, 2 * hk + hv:2 * hk + 2 * hv]
        lb = p1_ref[...]
        la = _log_sigmoid(x) + jnp.log1p(lb * jnp.exp(-x))
        k = (1.0 - lb) * _sigmoid(-x)

    bf = lambda t: t.astype(BF16)
    geo = _pair_geometry(nh, dk)
    npair = len(geo)
    chunks = range(nch)
    rows = [slice(i * c, (i + 1) * c) for i in chunks]
    vb = bf(v)

    qs, ks, q_in, k_rem, e_last = [], [], [], [], []
    for i in chunks:
        dall = _dot_sel_l(mats_ref[...], la[rows[i]])
        bcum = dall[nlev * c:(nlev + 1) * c]
        wl = [jnp.exp(-jnp.abs(dall[l * c:(l + 1) * c])) for l in range(nlev)]
        qc, kc = q[rows[i]], k[rows[i]]
        qs.append([bf(qc * w) for w in wl] + [bf(qc)])
        ks.append([bf(kc * w) for w in wl] + [bf(kc)])
        q_in.append(bf(qc * jnp.exp(bcum)))
        k_rem.append(bf(kc * jnp.exp(dall[(nlev + 1) * c:(nlev + 2) * c])))
        e_last.append(jnp.exp(bcum[c - 1:c]))

    out_cols = []
    for p in range(npair):
        start, width, offs = geo[p]
        blk = slice(start, start + width)
        vcol = slice(p * LANE, (p + 1) * LANE)
        lane = lax.broadcasted_iota(jnp.int32, (c, width), 1)
        hmask = [(lane >= o) & (lane < o + dk) for o in offs]
        zero = jnp.zeros((c, width), BF16)
        atts = []
        for i in chunks:
            att = None
            for l in range(nlev + 1):
                qb = qs[i][l][:, blk]
                lhs = jnp.concatenate([jnp.where(hmask[0], qb, zero), jnp.where(hmask[1], qb, zero)], axis=0)
                term = _dot_nt(lhs, ks[i][l][:, blk]) * masks_ref[l]
                att = term if att is None else att + term
            atts.append(bf(att))
        lane_v = lax.broadcasted_iota(jnp.int32, (c, LANE), 1)
        first = lane_v < dv
        intra = []
        for i in chunks:
            r2 = _dot(atts[i], vb[rows[i], vcol])
            intra.append(jnp.where(first, r2[0:c], r2[c:2 * c]))
        ri = lax.broadcasted_iota(jnp.int32, (2 * dv, width), 0)
        li = lax.broadcasted_iota(jnp.int32, (2 * dv, width), 1)
        smask = ((ri < dv) & (li >= offs[0]) & (li < offs[0] + dk)) | ((ri >= dv) & (li >= offs[1]) & (li < offs[1] + dk))
        upd = [jnp.where(smask, _dot_tn(vb[rows[i], vcol], k_rem[i][:, blk]), 0.0) for i in chunks]
        s = s_scr[p, :, 0:width]
        states = []
        for i in chunks:
            states.append(bf(s))
            s = s * e_last[i][:, blk] + upd[i]
        s_scr[p, :, 0:width] = s
        out_cols.append(jnp.concatenate(
            [intra[i] + _dot_nt(q_in[i][:, blk], states[i]) for i in chunks], axis=0))
    o = jnp.concatenate(out_cols, axis=-1)

    if mode == "gla":
        ms = _seg_sum(o * o, ind_ref[...]) * (1.0 / dv)
        o = o * lax.rsqrt(ms + EPS) * ng_ref[...] * _silu(gate)
    else:
        o = o * _sigmoid(gate)
        ms = _seg_sum(o * o, ind_ref[...]) * (1.0 / dv)
        o = o * lax.rsqrt(ms + EPS) * ng_ref[...]
    o_ref[...] = o.astype(o_ref.dtype)

    @pl.when(ci == pl.num_programs(1) - 1)
    def _():
        st_ref[0] = s_scr[...]


def _gla_chunk(mode, slab, nb, seq, p1, p2, ng):
    nh, dk, dv = (H_GLA, DK_GLA, DV_GLA) if mode == "gla" else (H_HG, DK_HG, DV_HG)
    w = slab.shape[1]
    nch = math.gcd(seq // CHUNK, GLA_CHUNKS_PER_STEP)
    sc = nch * CHUNK
    ns = seq // sc
    mats, masks, nlev = _decay_consts(CHUNK)
    ind = _seg_ind(nh, dv)
    ngt = jnp.tile(ng, (1, nh))
    geo = _pair_geometry(nh, dk)
    fix2 = lambda b, c: (0, 0)
    kern = functools.partial(_gla_chunk_kernel, mode, nh, dk, dv, nlev, nch)
    o, st = pl.pallas_call(
        kern,
        grid=(nb, ns),
        in_specs=[pl.BlockSpec((sc, w), lambda b, c: (b * ns + c, 0)),
                  pl.BlockSpec(mats.shape, fix2),
                  pl.BlockSpec(masks.shape, lambda b, c: (0, 0, 0)),
                  pl.BlockSpec(ind.shape, fix2),
                  pl.BlockSpec(p1.shape, fix2), pl.BlockSpec(p2.shape, fix2), pl.BlockSpec(ngt.shape, fix2)],
        out_specs=[pl.BlockSpec((sc, nh * dv), lambda b, c: (b * ns + c, 0)),
                   pl.BlockSpec((1, len(geo), 2 * dv, LANE), lambda b, c: (b, 0, 0, 0))],
        out_shape=[jax.ShapeDtypeStruct((nb * seq, nh * dv), BF16),
                   jax.ShapeDtypeStruct((nb, len(geo), 2 * dv, LANE), F32)],
        scratch_shapes=[pltpu.VMEM((len(geo), 2 * dv, LANE), F32)],
        compiler_params=_cparams(("parallel", "arbitrary")),
        name="chunk_" + mode,
    )(slab, mats, masks, ind, p1, p2, ngt)
    heads = []
    for p, (_, _, offs) in enumerate(geo):
        for j in range(2):
            heads.append(jnp.swapaxes(st[:, p, j * dv:(j + 1) * dv, offs[j]:offs[j] + dk], -1, -2))
    return o, jnp.stack(heads, axis=1)


def _dn_chunk_kernel(nch, slab_ref, abr_ref, lm_ref, umb_ref, ind_ref, cw_ref, pcol_ref, prow_ref, ng_ref,
                     o_ref, st_ref, s_scr, ext_scr):
    c = CHUNK
    sc = nch * c
    nh, dk, dv = H_DN, DK_DN, DV_DN
    npair = nh // 2
    ci = pl.program_id(1)

    @pl.when(ci == 0)
    def _():
        s_scr[...] = jnp.zeros_like(s_scr)
        ext_scr[0:8, :] = jnp.zeros((8, CONV_DIM), F32)

    ext_scr[8:8 + sc, :] = slab_ref[:, 0:CONV_DIM]
    conv = ext_scr[5:5 + sc, :] * cw_ref[0:1, :]
    for i in range(1, CONV_W):
        conv = conv + ext_scr[5 + i:5 + i + sc, :] * cw_ref[i:i + 1, :]
    ext_scr[0:8, :] = ext_scr[sc:sc + 8, :]
    qkv = _silu(conv)
    gate = slab_ref[:, CONV_DIM:DN_AB]
    hk = nh * dk
    q = qkv[:, 0:hk]
    k = qkv[:, hk:2 * hk]
    v = qkv[:, 2 * hk:2 * hk + nh * dv]
    q = q * lax.rsqrt(_seg_sum(q * q, ind_ref[...]) + EPS) * (dk ** -0.5)
    k = k * lax.rsqrt(_seg_sum(k * k, ind_ref[...]) + EPS)

    ab_c = slab_ref[:, DN_AB:DN_AB + LANE]
    g_c = -jnp.exp(pcol_ref[0:1, :]) * _softplus(ab_c + pcol_ref[1:2, :])
    beta_c = _sigmoid(ab_c)
    g_r = -jnp.exp(prow_ref[0]) * _softplus(abr_ref[0, 0] + prow_ref[1])
    gcum_r = _dot_sel_r(g_r, umb_ref[...])

    bf = lambda t: t.astype(BF16)
    rows = [slice(i * c, (i + 1) * c) for i in range(nch)]
    ri = lax.broadcasted_iota(jnp.int32, (2 * c, 2 * c), 0)
    cj = lax.broadcasted_iota(jnp.int32, (2 * c, 2 * c), 1)
    same = (ri < c) == (cj < c)
    tri = same & (ri >= cj)
    strict = same & (ri > cj)
    eye = (ri == cj).astype(F32)
    lane = lax.broadcasted_iota(jnp.int32, (c, LANE), 1)
    first = lane < dk
    zero = jnp.zeros((c, LANE), BF16)
    gcum_cs = [_dot_sel_l(lm_ref[...], g_c[rows[i]]) for i in range(nch)]

    def lanes2(col0, col1):
        return jnp.where(first, jnp.broadcast_to(col0, (c, LANE)), jnp.broadcast_to(col1, (c, LANE)))

    def stack2(col0, col1):
        return jnp.concatenate([jnp.broadcast_to(col0, (c, LANE)), jnp.broadcast_to(col1, (c, LANE))], axis=0)

    def rows2(x):
        return jnp.concatenate([jnp.where(first, x, zero), jnp.where(first, zero, x)], axis=0)

    units = [(i, p) for i in range(nch) for p in range(npair)]
    pre = {}
    for (i, p) in units:
        h0, h1 = 2 * p, 2 * p + 1
        blk = slice(p * LANE, (p + 1) * LANE)
        gcc = gcum_cs[i]
        gc0, gc1 = gcc[:, h0:h0 + 1], gcc[:, h1:h1 + 1]
        gl0, gl1 = gcc[c - 1:c, h0:h0 + 1], gcc[c - 1:c, h1:h1 + 1]
        b0, b1 = beta_c[rows[i], nh + h0:nh + h0 + 1], beta_c[rows[i], nh + h1:nh + h1 + 1]
        gcr = gcum_r[p:p + 1, i * LANE:(i + 1) * LANE]
        gam = jnp.where(tri, jnp.exp(jnp.where(tri, stack2(gc0, gc1) - gcr, 0.0)), 0.0)
        kc, qc, vc = k[rows[i], blk], q[rows[i], blk], v[rows[i], blk]
        beta_l = lanes2(b0, b1)
        egc_l = jnp.exp(lanes2(gc0, gc1))
        kb = kc * beta_l
        pre[(i, p)] = dict(
            gam=gam, k2=rows2(bf(kc)), kb2=rows2(bf(kb)), q2=rows2(bf(qc)), vb2=rows2(bf(vc * beta_l)),
            ke2=rows2(bf(kb * egc_l)), qe=bf(qc * egc_l),
            kd2=rows2(bf(kc * jnp.exp(lanes2(gl0, gl1) - lanes2(gc0, gc1)))),
            eglast=jnp.exp(stack2(gl0, gl1)))
    ms = {u: jnp.where(strict, _dot_nt(pre[u]["kb2"], pre[u]["k2"]) * pre[u]["gam"], 0.0) for u in units}
    tinv = {u: eye - ms[u] for u in units}
    pw = {u: bf(ms[u]) for u in units}
    for _ in range(int(math.log2(c)) - 1):
        pw = {u: bf(_dot(pw[u], pw[u])) for u in units}
        tinv = {u: tinv[u] + _dot(bf(tinv[u]), pw[u]) for u in units}
    tb = {u: bf(tinv[u]) for u in units}
    uu = {u: _dot(tb[u], pre[u]["vb2"]) for u in units}
    ww = {u: bf(_dot(tb[u], pre[u]["ke2"])) for u in units}
    att = {u: bf(_dot_nt(pre[u]["q2"], pre[u]["k2"]) * pre[u]["gam"]) for u in units}

    out_cols = []
    for p in range(npair):
        s = s_scr[p]
        pieces = []
        for i in range(nch):
            u = (i, p)
            sb = bf(s)
            vn = bf(uu[u] - _dot(ww[u], sb))
            o2 = _dot(att[u], vn)
            pieces.append(_dot(pre[u]["qe"], sb) + o2[0:c] + o2[c:2 * c])
            s = s * pre[u]["eglast"] + _dot_tn(pre[u]["kd2"], vn)
        s_scr[p] = s
        out_cols.append(jnp.concatenate(pieces, axis=0))
    o = jnp.concatenate(out_cols, axis=-1)
    ms_o = _seg_sum(o * o, ind_ref[...]) * (1.0 / dv)
    o_ref[...] = (o * lax.rsqrt(ms_o + EPS) * ng_ref[...] * _silu(gate)).astype(o_ref.dtype)

    @pl.when(ci == pl.num_programs(1) - 1)
    def _():
        st_ref[0] = s_scr[...]


def _dn_chunk(slab, nb, seq, conv_w, a_log, dt_bias, ng):
    c = CHUNK
    nch = math.gcd(seq // c, DN_CHUNKS_PER_STEP)
    sc = nch * c
    ns = seq // sc
    npair = H_DN // 2
    nrow = 16
    idx = np.arange(c)
    lm = jnp.asarray((idx[None, :] <= idx[:, None]), BF16)
    um = (idx[:, None] <= idx[None, :]).astype(np.float32)
    umb = jnp.asarray(np.kron(np.eye(2 * nch, dtype=np.float32), um), BF16)
    ind = _seg_ind(H_DN, DK_DN)
    cw = jnp.pad(conv_w, ((0, 8 - CONV_W), (0, 0)))
    pad = lambda p: jnp.pad(p, (0, LANE - H_DN))
    pcol = jnp.zeros((8, LANE), F32).at[0].set(pad(a_log)).at[1].set(pad(dt_bias))
    a_cols = slab[:, DN_AB:DN_AB + H_DN].reshape(nb, ns, nch, c, npair, 2)
    a_rows = jnp.transpose(a_cols, (0, 1, 4, 2, 5, 3)).reshape(nb, ns, npair, nch * 2 * c)
    a_rows = jnp.pad(a_rows, ((0, 0), (0, 0), (0, nrow - npair), (0, 0)))
    rowp = lambda p: jnp.pad(jnp.broadcast_to(p.reshape(npair, 1, 2, 1), (npair, nch, 2, c)).reshape(npair, nch * 2 * c),
                             ((0, nrow - npair), (0, 0)))
    prow = jnp.stack([rowp(a_log), rowp(dt_bias)])
    ngt = jnp.tile(ng, (1, H_DN))
    fix2 = lambda b, c_: (0, 0)
    o, st = pl.pallas_call(
        functools.partial(_dn_chunk_kernel, nch),
        grid=(nb, ns),
        in_specs=[pl.BlockSpec((sc, DN_W), lambda b, c_: (b * ns + c_, 0)),
                  pl.BlockSpec((1, 1, nrow, nch * 2 * c), lambda b, c_: (b, c_, 0, 0)),
                  pl.BlockSpec((c, c), fix2), pl.BlockSpec(umb.shape, fix2), pl.BlockSpec(ind.shape, fix2),
                  pl.BlockSpec((8, CONV_DIM), fix2), pl.BlockSpec((8, LANE), fix2),
                  pl.BlockSpec(prow.shape, lambda b, c_: (0, 0, 0)), pl.BlockSpec(ngt.shape, fix2)],
        out_specs=[pl.BlockSpec((sc, H_DN * DV_DN), lambda b, c_: (b * ns + c_, 0)),
                   pl.BlockSpec((1, npair, 2 * DK_DN, 2 * DV_DN), lambda b, c_: (b, 0, 0, 0))],
        out_shape=[jax.ShapeDtypeStruct((nb * seq, H_DN * DV_DN), BF16),
                   jax.ShapeDtypeStruct((nb, npair, 2 * DK_DN, 2 * DV_DN), F32)],
        scratch_shapes=[pltpu.VMEM((npair, 2 * DK_DN, 2 * DV_DN), F32), pltpu.VMEM((sc + 8, CONV_DIM), F32)],
        compiler_params=_cparams(("parallel", "arbitrary")),
        name="chunk_dn",
    )(slab, a_rows, lm, umb, ind, cw, pcol, prow, ngt)
    heads = [st[:, p, j * DK_DN:(j + 1) * DK_DN, j * DV_DN:(j + 1) * DV_DN] for p in range(npair) for j in range(2)]
    return o, jnp.stack(heads, axis=1)


def _decode_gla_kernel(q_ref, k_ref, v_ref, g_ref, alr_ref, wa2t_ref, ba_ref, ng_ref, s_ref, o_ref, so_ref):
    dk = q_ref.shape[1]
    z = _dot_hp(wa2t_ref[0], alr_ref[...]) + ba_ref[0]
    dec = jnp.exp(_log_sigmoid(z) * (1.0 / GLA_TAU))
    q = q_ref[0] * (dk ** -0.5)
    k = k_ref[0]
    v = v_ref[0]
    acc = jnp.zeros_like(v)
    for d in range(dk):
        s_new = s_ref[0, d] * dec[d:d + 1, :] + k[d:d + 1, :] * v
        so_ref[0, d] = s_new
        acc = acc + q[d:d + 1, :] * s_new
    ms = jnp.mean(acc * acc, axis=0, keepdims=True)
    o_ref[0] = acc * lax.rsqrt(ms + EPS) * ng_ref[...] * _silu(g_ref[0])


def _decode_dn_kernel(x_ref, cb_ref, cw_ref, a_ref, b_ref, p_ref, g_ref, ng_ref, s_ref, o_ref, so_ref):
    dk = s_ref.shape[1]
    conv = x_ref[:, 0] * cw_ref[CONV_W - 1, :, 0]
    for i in range(CONV_W - 1):
        conv = conv + cb_ref[i, :, 0] * cw_ref[i, :, 0]
    qkv = _silu(conv)
    q, k, v = qkv[0], qkv[1], qkv[2]
    q = q * lax.rsqrt(jnp.sum(q * q, axis=0, keepdims=True) + EPS) * (dk ** -0.5)
    k = k * lax.rsqrt(jnp.sum(k * k, axis=0, keepdims=True) + EPS)
    eg = jnp.exp(-jnp.exp(p_ref[0, 0:1, :]) * _softplus(a_ref[0] + p_ref[0, 1:2, :]))
    beta = _sigmoid(b_ref[0])
    ks = jnp.zeros_like(v)
    for d in range(dk):
        ks = ks + k[d:d + 1, :] * s_ref[0, d]
    v_new = beta * (v - eg * ks)
    acc = jnp.zeros_like(v)
    for d in range(dk):
        s_new = s_ref[0, d] * eg + k[d:d + 1, :] * v_new
        so_ref[0, d] = s_new
        acc = acc + q[d:d + 1, :] * s_new
    ms = jnp.mean(acc * acc, axis=0, keepdims=True)
    o_ref[0] = acc * lax.rsqrt(ms + EPS) * ng_ref[...] * _silu(g_ref[0])


def _decode_hg_kernel(q_ref, f_ref, v_ref, g_ref, lb_ref, ng_ref, s_ref, o_ref, so_ref):
    dk = q_ref.shape[1]
    x = f_ref[0]
    lb = lb_ref[0]
    f = jnp.exp(_log_sigmoid(x) + jnp.log1p(lb * jnp.exp(-x)))
    k = (1.0 - lb) * _sigmoid(-x)
    q = _silu(q_ref[0])
    v = v_ref[0]
    acc = jnp.zeros_like(v)
    for d in range(dk):
        s_new = s_ref[0, d] * f[d:d + 1, :] + k[d:d + 1, :] * v
        so_ref[0, d] = s_new
        acc = acc + q[d:d + 1, :] * s_new
    acc = acc * _sigmoid(g_ref[0])
    ms = jnp.mean(acc * acc, axis=0, keepdims=True)
    o_ref[0] = acc * lax.rsqrt(ms + EPS) * ng_ref[...]


def _head_call(kern, name, nh, dk, dv, nb, args, specs):
    o, s = pl.pallas_call(
        kern,
        grid=(nh,),
        in_specs=specs + [pl.BlockSpec((1, dk, dv, nb), lambda h: (h, 0, 0, 0))],
        out_specs=[pl.BlockSpec((1, dv, nb), lambda h: (h, 0, 0)),
                   pl.BlockSpec((1, dk, dv, nb), lambda h: (h, 0, 0, 0))],
        out_shape=[jax.ShapeDtypeStruct((nh, dv, nb), F32), jax.ShapeDtypeStruct((nh, dk, dv, nb), F32)],
        compiler_params=_cparams(("parallel",)),
        name=name,
    )(*args)
    return o.reshape(nh * dv, nb), jnp.transpose(s, (3, 0, 1, 2))


def _decode(pa, pb, pc, s_gla, s_dn, s_conv, s_hg, wa2, ba, conv_w, a_log, dt_bias, lb, nga, ngb, ngc):
    nb = pa.shape[0]
    bl = lambda p, *shape: jnp.broadcast_to(p.reshape(shape + (1,)), shape + (nb,))
    tr = lambda s: jnp.transpose(s, (1, 2, 3, 0))
    byh = lambda n: pl.BlockSpec((1, n, nb), lambda h: (h, 0, 0))
    fixed = lambda shape: pl.BlockSpec(shape, lambda h: (0,) * len(shape))

    nh, dk, dv = H_GLA, DK_GLA, DV_GLA
    hk, hv = nh * dk, nh * dv
    pt = pa.T
    args = (pt[0:hk].reshape(nh, dk, nb), pt[hk:2 * hk].reshape(nh, dk, nb),
            pt[2 * hk:2 * hk + hv].reshape(nh, dv, nb), pt[2 * hk + hv:2 * hk + 2 * hv].reshape(nh, dv, nb),
            pt[2 * hk + 2 * hv:2 * hk + 2 * hv + GLA_RANK], wa2.T.reshape(nh, dk, GLA_RANK),
            bl(ba, nh, dk), bl(nga, dv), tr(s_gla))
    specs = [byh(dk), byh(dk), byh(dv), byh(dv), fixed((GLA_RANK, nb)),
             pl.BlockSpec((1, dk, GLA_RANK), lambda h: (h, 0, 0)), byh(dk), fixed((dv, nb))]
    o_a, sa = _head_call(_decode_gla_kernel, "decode_gla", nh, dk, dv, nb, args, specs)

    nh, dk, dv = H_DN, DK_DN, DV_DN
    pt = pb.T
    x = pt[0:CONV_DIM].reshape(3, nh, dk, nb)
    cb = jnp.transpose(s_conv, (1, 2, 0)).reshape(CONV_W - 1, 3, nh, dk, nb)
    cw = bl(conv_w, CONV_W, 3, nh, dk)
    prm = jnp.stack([bl(a_log, nh), bl(dt_bias, nh)], axis=1)
    args = (x, cb, cw, pt[DN_AB:DN_AB + nh].reshape(nh, 1, nb), pt[DN_AB + nh:DN_AB + 2 * nh].reshape(nh, 1, nb),
            prm, pt[CONV_DIM:DN_AB].reshape(nh, dv, nb), bl(ngb, dv), tr(s_dn))
    specs = [pl.BlockSpec((3, 1, dk, nb), lambda h: (0, h, 0, 0)),
             pl.BlockSpec((CONV_W - 1, 3, 1, dk, nb), lambda h: (0, 0, h, 0, 0)),
             pl.BlockSpec((CONV_W, 3, 1, dk, nb), lambda h: (0, 0, h, 0, 0)),
             byh(1), byh(1), byh(2), byh(dv), fixed((dv, nb))]
    o_b, sb = _head_call(_decode_dn_kernel, "decode_dn", nh, dk, dv, nb, args, specs)

    nh, dk, dv = H_HG, DK_HG, DV_HG
    hk = nh * dk
    pt = pc.T
    args = (pt[0:hk].reshape(nh, dk, nb), pt[hk:2 * hk].reshape(nh, dk, nb),
            pt[2 * hk:3 * hk].reshape(nh, dv, nb), pt[3 * hk:4 * hk].reshape(nh, dv, nb),
            bl(lb, nh, dk), bl(ngc, dv), tr(s_hg))
    specs = [byh(dk), byh(dk), byh(dv), byh(dv), byh(dk), fixed((dv, nb))]
    o_c, sc = _head_call(_decode_hg_kernel, "decode_hgrn2", nh, dk, dv, nb, args, specs)

    return jnp.concatenate([o_a, o_b, o_c], axis=0).T.astype(BF16), sa, sb, sc


def _outproj_router_kernel(x_ref, oa_ref, ob_ref, oc_ref, w_ref, g_ref, rw_ref, rb_ref, ui_ref,
                           x1_ref, xa_ref, slot_ref, cnt_ref):
    tm, d = x_ref.shape
    na, nb_ = oa_ref.shape[1], ob_ref.shape[1]
    x1 = (x_ref[...] + _dot(oa_ref[...], w_ref[0:na, :]) + _dot(ob_ref[...], w_ref[na:na + nb_, :])
          + _dot(oc_ref[...], w_ref[na + nb_:, :]))
    x1_ref[...] = x1
    h2 = _rms(x1, g_ref[...]).astype(BF16)
    xa_ref[:, 0:d] = h2
    logits = _dot(h2, rw_ref[...]) + rb_ref[...]
    lane = lax.broadcasted_iota(jnp.int32, logits.shape, 1)
    neg = jnp.float32(-jnp.inf)
    big = jnp.int32(1 << 20)
    is_g = (lane >= N_EXPERTS) & (lane < N_EXPERTS + N_GROUPS)
    lg = jnp.where(is_g, logits, neg)
    mg = jnp.max(lg, axis=-1, keepdims=True)
    pg_top = 1.0 / jnp.sum(jnp.where(is_g, jnp.exp(lg - mg), 0.0), axis=-1, keepdims=True)
    g_idx = jnp.min(jnp.where(lg == mg, lane, big), axis=-1, keepdims=True) - N_EXPERTS
    in_grp = (lane >= g_idx * EXP_PER_GROUP) & (lane < (g_idx + 1) * EXP_PER_GROUP)
    le = jnp.where(in_grp, logits, neg)
    me = jnp.max(le, axis=-1, keepdims=True)
    ex = jnp.where(in_grp, jnp.exp(le - me), 0.0)
    pe = ex / jnp.sum(ex, axis=-1, keepdims=True)
    pe = jnp.where(in_grp, pe, -1.0)
    v1 = jnp.max(pe, axis=-1, keepdims=True)
    i1 = jnp.min(jnp.where(pe == v1, lane, big), axis=-1, keepdims=True)
    pe2 = jnp.where(lane == i1, -1.0, pe)
    v2 = jnp.max(pe2, axis=-1, keepdims=True)
    i2 = jnp.min(jnp.where(pe2 == v2, lane, big), axis=-1, keepdims=True)
    tot = v1 + v2
    gate = pg_top * (jnp.where(lane == i1, v1 / tot, 0.0) + jnp.where(lane == i2, v2 / tot, 0.0))
    hi, mid, lo = _split3(gate)
    xa_ref[:, d:d + LANE] = (hi.astype(F32) + pltpu.roll(mid.astype(F32), N_EXPERTS, 1)
                             + pltpu.roll(lo.astype(F32), 2 * N_EXPERTS, 1)).astype(BF16)
    ind = (lane == g_idx).astype(BF16)
    both = _dot_tn(ind, ui_ref[...])
    slot_ref[0] = jnp.where(both[0:8, tm:2 * tm] > 0.5, both[0:8, 0:tm], -1.0)
    cnt_ref[0] = _dot(jnp.ones((8, tm), BF16), ind).astype(jnp.int32)


def _outproj_router(x, oa, ob, oc, w, g, rw, rb, tm):
    t, d = x.shape
    nt = t // tm
    idx = np.arange(tm)
    ui = jnp.asarray(np.concatenate([idx[:, None] < idx[None, :], np.eye(tm, dtype=bool)], axis=1), BF16)
    row = lambda i: (i, 0)
    fix = lambda i: (0, 0)
    x1, xa, slot, cnt = pl.pallas_call(
        _outproj_router_kernel,
        grid=(nt,),
        in_specs=[pl.BlockSpec((tm, d), row), pl.BlockSpec((tm, oa.shape[1]), row),
                  pl.BlockSpec((tm, ob.shape[1]), row), pl.BlockSpec((tm, oc.shape[1]), row),
                  pl.BlockSpec(w.shape, fix), pl.BlockSpec((1, d), fix),
                  pl.BlockSpec(rw.shape, fix), pl.BlockSpec((1, LANE), fix), pl.BlockSpec(ui.shape, fix)],
        out_specs=[pl.BlockSpec((tm, d), row), pl.BlockSpec((tm, d + LANE), row),
                   pl.BlockSpec((1, 8, tm), lambda i: (i, 0, 0)), pl.BlockSpec((1, 8, LANE), lambda i: (i, 0, 0))],
        out_shape=[jax.ShapeDtypeStruct((t, d), F32), jax.ShapeDtypeStruct((t, d + LANE), BF16),
                   jax.ShapeDtypeStruct((nt, 8, tm), F32), jax.ShapeDtypeStruct((nt, 8, LANE), jnp.int32)],
        compiler_params=_cparams(("parallel",)),
        name="outproj_router",
    )(x, oa, ob, oc, w, g, rw, rb, ui)
    return x1, xa, slot[:, 0:N_GROUPS, None, :], cnt[:, 0, 0:N_GROUPS]


def _moe_kernel(final, rb0, rbx, cnt_ref, x1_ref, xa_ref, slot_ref, ex_ref, w1_ref, w3_ref, w2_ref, fg_ref, y_ref):
    i, g = pl.program_id(0), pl.program_id(1)
    tm, d = x1_ref.shape

    @pl.when(g == 0)
    def _():
        y_ref[...] = x1_ref[...]

    cnt = cnt_ref[i, g]
    slot = slot_ref[0, 0]

    def rows(row0, rb):
        rid = (lax.broadcasted_iota(jnp.int32, (rb, tm), 0) + row0).astype(F32)
        sel = (rid == slot).astype(BF16)
        xg = _dot(sel, xa_ref[...])
        xb = xg[:, 0:d].astype(BF16)
        gexp = _dot(xg[:, d:d + LANE].astype(BF16), ex_ref[0])
        hid = _silu(_dot(xb, w1_ref[0])) * _dot(xb, w3_ref[0]) * gexp
        yg = _dot(hid.astype(BF16), w2_ref[0])
        y_ref[...] += _dot_tn(sel, yg.astype(BF16))

    @pl.when(cnt > 0)
    def _():
        rows(0, rb0)

    def extra(j, carry):
        rows(rb0 + j * rbx, rbx)
        return carry

    lax.fori_loop(0, (jnp.maximum(cnt - rb0, 0) + rbx - 1) // rbx, extra, 0)

    if final:
        @pl.when(g == pl.num_programs(1) - 1)
        def _():
            y_ref[...] = _rms(y_ref[...], fg_ref[...])


def _group_weights(w1, w3, w2):
    ne, d, f = w1.shape
    side = lambda w: jnp.transpose(w.astype(BF16).reshape(N_GROUPS, EXP_PER_GROUP, d, f), (0, 2, 1, 3)).reshape(
        N_GROUPS, d, EXP_PER_GROUP * f)
    return side(w1), side(w3), w2.astype(BF16).reshape(N_GROUPS, EXP_PER_GROUP * f, d)


def _gate_expand(f):
    r = np.arange(LANE)
    e = r % N_EXPERTS
    col_e = np.arange(EXP_PER_GROUP * f) // f
    m = [(r[:, None] < 3 * N_EXPERTS) & (e[:, None] == g * EXP_PER_GROUP + col_e[None, :]) for g in range(N_GROUPS)]
    return jnp.asarray(np.stack(m), BF16)


def _moe(x1, xa, slot, cnt, w1g, w3g, w2g, expand, fg, final, tm):
    t, d = x1.shape
    gf = w1g.shape[2]
    rb0 = min(tm, MOE_ROWS_FIRST)
    rbx = min(tm, MOE_ROWS_EXTRA)
    row = lambda i, g, c: (i, 0)
    grp = lambda i, g, c: (g, 0, 0)
    return pl.pallas_call(
        functools.partial(_moe_kernel, final, rb0, rbx),
        grid_spec=pltpu.PrefetchScalarGridSpec(
            num_scalar_prefetch=1,
            grid=(t // tm, N_GROUPS),
            in_specs=[pl.BlockSpec((tm, d), row), pl.BlockSpec((tm, d + LANE), row),
                      pl.BlockSpec((1, 1, 1, tm), lambda i, g, c: (i, g, 0, 0)),
                      pl.BlockSpec((1, LANE, gf), grp),
                      pl.BlockSpec((1, d, gf), grp), pl.BlockSpec((1, d, gf), grp), pl.BlockSpec((1, gf, d), grp),
                      pl.BlockSpec((1, d), lambda i, g, c: (0, 0))],
            out_specs=pl.BlockSpec((tm, d), row)),
        out_shape=jax.ShapeDtypeStruct((t, d), F32),
        compiler_params=_cparams(("parallel", "arbitrary")),
        name="moe",
    )(cnt, x1, xa, slot, expand, w1g, w3g, w2g, fg)


def kernel(x_prompt, x_sample, state_gla, state_dn, state_conv, state_hgrn, norm1_g, w_in, gla_wa2, gla_ba, gla_norm_g, dn_conv_w, dn_a_log, dn_dt_bias, dn_norm_g, hg_lb_logits, hg_norm_g, w_out, norm2_g, router_g_w, router_g_b, router_e_w, router_e_b, exp_w1, exp_w3, exp_w2, final_norm_g):
    nbp, seq, d = x_prompt.shape
    nbs = x_sample.shape[0]
    depth = w_in.shape[0]
    assert x_sample.shape[1] == 1 and seq % CHUNK == 0
    tp = nbp * seq
    xp = x_prompt.reshape(tp, d)
    xs = x_sample.reshape(nbs, d)

    sm = jax.nn.softmax(hg_lb_logits.astype(F32), axis=0)
    lb_all = jnp.maximum(jnp.cumsum(sm, axis=0) - sm[0:1], 0.0)

    tm_p = min(512, tp)
    tm_moe = min(MOE_TILE, tp)
    expand = _gate_expand(exp_w1.shape[3])
    row2 = lambda v: v.reshape(1, -1)
    gla_p, dn_p, conv_p, hg_p, gla_s, dn_s, conv_s, hg_s = ([] for _ in range(8))
    for li in range(depth):
        w_in_b = _perm_w_in(w_in[li])
        w_out_b = w_out[li].astype(BF16)
        rw = jnp.pad(jnp.concatenate([router_e_w[li], router_g_w[li]], axis=1),
                     ((0, 0), (0, LANE - N_EXPERTS - N_GROUPS))).astype(BF16)
        rb = jnp.pad(jnp.concatenate([router_e_b[li], router_g_b[li]]), (0, LANE - N_EXPERTS - N_GROUPS)).reshape(1, LANE)
        w1g, w3g, w2g = _group_weights(exp_w1[li], exp_w3[li], exp_w2[li])
        final = li == depth - 1
        g1, g2, fg = row2(norm1_g[li]), row2(norm2_g[li]), row2(final_norm_g)

        pa, pb, pc = _inproj(xp, g1, w_in_b, tm_p)
        oa, sa = _gla_chunk("gla", pa, nbp, seq, gla_wa2[li], row2(gla_ba[li]), row2(gla_norm_g[li]))
        ob, sb = _dn_chunk(pb, nbp, seq, dn_conv_w[li], dn_a_log[li], dn_dt_bias[li], row2(dn_norm_g[li]))
        oc, sc = _gla_chunk("hgrn2", pc, nbp, seq, row2(lb_all[li]), row2(lb_all[li]), row2(hg_norm_g[li]))
        gla_p.append(sa)
        dn_p.append(sb)
        conv_p.append(pb.reshape(nbp, seq, DN_W)[:, seq - (CONV_W - 1):, 0:CONV_DIM])
        hg_p.append(sc)
        x1, xa, slot, cnt = _outproj_router(xp, oa, ob, oc, w_out_b, g2, rw, rb, tm_moe)
        xp = _moe(x1, xa, slot, cnt, w1g, w3g, w2g, expand, fg, final, tm_moe)

        qa, qb, qc = _inproj(xs, g1, w_in_b, nbs)
        o_s, sa, sb, sc = _decode(qa, qb, qc, state_gla[li], state_dn[li], state_conv[li], state_hgrn[li],
                                  gla_wa2[li], gla_ba[li], dn_conv_w[li], dn_a_log[li], dn_dt_bias[li],
                                  lb_all[li], gla_norm_g[li], dn_norm_g[li], hg_norm_g[li])
        gla_s.append(sa)
        dn_s.append(sb)
        conv_s.append(jnp.concatenate([state_conv[li][:, 1:], qb[:, None, 0:CONV_DIM]], axis=1))
        hg_s.append(sc)
        na, nb_ = H_GLA * DV_GLA, H_DN * DV_DN
        x1, xa, slot, cnt = _outproj_router(xs, o_s[:, 0:na], o_s[:, na:na + nb_], o_s[:, na + nb_:], w_out_b, g2, rw, rb, nbs)
        xs = _moe(x1, xa, slot, cnt, w1g, w3g, w2g, expand, fg, final, nbs)

    st = lambda xs_, ref: jnp.stack(xs_).astype(ref.dtype)
    return (xp.reshape(nbp, seq, d), xs.reshape(nbs, 1, d),
            st(gla_p, state_gla), st(dn_p, state_dn), st(conv_p, state_conv), st(hg_p, state_hgrn),
            st(gla_s, state_gla), st(dn_s, state_dn), st(conv_s, state_conv), st(hg_s, state_hgrn))
```

```python
import functools
import math

import numpy as np
import jax
import jax.numpy as jnp
from jax import lax
from jax.experimental import pallas as pl
from jax.experimental.pallas import tpu as pltpu

F32 = jnp.float32
BF16 = jnp.bfloat16
EPS = 1e-6

H_GLA, DK_GLA, DV_GLA, GLA_RANK, GLA_TAU = 6, 32, 64, 16, 16.0
H_DN, DK_DN, DV_DN, CONV_W = 6, 64, 64, 4
CONV_DIM = H_DN * (2 * DK_DN + DV_DN)
H_HG, DK_HG, DV_HG = 4, 64, 64
N_GROUPS, EXP_PER_GROUP, TOP_K = 4, 8, 2
N_EXPERTS = N_GROUPS * EXP_PER_GROUP

LANE = 128
CHUNK = 64
GLA_CHUNKS_PER_STEP = 8
DN_CHUNKS_PER_STEP = 4
MOE_TILE = 1024
MOE_ROWS_FIRST = 320
MOE_ROWS_EXTRA = 128
VMEM_LIMIT = 56 * 1024 * 1024

GLA_W = 1280
DN_W = 1664
HG_W = 1024
DN_AB = 1536


def _cparams(sem):
    return pltpu.CompilerParams(dimension_semantics=sem, vmem_limit_bytes=VMEM_LIMIT)


def _dot(a, b):
    return jnp.dot(a, b, preferred_element_type=F32)


def _dot_nt(a, b):
    return lax.dot_general(a, b, (((1,), (1,)), ((), ())), preferred_element_type=F32)


def _dot_tn(a, b):
    return lax.dot_general(a, b, (((0,), (0,)), ((), ())), preferred_element_type=F32)


def _split3(x):
    hi = x.astype(BF16)
    r = x - hi.astype(F32)
    mid = r.astype(BF16)
    lo = (r - mid.astype(F32)).astype(BF16)
    return hi, mid, lo


def _split2(x):
    hi = x.astype(BF16)
    lo = (x - hi.astype(F32)).astype(BF16)
    return hi, lo


def _dot_sel_l(m, x):
    hi, mid, lo = _split3(x)
    return _dot(m, hi) + _dot(m, mid) + _dot(m, lo)


def _dot_sel_r(x, m):
    hi, mid, lo = _split3(x)
    return _dot(hi, m) + _dot(mid, m) + _dot(lo, m)


def _dot_hp(a, b, fn=_dot):
    ah, al = _split2(a)
    bh, bl = _split2(b)
    return fn(ah, bh) + fn(ah, bl) + fn(al, bh)


def _rms(x, g):
    return x * lax.rsqrt(jnp.mean(x * x, axis=-1, keepdims=True) + EPS) * g


def _sigmoid(x):
    return 1.0 / (1.0 + jnp.exp(-x))


def _silu(x):
    return x * _sigmoid(x)


def _log_sigmoid(x):
    return jnp.minimum(x, 0.0) - jnp.log1p(jnp.exp(-jnp.abs(x)))


def _softplus(x):
    return jnp.maximum(x, 0.0) + jnp.log1p(jnp.exp(-jnp.abs(x)))


def _inproj_kernel(x_ref, g_ref, w_ref, oa_ref, ob_ref, oc_ref):
    h = _rms(x_ref[...], g_ref[...]).astype(BF16)
    oa_ref[...] = _dot(h, w_ref[:, 0:GLA_W])
    ob_ref[...] = _dot(h, w_ref[:, GLA_W:GLA_W + DN_W])
    oc_ref[...] = _dot(h, w_ref[:, GLA_W + DN_W:GLA_W + DN_W + HG_W])


def _inproj(x, g, w, tm):
    t, d = x.shape
    n = w.shape[1]
    row = lambda i: (i, 0)
    fix = lambda i: (0, 0)
    return pl.pallas_call(
        _inproj_kernel,
        grid=(t // tm,),
        in_specs=[pl.BlockSpec((tm, d), row), pl.BlockSpec((1, d), fix), pl.BlockSpec((d, n), fix)],
        out_specs=[pl.BlockSpec((tm, GLA_W), row), pl.BlockSpec((tm, DN_W), row), pl.BlockSpec((tm, HG_W), row)],
        out_shape=[jax.ShapeDtypeStruct((t, GLA_W), F32), jax.ShapeDtypeStruct((t, DN_W), F32),
                   jax.ShapeDtypeStruct((t, HG_W), F32)],
        compiler_params=_cparams(("parallel",)),
        name="inproj",
    )(x, g, w)


def _perm_w_in(w):
    sizes = (H_GLA * DK_GLA, H_GLA * DK_GLA, H_GLA * DV_GLA, GLA_RANK, H_GLA * DV_GLA,
             CONV_DIM, H_DN, H_DN, H_DN * DV_DN,
             H_HG * DK_HG, H_HG * DK_HG, H_HG * DV_HG, H_HG * DV_HG)
    offs = np.concatenate([[0], np.cumsum(sizes)])
    seg = [w[:, offs[i]:offs[i + 1]] for i in range(len(sizes))]
    g_q, g_k, g_v, g_a, g_g, d_qkv, d_a, d_b, d_g, h_q, h_f, h_i, h_g = seg
    z = lambda n: jnp.zeros((w.shape[0], n), w.dtype)
    cols = [g_q, g_k, g_v, g_g, g_a, z(GLA_W - 1168),
            d_qkv, d_g, d_a, d_b, z(DN_W - 1548),
            h_q, h_f, h_i, h_g]
    return jnp.concatenate(cols, axis=1).astype(BF16)


def _decay_consts(c):
    n = int(math.log2(c))
    idx = np.arange(c)
    lm = (idx[None, :] <= idx[:, None]).astype(np.float32)
    mats, masks = [], []
    for l in range(1, n + 1):
        hs = c >> l
        bs = 2 * hs
        blk = idx // bs
        ref = blk * bs + hs - 1
        mats.append(lm - lm[ref])
        lower = (idx % bs) >= hs
        same = blk[:, None] == blk[None, :]
        masks.append((same & lower[:, None] & (~lower)[None, :]).astype(np.float32))
    mats.append(lm)
    mats.append(1.0 - lm)
    masks.append(np.eye(c, dtype=np.float32))
    masks = np.stack(masks)
    return (jnp.asarray(np.concatenate(mats, 0), BF16), jnp.asarray(np.concatenate([masks, masks], axis=1), F32), n)


def _seg_ind(nh, dv):
    h = np.arange(nh * dv) // dv
    return jnp.asarray(h[:, None] == h[None, :], BF16)


def _seg_sum(x, ind):
    hi, lo = _split2(x)
    return _dot(hi, ind) + _dot(lo, ind)


def _pair_geometry(nh, dk):
    geo = []
    for p in range(nh // 2):
        start = (2 * p * dk) // LANE * LANE
        width = min(LANE, nh * dk - start)
        geo.append((start, width, (2 * p * dk - start, (2 * p + 1) * dk - start)))
    return geo


def _gla_chunk_kernel(mode, nh, dk, dv, nlev, nch, slab_ref, mats_ref, masks_ref, ind_ref, p1_ref, p2_ref, ng_ref,
                      o_ref, st_ref, s_scr):
    c = CHUNK
    ci = pl.program_id(1)

    @pl.when(ci == 0)
    def _():
        s_scr[...] = jnp.zeros_like(s_scr)

    hk, hv = nh * dk, nh * dv
    if mode == "gla":
        q = slab_ref[:, 0:hk] * (dk ** -0.5)
        k = slab_ref[:, hk:2 * hk]
        v = slab_ref[:, 2 * hk:2 * hk + hv]
        gate = slab_ref[:, 2 * hk + hv:2 * hk + 2 * hv]
        a_lr = slab_ref[:, 2 * hk + 2 * hv:2 * hk + 2 * hv + GLA_RANK]
        z = _dot_hp(a_lr, p1_ref[...]) + p2_ref[...]
        la = _log_sigmoid(z) * (1.0 / GLA_TAU)
    else:
        q = _silu(slab_ref[:, 0:hk])
        x = slab_ref[:, hk:2 * hk]
        v = slab_ref[:, 2 * hk:2 * hk + hv]
        gate = slab_ref[:, 2 * hk + hv:2 * hk + 2 * hv]
        lb = p1_ref[...]
        la = _log_sigmoid(x) + jnp.log1p(lb * jnp.exp(-x))
        k = (1.0 - lb) * _sigmoid(-x)

    bf = lambda t: t.astype(BF16)
    geo = _pair_geometry(nh, dk)
    npair = len(geo)
    chunks = range(nch)
    rows = [slice(i * c, (i + 1) * c) for i in chunks]
    vb = bf(v)

    qs, ks, q_in, k_rem, e_last = [], [], [], [], []
    for i in chunks:
        dall = _dot_sel_l(mats_ref[...], la[rows[i]])
        bcum = dall[nlev * c:(nlev + 1) * c]
        wl = [jnp.exp(-jnp.abs(dall[l * c:(l + 1) * c])) for l in range(nlev)]
        qc, kc = q[rows[i]], k[rows[i]]
        qs.append([bf(qc * w) for w in wl] + [bf(qc)])
        ks.append([bf(kc * w) for w in wl] + [bf(kc)])
        q_in.append(bf(qc * jnp.exp(bcum)))
        k_rem.append(bf(kc * jnp.exp(dall[(nlev + 1) * c:(nlev + 2) * c])))
        e_last.append(jnp.exp(bcum[c - 1:c]))

    out_cols = []
    for p in range(npair):
        start, width, offs = geo[p]
        blk = slice(start, start + width)
        vcol = slice(p * LANE, (p + 1) * LANE)
        lane = lax.broadcasted_iota(jnp.int32, (c, width), 1)
        hmask = [(lane >= o) & (lane < o + dk) for o in offs]
        zero = jnp.zeros((c, width), BF16)
        atts = []
        for i in chunks:
            att = None
            for l in range(nlev + 1):
                qb = qs[i][l][:, blk]
                lhs = jnp.concatenate([jnp.where(hmask[0], qb, zero), jnp.where(hmask[1], qb, zero)], axis=0)
                term = _dot_nt(lhs, ks[i][l][:, blk]) * masks_ref[l]
                att = term if att is None else att + term
            atts.append(bf(att))
        lane_v = lax.broadcasted_iota(jnp.int32, (c, LANE), 1)
        first = lane_v < dv
        intra = []
        for i in chunks:
            r2 = _dot(atts[i], vb[rows[i], vcol])
            intra.append(jnp.where(first, r2[0:c], r2[c:2 * c]))
        ri = lax.broadcasted_iota(jnp.int32, (2 * dv, width), 0)
        li = lax.broadcasted_iota(jnp.int32, (2 * dv, width), 1)
        smask = ((ri < dv) & (li >= offs[0]) & (li < offs[0] + dk)) | ((ri >= dv) & (li >= offs[1]) & (li < offs[1] + dk))
        upd = [jnp.where(smask, _dot_tn(vb[rows[i], vcol], k_rem[i][:, blk]), 0.0) for i in chunks]
        s = s_scr[p, :, 0:width]
        states = []
        for i in chunks:
            states.append(bf(s))
            s = s * e_last[i][:, blk] + upd[i]
        s_scr[p, :, 0:width] = s
        out_cols.append(jnp.concatenate(
            [intra[i] + _dot_nt(q_in[i][:, blk], states[i]) for i in chunks], axis=0))
    o = jnp.concatenate(out_cols, axis=-1)

    if mode == "gla":
        ms = _seg_sum(o * o, ind_ref[...]) * (1.0 / dv)
        o = o * lax.rsqrt(ms + EPS) * ng_ref[...] * _silu(gate)
    else:
        o = o * _sigmoid(gate)
        ms = _seg_sum(o * o, ind_ref[...]) * (1.0 / dv)
        o = o * lax.rsqrt(ms + EPS) * ng_ref[...]
    o_ref[...] = o.astype(o_ref.dtype)

    @pl.when(ci == pl.num_programs(1) - 1)
    def _():
        st_ref[0] = s_scr[...]


def _gla_chunk(mode, slab, nb, seq, p1, p2, ng):
    nh, dk, dv = (H_GLA, DK_GLA, DV_GLA) if mode == "gla" else (H_HG, DK_HG, DV_HG)
    w = slab.shape[1]
    nch = math.gcd(seq // CHUNK, GLA_CHUNKS_PER_STEP)
    sc = nch * CHUNK
    ns = seq // sc
    mats, masks, nlev = _decay_consts(CHUNK)
    ind = _seg_ind(nh, dv)
    ngt = jnp.tile(ng, (1, nh))
    geo = _pair_geometry(nh, dk)
    fix2 = lambda b, c: (0, 0)
    kern = functools.partial(_gla_chunk_kernel, mode, nh, dk, dv, nlev, nch)
    o, st = pl.pallas_call(
        kern,
        grid=(nb, ns),
        in_specs=[pl.BlockSpec((sc, w), lambda b, c: (b * ns + c, 0)),
                  pl.BlockSpec(mats.shape, fix2),
                  pl.BlockSpec(masks.shape, lambda b, c: (0, 0, 0)),
                  pl.BlockSpec(ind.shape, fix2),
                  pl.BlockSpec(p1.shape, fix2), pl.BlockSpec(p2.shape, fix2), pl.BlockSpec(ngt.shape, fix2)],
        out_specs=[pl.BlockSpec((sc, nh * dv), lambda b, c: (b * ns + c, 0)),
                   pl.BlockSpec((1, len(geo), 2 * dv, LANE), lambda b, c: (b, 0, 0, 0))],
        out_shape=[jax.ShapeDtypeStruct((nb * seq, nh * dv), BF16),
                   jax.ShapeDtypeStruct((nb, len(geo), 2 * dv, LANE), F32)],
        scratch_shapes=[pltpu.VMEM((len(geo), 2 * dv, LANE), F32)],
        compiler_params=_cparams(("parallel", "arbitrary")),
        name="chunk_" + mode,
    )(slab, mats, masks, ind, p1, p2, ngt)
    heads = []
    for p, (_, _, offs) in enumerate(geo):
        for j in range(2):
            heads.append(jnp.swapaxes(st[:, p, j * dv:(j + 1) * dv, offs[j]:offs[j] + dk], -1, -2))
    return o, jnp.stack(heads, axis=1)


def _dn_chunk_kernel(nch, slab_ref, abr_ref, lm_ref, umb_ref, ind_ref, cw_ref, pcol_ref, prow_ref, ng_ref,
                     o_ref, st_ref, s_scr, ext_scr):
    c = CHUNK
    sc = nch * c
    nh, dk, dv = H_DN, DK_DN, DV_DN
    npair = nh // 2
    ci = pl.program_id(1)

    @pl.when(ci == 0)
    def _():
        s_scr[...] = jnp.zeros_like(s_scr)
        ext_scr[0:8, :] = jnp.zeros((8, CONV_DIM), F32)

    ext_scr[8:8 + sc, :] = slab_ref[:, 0:CONV_DIM]
    conv = ext_scr[5:5 + sc, :] * cw_ref[0:1, :]
    for i in range(1, CONV_W):
        conv = conv + ext_scr[5 + i:5 + i + sc, :] * cw_ref[i:i + 1, :]
    ext_scr[0:8, :] = ext_scr[sc:sc + 8, :]
    qkv = _silu(conv)
    gate = slab_ref[:, CONV_DIM:DN_AB]
    hk = nh * dk
    q = qkv[:, 0:hk]
    k = qkv[:, hk:2 * hk]
    v = qkv[:, 2 * hk:2 * hk + nh * dv]
    q = q * lax.rsqrt(_seg_sum(q * q, ind_ref[...]) + EPS) * (dk ** -0.5)
    k = k * lax.rsqrt(_seg_sum(k * k, ind_ref[...]) + EPS)

    ab_c = slab_ref[:, DN_AB:DN_AB + LANE]
    g_c = -jnp.exp(pcol_ref[0:1, :]) * _softplus(ab_c + pcol_ref[1:2, :])
    beta_c = _sigmoid(ab_c)
    g_r = -jnp.exp(prow_ref[0]) * _softplus(abr_ref[0, 0] + prow_ref[1])
    gcum_r = _dot_sel_r(g_r, umb_ref[...])

    bf = lambda t: t.astype(BF16)
    rows = [slice(i * c, (i + 1) * c) for i in range(nch)]
    ri = lax.broadcasted_iota(jnp.int32, (2 * c, 2 * c), 0)
    cj = lax.broadcasted_iota(jnp.int32, (2 * c, 2 * c), 1)
    same = (ri < c) == (cj < c)
    tri = same & (ri >= cj)
    strict = same & (ri > cj)
    eye = (ri == cj).astype(F32)
    lane = lax.broadcasted_iota(jnp.int32, (c, LANE), 1)
    first = lane < dk
    zero = jnp.zeros((c, LANE), BF16)
    gcum_cs = [_dot_sel_l(lm_ref[...], g_c[rows[i]]) for i in range(nch)]

    def lanes2(col0, col1):
        return jnp.where(first, jnp.broadcast_to(col0, (c, LANE)), jnp.broadcast_to(col1, (c, LANE)))

    def stack2(col0, col1):
        return jnp.concatenate([jnp.broadcast_to(col0, (c, LANE)), jnp.broadcast_to(col1, (c, LANE))], axis=0)

    def rows2(x):
        return jnp.concatenate([jnp.where(first, x, zero), jnp.where(first, zero, x)], axis=0)

    pairs = range(npair)

    def prepare(i, p):
        h0, h1 = 2 * p, 2 * p + 1
        blk = slice(p * LANE, (p + 1) * LANE)
        gcc = gcum_cs[i]
        gc0, gc1 = gcc[:, h0:h0 + 1], gcc[:, h1:h1 + 1]
        gl0, gl1 = gcc[c - 1:c, h0:h0 + 1], gcc[c - 1:c, h1:h1 + 1]
        b0, b1 = beta_c[rows[i], nh + h0:nh + h0 + 1], beta_c[rows[i], nh + h1:nh + h1 + 1]
        gcr = gcum_r[p:p + 1, i * LANE:(i + 1) * LANE]
        gam = jnp.where(tri, jnp.exp(jnp.where(tri, stack2(gc0, gc1) - gcr, 0.0)), 0.0)
        kc, qc, vc = k[rows[i], blk], q[rows[i], blk], v[rows[i], blk]
        beta_l = lanes2(b0, b1)
        egc_l = jnp.exp(lanes2(gc0, gc1))
        kb = kc * beta_l
        return dict(
            gam=gam, k2=rows2(bf(kc)), kb2=rows2(bf(kb)), q2=rows2(bf(qc)), vb2=rows2(bf(vc * beta_l)),
            ke2=rows2(bf(kb * egc_l)), qe=bf(qc * egc_l),
            kd2=rows2(bf(kc * jnp.exp(lanes2(gl0, gl1) - lanes2(gc0, gc1)))),
            eglast=jnp.exp(stack2(gl0, gl1)))

    units = [(i, p) for i in range(nch) for p in pairs]
    pre = {u: prepare(*u) for u in units}
    ms = {u: jnp.where(strict, _dot_nt(pre[u]["kb2"], pre[u]["k2"]) * pre[u]["gam"], 0.0) for u in units}
    tinv = {u: eye - ms[u] for u in units}
    pw = {u: bf(ms[u]) for u in units}
    for _ in range(int(math.log2(c)) - 1):
        pw = {u: bf(_dot(pw[u], pw[u])) for u in units}
        tinv = {u: tinv[u] + _dot(bf(tinv[u]), pw[u]) for u in units}
    tb = {u: bf(tinv[u]) for u in units}
    uu = {u: _dot(tb[u], pre[u]["vb2"]) for u in units}
    ww = {u: bf(_dot(tb[u], pre[u]["ke2"])) for u in units}
    att = {u: bf(_dot_nt(pre[u]["q2"], pre[u]["k2"]) * pre[u]["gam"]) for u in units}

    states = [s_scr[p] for p in pairs]
    pieces = [[] for _ in pairs]
    for i in range(nch):
        for p in pairs:
            u = (i, p)
            sb = bf(states[p])
            vn = bf(uu[u] - _dot(ww[u], sb))
            o2 = _dot(att[u], vn)
            pieces[p].append(_dot(pre[u]["qe"], sb) + o2[0:c] + o2[c:2 * c])
            states[p] = states[p] * pre[u]["eglast"] + _dot_tn(pre[u]["kd2"], vn)
    for p in pairs:
        s_scr[p] = states[p]
    o = jnp.concatenate([jnp.concatenate(pc, axis=0) for pc in pieces], axis=-1)
    ms_o = _seg_sum(o * o, ind_ref[...]) * (1.0 / dv)
    o_ref[...] = (o * lax.rsqrt(ms_o + EPS) * ng_ref[...] * _silu(gate)).astype(o_ref.dtype)

    @pl.when(ci == pl.num_programs(1) - 1)
    def _():
        st_ref[0] = s_scr[...]


def _dn_chunk(slab, nb, seq, conv_w, a_log, dt_bias, ng):
    c = CHUNK
    nch = math.gcd(seq // c, DN_CHUNKS_PER_STEP)
    sc = nch * c
    ns = seq // sc
    npair = H_DN // 2
    nrow = 16
    idx = np.arange(c)
    lm = jnp.asarray((idx[None, :] <= idx[:, None]), BF16)
    um = (idx[:, None] <= idx[None, :]).astype(np.float32)
    umb = jnp.asarray(np.kron(np.eye(2 * nch, dtype=np.float32), um), BF16)
    ind = _seg_ind(H_DN, DK_DN)
    cw = jnp.pad(conv_w, ((0, 8 - CONV_W), (0, 0)))
    pad = lambda p: jnp.pad(p, (0, LANE - H_DN))
    pcol = jnp.zeros((8, LANE), F32).at[0].set(pad(a_log)).at[1].set(pad(dt_bias))
    a_cols = slab[:, DN_AB:DN_AB + H_DN].reshape(nb, ns, nch, c, npair, 2)
    a_rows = jnp.transpose(a_cols, (0, 1, 4, 2, 5, 3)).reshape(nb, ns, npair, nch * 2 * c)
    a_rows = jnp.pad(a_rows, ((0, 0), (0, 0), (0, nrow - npair), (0, 0)))
    rowp = lambda p: jnp.pad(jnp.broadcast_to(p.reshape(npair, 1, 2, 1), (npair, nch, 2, c)).reshape(npair, nch * 2 * c),
                             ((0, nrow - npair), (0, 0)))
    prow = jnp.stack([rowp(a_log), rowp(dt_bias)])
    ngt = jnp.tile(ng, (1, H_DN))
    fix2 = lambda b, c_: (0, 0)
    o, st = pl.pallas_call(
        functools.partial(_dn_chunk_kernel, nch),
        grid=(nb, ns),
        in_specs=[pl.BlockSpec((sc, DN_W), lambda b, c_: (b * ns + c_, 0)),
                  pl.BlockSpec((1, 1, nrow, nch * 2 * c), lambda b, c_: (b, c_, 0, 0)),
                  pl.BlockSpec((c, c), fix2), pl.BlockSpec(umb.shape, fix2), pl.BlockSpec(ind.shape, fix2),
                  pl.BlockSpec((8, CONV_DIM), fix2), pl.BlockSpec((8, LANE), fix2),
                  pl.BlockSpec(prow.shape, lambda b, c_: (0, 0, 0)), pl.BlockSpec(ngt.shape, fix2)],
        out_specs=[pl.BlockSpec((sc, H_DN * DV_DN), lambda b, c_: (b * ns + c_, 0)),
                   pl.BlockSpec((1, npair, 2 * DK_DN, 2 * DV_DN), lambda b, c_: (b, 0, 0, 0))],
        out_shape=[jax.ShapeDtypeStruct((nb * seq, H_DN * DV_DN), BF16),
                   jax.ShapeDtypeStruct((nb, npair, 2 * DK_DN, 2 * DV_DN), F32)],
        scratch_shapes=[pltpu.VMEM((npair, 2 * DK_DN, 2 * DV_DN), F32), pltpu.VMEM((sc + 8, CONV_DIM), F32)],
        compiler_params=_cparams(("parallel", "arbitrary")),
        name="chunk_dn",
    )(slab, a_rows, lm, umb, ind, cw, pcol, prow, ngt)
    heads = [st[:, p, j * DK_DN:(j + 1) * DK_DN, j * DV_DN:(j + 1) * DV_DN] for p in range(npair) for j in range(2)]
    return o, jnp.stack(heads, axis=1)


def _state_in(s_ref, dk):
    st = s_ref[...].T
    return st.reshape(dk, st.shape[0] // dk, st.shape[1])


def _state_out(so_ref, new):
    so_ref[...] = jnp.concatenate(new, axis=0).T


def _decode_gla_kernel(q_ref, k_ref, v_ref, g_ref, alr_ref, wa2t_ref, ba_ref, ng_ref, s_ref, o_ref, so_ref):
    dk = q_ref.shape[1]
    z = _dot_hp(wa2t_ref[0], alr_ref[...]) + ba_ref[0]
    dec = jnp.exp(_log_sigmoid(z) * (1.0 / GLA_TAU))
    q = q_ref[0] * (dk ** -0.5)
    k = k_ref[0]
    v = v_ref[0]
    st = _state_in(s_ref, dk)
    acc = jnp.zeros_like(v)
    new = []
    for d in range(dk):
        s_new = st[d] * dec[d:d + 1, :] + k[d:d + 1, :] * v
        new.append(s_new)
        acc = acc + q[d:d + 1, :] * s_new
    _state_out(so_ref, new)
    ms = jnp.mean(acc * acc, axis=0, keepdims=True)
    o_ref[0] = acc * lax.rsqrt(ms + EPS) * ng_ref[...] * _silu(g_ref[0])


def _decode_dn_kernel(x_ref, cb_ref, cw_ref, a_ref, b_ref, p_ref, g_ref, ng_ref, s_ref, o_ref, so_ref):
    dk = x_ref.shape[2]
    conv = x_ref[:, 0] * cw_ref[CONV_W - 1, :, 0]
    for i in range(CONV_W - 1):
        conv = conv + cb_ref[i, :, 0] * cw_ref[i, :, 0]
    qkv = _silu(conv)
    q, k, v = qkv[0], qkv[1], qkv[2]
    q = q * lax.rsqrt(jnp.sum(q * q, axis=0, keepdims=True) + EPS) * (dk ** -0.5)
    k = k * lax.rsqrt(jnp.sum(k * k, axis=0, keepdims=True) + EPS)
    eg = jnp.exp(-jnp.exp(p_ref[0, 0:1, :]) * _softplus(a_ref[0] + p_ref[0, 1:2, :]))
    beta = _sigmoid(b_ref[0])
    st = _state_in(s_ref, dk)
    ks = jnp.zeros_like(v)
    for d in range(dk):
        ks = ks + k[d:d + 1, :] * st[d]
    v_new = beta * (v - eg * ks)
    acc = jnp.zeros_like(v)
    new = []
    for d in range(dk):
        s_new = st[d] * eg + k[d:d + 1, :] * v_new
        new.append(s_new)
        acc = acc + q[d:d + 1, :] * s_new
    _state_out(so_ref, new)
    ms = jnp.mean(acc * acc, axis=0, keepdims=True)
    o_ref[0] = acc * lax.rsqrt(ms + EPS) * ng_ref[...] * _silu(g_ref[0])


def _decode_hg_kernel(q_ref, f_ref, v_ref, g_ref, lb_ref, ng_ref, s_ref, o_ref, so_ref):
    dk = q_ref.shape[1]
    x = f_ref[0]
    lb = lb_ref[0]
    f = jnp.exp(_log_sigmoid(x) + jnp.log1p(lb * jnp.exp(-x)))
    k = (1.0 - lb) * _sigmoid(-x)
    q = _silu(q_ref[0])
    v = v_ref[0]
    st = _state_in(s_ref, dk)
    acc = jnp.zeros_like(v)
    new = []
    for d in range(dk):
        s_new = st[d] * f[d:d + 1, :] + k[d:d + 1, :] * v
        new.append(s_new)
        acc = acc + q[d:d + 1, :] * s_new
    _state_out(so_ref, new)
    acc = acc * _sigmoid(g_ref[0])
    ms = jnp.mean(acc * acc, axis=0, keepdims=True)
    o_ref[0] = acc * lax.rsqrt(ms + EPS) * ng_ref[...]


def _head_call(kern, name, nh, dk, dv, nb, args, specs):
    o, s = pl.pallas_call(
        kern,
        grid=(nh,),
        in_specs=specs + [pl.BlockSpec((nb, dk * dv), lambda h: (0, h))],
        out_specs=[pl.BlockSpec((1, dv, nb), lambda h: (h, 0, 0)),
                   pl.BlockSpec((nb, dk * dv), lambda h: (0, h))],
        out_shape=[jax.ShapeDtypeStruct((nh, dv, nb), F32), jax.ShapeDtypeStruct((nb, nh * dk * dv), F32)],
        compiler_params=_cparams(("parallel",)),
        name=name,
    )(*args)
    return o.reshape(nh * dv, nb), s.reshape(nb, nh, dk, dv)


def _decode(pa, pb, pc, s_gla, s_dn, s_conv, s_hg, wa2, ba, conv_w, a_log, dt_bias, lb, nga, ngb, ngc):
    nb = pa.shape[0]
    bl = lambda p, *shape: jnp.broadcast_to(p.reshape(shape + (1,)), shape + (nb,))
    tr = lambda s: s.reshape(nb, -1)
    byh = lambda n: pl.BlockSpec((1, n, nb), lambda h: (h, 0, 0))
    fixed = lambda shape: pl.BlockSpec(shape, lambda h: (0,) * len(shape))

    nh, dk, dv = H_GLA, DK_GLA, DV_GLA
    hk, hv = nh * dk, nh * dv
    pt = pa.T
    args = (pt[0:hk].reshape(nh, dk, nb), pt[hk:2 * hk].reshape(nh, dk, nb),
            pt[2 * hk:2 * hk + hv].reshape(nh, dv, nb), pt[2 * hk + hv:2 * hk + 2 * hv].reshape(nh, dv, nb),
            pt[2 * hk + 2 * hv:2 * hk + 2 * hv + GLA_RANK], wa2.T.reshape(nh, dk, GLA_RANK),
            bl(ba, nh, dk), bl(nga, dv), tr(s_gla))
    specs = [byh(dk), byh(dk), byh(dv), byh(dv), fixed((GLA_RANK, nb)),
             pl.BlockSpec((1, dk, GLA_RANK), lambda h: (h, 0, 0)), byh(dk), fixed((dv, nb))]
    o_a, sa = _head_call(_decode_gla_kernel, "decode_gla", nh, dk, dv, nb, args, specs)

    nh, dk, dv = H_DN, DK_DN, DV_DN
    pt = pb.T
    x = pt[0:CONV_DIM].reshape(3, nh, dk, nb)
    cb = jnp.transpose(s_conv, (1, 2, 0)).reshape(CONV_W - 1, 3, nh, dk, nb)
    cw = bl(conv_w, CONV_W, 3, nh, dk)
    prm = jnp.stack([bl(a_log, nh), bl(dt_bias, nh)], axis=1)
    args = (x, cb, cw, pt[DN_AB:DN_AB + nh].reshape(nh, 1, nb), pt[DN_AB + nh:DN_AB + 2 * nh].reshape(nh, 1, nb),
            prm, pt[CONV_DIM:DN_AB].reshape(nh, dv, nb), bl(ngb, dv), tr(s_dn))
    specs = [pl.BlockSpec((3, 1, dk, nb), lambda h: (0, h, 0, 0)),
             pl.BlockSpec((CONV_W - 1, 3, 1, dk, nb), lambda h: (0, 0, h, 0, 0)),
             pl.BlockSpec((CONV_W, 3, 1, dk, nb), lambda h: (0, 0, h, 0, 0)),
             byh(1), byh(1), byh(2), byh(dv), fixed((dv, nb))]
    o_b, sb = _head_call(_decode_dn_kernel, "decode_dn", nh, dk, dv, nb, args, specs)

    nh, dk, dv = H_HG, DK_HG, DV_HG
    hk = nh * dk
    pt = pc.T
    args = (pt[0:hk].reshape(nh, dk, nb), pt[hk:2 * hk].reshape(nh, dk, nb),
            pt[2 * hk:3 * hk].reshape(nh, dv, nb), pt[3 * hk:4 * hk].reshape(nh, dv, nb),
            bl(lb, nh, dk), bl(ngc, dv), tr(s_hg))
    specs = [byh(dk), byh(dk), byh(dv), byh(dv), byh(dk), fixed((dv, nb))]
    o_c, sc = _head_call(_decode_hg_kernel, "decode_hgrn2", nh, dk, dv, nb, args, specs)

    return jnp.concatenate([o_a, o_b, o_c], axis=0).T.astype(BF16), sa, sb, sc


def _outproj_router_kernel(x_ref, oa_ref, ob_ref, oc_ref, w_ref, g_ref, rw_ref, rb_ref, ui_ref,
                           x1_ref, xa_ref, slot_ref, cnt_ref):
    tm, d = x_ref.shape
    na, nb_ = oa_ref.shape[1], ob_ref.shape[1]
    x1 = (x_ref[...] + _dot(oa_ref[...], w_ref[0:na, :]) + _dot(ob_ref[...], w_ref[na:na + nb_, :])
          + _dot(oc_ref[...], w_ref[na + nb_:, :]))
    x1_ref[...] = x1
    h2 = _rms(x1, g_ref[...]).astype(BF16)
    xa_ref[:, 0:d] = h2
    logits = _dot(h2, rw_ref[...]) + rb_ref[...]
    lane = lax.broadcasted_iota(jnp.int32, logits.shape, 1)
    neg = jnp.float32(-jnp.inf)
    big = jnp.int32(1 << 20)
    is_g = (lane >= N_EXPERTS) & (lane < N_EXPERTS + N_GROUPS)
    lg = jnp.where(is_g, logits, neg)
    mg = jnp.max(lg, axis=-1, keepdims=True)
    pg_top = 1.0 / jnp.sum(jnp.where(is_g, jnp.exp(lg - mg), 0.0), axis=-1, keepdims=True)
    g_idx = jnp.min(jnp.where(lg == mg, lane, big), axis=-1, keepdims=True) - N_EXPERTS
    in_grp = (lane >= g_idx * EXP_PER_GROUP) & (lane < (g_idx + 1) * EXP_PER_GROUP)
    le = jnp.where(in_grp, logits, neg)
    me = jnp.max(le, axis=-1, keepdims=True)
    ex = jnp.where(in_grp, jnp.exp(le - me), 0.0)
    pe = ex / jnp.sum(ex, axis=-1, keepdims=True)
    pe = jnp.where(in_grp, pe, -1.0)
    v1 = jnp.max(pe, axis=-1, keepdims=True)
    i1 = jnp.min(jnp.where(pe == v1, lane, big), axis=-1, keepdims=True)
    pe2 = jnp.where(lane == i1, -1.0, pe)
    v2 = jnp.max(pe2, axis=-1, keepdims=True)
    i2 = jnp.min(jnp.where(pe2 == v2, lane, big), axis=-1, keepdims=True)
    tot = v1 + v2
    gate = pg_top * (jnp.where(lane == i1, v1 / tot, 0.0) + jnp.where(lane == i2, v2 / tot, 0.0))
    hi, mid, lo = _split3(gate)
    xa_ref[:, d:d + LANE] = (hi.astype(F32) + pltpu.roll(mid.astype(F32), N_EXPERTS, 1)
                             + pltpu.roll(lo.astype(F32), 2 * N_EXPERTS, 1)).astype(BF16)
    ind = (lane == g_idx).astype(BF16)
    both = _dot_tn(ind, ui_ref[...])
    slot_ref[0] = jnp.where(both[0:8, tm:2 * tm] > 0.5, both[0:8, 0:tm], -1.0)
    cnt_ref[0] = _dot(jnp.ones((8, tm), BF16), ind).astype(jnp.int32)


def _outproj_router(x, oa, ob, oc, w, g, rw, rb, tm):
    t, d = x.shape
    nt = t // tm
    idx = np.arange(tm)
    ui = jnp.asarray(np.concatenate([idx[:, None] < idx[None, :], np.eye(tm, dtype=bool)], axis=1), BF16)
    row = lambda i: (i, 0)
    fix = lambda i: (0, 0)
    x1, xa, slot, cnt = pl.pallas_call(
        _outproj_router_kernel,
        grid=(nt,),
        in_specs=[pl.BlockSpec((tm, d), row), pl.BlockSpec((tm, oa.shape[1]), row),
                  pl.BlockSpec((tm, ob.shape[1]), row), pl.BlockSpec((tm, oc.shape[1]), row),
                  pl.BlockSpec(w.shape, fix), pl.BlockSpec((1, d), fix),
                  pl.BlockSpec(rw.shape, fix), pl.BlockSpec((1, LANE), fix), pl.BlockSpec(ui.shape, fix)],
        out_specs=[pl.BlockSpec((tm, d), row), pl.BlockSpec((tm, d + LANE), row),
                   pl.BlockSpec((1, 8, tm), lambda i: (i, 0, 0)), pl.BlockSpec((1, 8, LANE), lambda i: (i, 0, 0))],
        out_shape=[jax.ShapeDtypeStruct((t, d), F32), jax.ShapeDtypeStruct((t, d + LANE), BF16),
                   jax.ShapeDtypeStruct((nt, 8, tm), F32), jax.ShapeDtypeStruct((nt, 8, LANE), jnp.int32)],
        compiler_params=_cparams(("parallel",)),
        name="outproj_router",
    )(x, oa, ob, oc, w, g, rw, rb, ui)
    return x1, xa, slot[:, 0:N_GROUPS, None, :], cnt[:, 0, 0:N_GROUPS]


def _moe_kernel(final, rb0, rbx, cnt_ref, x1_ref, xa_ref, slot_ref, ex_ref, w1_ref, w3_ref, w2_ref, fg_ref, y_ref):
    i, g = pl.program_id(0), pl.program_id(1)
    tm, d = x1_ref.shape
    ne, _, f = w1_ref.shape

    @pl.when(g == 0)
    def _():
        y_ref[...] = x1_ref[...]

    cnt = cnt_ref[i, g]
    slot = slot_ref[0, 0]

    def rows(row0, rb):
        rid = (lax.broadcasted_iota(jnp.int32, (rb, tm), 0) + row0).astype(F32)
        sel = (rid == slot).astype(BF16)
        xg = _dot(sel, xa_ref[...])
        xb = xg[:, 0:d].astype(BF16)
        gexp = _dot(xg[:, d:d + LANE].astype(BF16), ex_ref[0])
        hid = [(_silu(_dot(xb, w1_ref[e])) * _dot(xb, w3_ref[e]) * gexp[:, e * f:(e + 1) * f]).astype(BF16)
               for e in range(ne)]
        yg = _dot(jnp.concatenate(hid, axis=-1), w2_ref[...].reshape(ne * f, d))
        y_ref[...] += _dot_tn(sel, yg.astype(BF16))

    @pl.when(cnt > 0)
    def _():
        rows(0, rb0)

    def extra(j, carry):
        rows(rb0 + j * rbx, rbx)
        return carry

    lax.fori_loop(0, (jnp.maximum(cnt - rb0, 0) + rbx - 1) // rbx, extra, 0)

    if final:
        @pl.when(g == pl.num_programs(1) - 1)
        def _():
            y_ref[...] = _rms(y_ref[...], fg_ref[...])


def _gate_expand(f):
    r = np.arange(LANE)
    e = r % N_EXPERTS
    col_e = np.arange(EXP_PER_GROUP * f) // f
    m = [(r[:, None] < 3 * N_EXPERTS) & (e[:, None] == g * EXP_PER_GROUP + col_e[None, :]) for g in range(N_GROUPS)]
    return jnp.asarray(np.stack(m), BF16)


def _moe(x1, xa, slot, cnt, w1, w3, w2, expand, fg, final, tm):
    t, d = x1.shape
    f = w1.shape[2]
    gf = EXP_PER_GROUP * f
    rb0 = min(tm, MOE_ROWS_FIRST)
    rbx = min(tm, MOE_ROWS_EXTRA)
    row = lambda i, g, c: (i, 0)
    grp = lambda i, g, c: (g, 0, 0)
    return pl.pallas_call(
        functools.partial(_moe_kernel, final, rb0, rbx),
        grid_spec=pltpu.PrefetchScalarGridSpec(
            num_scalar_prefetch=1,
            grid=(t // tm, N_GROUPS),
            in_specs=[pl.BlockSpec((tm, d), row), pl.BlockSpec((tm, d + LANE), row),
                      pl.BlockSpec((1, 1, 1, tm), lambda i, g, c: (i, g, 0, 0)),
                      pl.BlockSpec((1, LANE, gf), grp),
                      pl.BlockSpec((EXP_PER_GROUP, d, f), grp), pl.BlockSpec((EXP_PER_GROUP, d, f), grp),
                      pl.BlockSpec((EXP_PER_GROUP, f, d), grp),
                      pl.BlockSpec((1, d), lambda i, g, c: (0, 0))],
            out_specs=pl.BlockSpec((tm, d), row)),
        out_shape=jax.ShapeDtypeStruct((t, d), F32),
        compiler_params=_cparams(("parallel", "arbitrary")),
        name="moe",
    )(cnt, x1, xa, slot, expand, w1, w3, w2, fg)


def kernel(x_prompt, x_sample, state_gla, state_dn, state_conv, state_hgrn, norm1_g, w_in, gla_wa2, gla_ba, gla_norm_g, dn_conv_w, dn_a_log, dn_dt_bias, dn_norm_g, hg_lb_logits, hg_norm_g, w_out, norm2_g, router_g_w, router_g_b, router_e_w, router_e_b, exp_w1, exp_w3, exp_w2, final_norm_g):
    nbp, seq, d = x_prompt.shape
    nbs = x_sample.shape[0]
    depth = w_in.shape[0]
    assert x_sample.shape[1] == 1 and seq % CHUNK == 0
    tp = nbp * seq
    xp = x_prompt.reshape(tp, d)
    xs = x_sample.reshape(nbs, d)

    sm = jax.nn.softmax(hg_lb_logits.astype(F32), axis=0)
    lb_all = jnp.maximum(jnp.cumsum(sm, axis=0) - sm[0:1], 0.0)

    tm_p = min(512, tp)
    tm_moe = min(MOE_TILE, tp)
    expand = _gate_expand(exp_w1.shape[3])
    row2 = lambda v: v.reshape(1, -1)
    gla_p, dn_p, conv_p, hg_p, gla_s, dn_s, conv_s, hg_s = ([] for _ in range(8))
    for li in range(depth):
        w_in_b = _perm_w_in(w_in[li])
        w_out_b = w_out[li].astype(BF16)
        rw = jnp.pad(jnp.concatenate([router_e_w[li], router_g_w[li]], axis=1),
                     ((0, 0), (0, LANE - N_EXPERTS - N_GROUPS))).astype(BF16)
        rb = jnp.pad(jnp.concatenate([router_e_b[li], router_g_b[li]]), (0, LANE - N_EXPERTS - N_GROUPS)).reshape(1, LANE)
        w1b, w3b, w2b = exp_w1[li].astype(BF16), exp_w3[li].astype(BF16), exp_w2[li].astype(BF16)
        final = li == depth - 1
        g1, g2, fg = row2(norm1_g[li]), row2(norm2_g[li]), row2(final_norm_g)

        pa, pb, pc = _inproj(xp, g1, w_in_b, tm_p)
        oa, sa = _gla_chunk("gla", pa, nbp, seq, gla_wa2[li], row2(gla_ba[li]), row2(gla_norm_g[li]))
        ob, sb = _dn_chunk(pb, nbp, seq, dn_conv_w[li], dn_a_log[li], dn_dt_bias[li], row2(dn_norm_g[li]))
        oc, sc = _gla_chunk("hgrn2", pc, nbp, seq, row2(lb_all[li]), row2(lb_all[li]), row2(hg_norm_g[li]))
        gla_p.append(sa)
        dn_p.append(sb)
        conv_p.append(pb.reshape(nbp, seq, DN_W)[:, seq - (CONV_W - 1):, 0:CONV_DIM])
        hg_p.append(sc)
        x1, xa, slot, cnt = _outproj_router(xp, oa, ob, oc, w_out_b, g2, rw, rb, tm_moe)
        xp = _moe(x1, xa, slot, cnt, w1b, w3b, w2b, expand, fg, final, tm_moe)

        qa, qb, qc = _inproj(xs, g1, w_in_b, nbs)
        o_s, sa, sb, sc = _decode(qa, qb, qc, state_gla[li], state_dn[li], state_conv[li], state_hgrn[li],
                                  gla_wa2[li], gla_ba[li], dn_conv_w[li], dn_a_log[li], dn_dt_bias[li],
                                  lb_all[li], gla_norm_g[li], dn_norm_g[li], hg_norm_g[li])
        gla_s.append(sa)
        dn_s.append(sb)
        conv_s.append(jnp.concatenate([state_conv[li][:, 1:], qb[:, None, 0:CONV_DIM]], axis=1))
        hg_s.append(sc)
        na, nb_ = H_GLA * DV_GLA, H_DN * DV_DN
        x1, xa, slot, cnt = _outproj_router(xs, o_s[:, 0:na], o_s[:, na:na + nb_], o_s[:, na + nb_:], w_out_b, g2, rw, rb, nbs)
        xs = _moe(x1, xa, slot, cnt, w1b, w3b, w2b, expand, fg, final, nbs)

    st = lambda xs_, ref: jnp.stack(xs_).astype(ref.dtype)
    return (xp.reshape(nbp, seq, d), xs.reshape(nbs, 1, d),
            st(gla_p, state_gla), st(dn_p, state_dn), st(conv_p, state_conv), st(hg_p, state_hgrn),
            st(gla_s, state_gla), st(dn_s, state_dn), st(conv_s, state_conv), st(hg_s, state_hgrn))
```

```python
import functools
import math

import numpy as np
import jax
import jax.numpy as jnp
from jax import lax
from jax.experimental import pallas as pl
from jax.experimental.pallas import tpu as pltpu

F32 = jnp.float32
BF16 = jnp.bfloat16
EPS = 1e-6

H_GLA, DK_GLA, DV_GLA, GLA_RANK, GLA_TAU = 6, 32, 64, 16, 16.0
H_DN, DK_DN, DV_DN, CONV_W = 6, 64, 64, 4
CONV_DIM = H_DN * (2 * DK_DN + DV_DN)
H_HG, DK_HG, DV_HG = 4, 64, 64
N_GROUPS, EXP_PER_GROUP, TOP_K = 4, 8, 2
N_EXPERTS = N_GROUPS * EXP_PER_GROUP

LANE = 128
CHUNK = 64
GLA_CHUNKS_PER_STEP = 8
DN_CHUNKS_PER_STEP = 4
MOE_TILE = 1024
MOE_ROWS_FIRST = 288
MOE_ROWS_EXTRA = 128
VMEM_LIMIT = 56 * 1024 * 1024

GLA_W = 1280
DN_W = 1664
HG_W = 1024
DN_AB = 1536


def _cparams(sem):
    return pltpu.CompilerParams(dimension_semantics=sem, vmem_limit_bytes=VMEM_LIMIT)


def _dot(a, b):
    return jnp.dot(a, b, preferred_element_type=F32)


def _dot_nt(a, b):
    return lax.dot_general(a, b, (((1,), (1,)), ((), ())), preferred_element_type=F32)


def _dot_tn(a, b):
    return lax.dot_general(a, b, (((0,), (0,)), ((), ())), preferred_element_type=F32)


def _split3(x):
    hi = x.astype(BF16)
    r = x - hi.astype(F32)
    mid = r.astype(BF16)
    lo = (r - mid.astype(F32)).astype(BF16)
    return hi, mid, lo


def _split2(x):
    hi = x.astype(BF16)
    lo = (x - hi.astype(F32)).astype(BF16)
    return hi, lo


def _dot_sel_l(m, x):
    hi, mid, lo = _split3(x)
    return _dot(m, hi) + _dot(m, mid) + _dot(m, lo)


def _dot_sel_r(x, m):
    hi, mid, lo = _split3(x)
    return _dot(hi, m) + _dot(mid, m) + _dot(lo, m)


def _dot_hp(a, b, fn=_dot):
    ah, al = _split2(a)
    bh, bl = _split2(b)
    return fn(ah, bh) + fn(ah, bl) + fn(al, bh)


def _rms(x, g):
    return x * lax.rsqrt(jnp.mean(x * x, axis=-1, keepdims=True) + EPS) * g


def _sigmoid(x):
    return 1.0 / (1.0 + jnp.exp(-x))


def _silu(x):
    return x * _sigmoid(x)


def _log_sigmoid(x):
    return jnp.minimum(x, 0.0) - jnp.log1p(jnp.exp(-jnp.abs(x)))


def _softplus(x):
    return jnp.maximum(x, 0.0) + jnp.log1p(jnp.exp(-jnp.abs(x)))


def _inproj_kernel(x_ref, g_ref, w_ref, oa_ref, ob_ref, oc_ref):
    h = _rms(x_ref[...], g_ref[...]).astype(BF16)
    oa_ref[...] = _dot(h, w_ref[0, :, 0:GLA_W])
    ob_ref[...] = _dot(h, w_ref[0, :, GLA_W:GLA_W + DN_W])
    oc_ref[...] = _dot(h, w_ref[0, :, GLA_W + DN_W:GLA_W + DN_W + HG_W])


def _inproj(x, g, w, li, tm):
    t, d = x.shape
    n = w.shape[2]
    row = lambda i: (i, 0)
    fix = lambda i: (0, 0)
    return pl.pallas_call(
        _inproj_kernel,
        grid=(t // tm,),
        in_specs=[pl.BlockSpec((tm, d), row), pl.BlockSpec((1, d), fix), pl.BlockSpec((1, d, n), lambda i: (li, 0, 0))],
        out_specs=[pl.BlockSpec((tm, GLA_W), row), pl.BlockSpec((tm, DN_W), row), pl.BlockSpec((tm, HG_W), row)],
        out_shape=[jax.ShapeDtypeStruct((t, GLA_W), F32), jax.ShapeDtypeStruct((t, DN_W), F32),
                   jax.ShapeDtypeStruct((t, HG_W), F32)],
        compiler_params=_cparams(("parallel",)),
        name="inproj",
    )(x, g, w)


def _w_in_segments():
    sizes = (H_GLA * DK_GLA, H_GLA * DK_GLA, H_GLA * DV_GLA, GLA_RANK, H_GLA * DV_GLA,
             CONV_DIM, H_DN, H_DN, H_DN * DV_DN,
             H_HG * DK_HG, H_HG * DK_HG, H_HG * DV_HG, H_HG * DV_HG)
    offs = np.concatenate([[0], np.cumsum(sizes)]).tolist()
    g_q, g_k, g_v, g_a, g_g, d_qkv, d_a, d_b, d_g, h_q, h_f, h_i, h_g = [(offs[i], sizes[i]) for i in range(len(sizes))]
    return [g_q, g_k, g_v, g_g, g_a, (None, GLA_W - 1168),
            d_qkv, d_g, d_a, d_b, (None, DN_W - 1548),
            h_q, h_f, h_i, h_g]


def _perm_w_kernel(w_ref, o_ref):
    w = w_ref[0]
    cols = [jnp.zeros((w.shape[0], n), F32) if src is None else w[:, src:src + n] for src, n in _w_in_segments()]
    o_ref[0] = jnp.concatenate(cols, axis=1).astype(BF16)


def _perm_w_in(w):
    depth, d, n = w.shape
    n_out = GLA_W + DN_W + HG_W
    tr = min(256, d)
    return pl.pallas_call(
        _perm_w_kernel,
        grid=(depth, d // tr),
        in_specs=[pl.BlockSpec((1, tr, n), lambda l, r: (l, r, 0))],
        out_specs=pl.BlockSpec((1, tr, n_out), lambda l, r: (l, r, 0)),
        out_shape=jax.ShapeDtypeStruct((depth, d, n_out), BF16),
        compiler_params=_cparams(("parallel", "parallel")),
        name="perm_w_in",
    )(w)


def _decay_consts(c):
    n = int(math.log2(c))
    idx = np.arange(c)
    lm = (idx[None, :] <= idx[:, None]).astype(np.float32)
    mats, masks = [], []
    for l in range(1, n + 1):
        hs = c >> l
        bs = 2 * hs
        blk = idx // bs
        ref = blk * bs + hs - 1
        mats.append(lm - lm[ref])
        lower = (idx % bs) >= hs
        same = blk[:, None] == blk[None, :]
        masks.append((same & lower[:, None] & (~lower)[None, :]).astype(np.float32))
    mats.append(lm)
    mats.append(1.0 - lm)
    masks.append(np.eye(c, dtype=np.float32))
    masks = np.stack(masks)
    return (jnp.asarray(np.concatenate(mats, 0), BF16), jnp.asarray(np.concatenate([masks, masks], axis=1), F32), n)


def _seg_ind(nh, dv):
    h = np.arange(nh * dv) // dv
    return jnp.asarray(h[:, None] == h[None, :], BF16)


def _seg_sum(x, ind):
    hi, lo = _split2(x)
    return _dot(hi, ind) + _dot(lo, ind)


def _pair_geometry(nh, dk):
    geo = []
    for p in range(nh // 2):
        start = (2 * p * dk) // LANE * LANE
        width = min(LANE, nh * dk - start)
        geo.append((start, width, (2 * p * dk - start, (2 * p + 1) * dk - start)))
    return geo


def _gla_chunk_kernel(mode, nh, dk, dv, nlev, nch, slab_ref, mats_ref, masks_ref, ind_ref, p1_ref, p2_ref, ng_ref,
                      o_ref, st_ref, s_scr):
    c = CHUNK
    ci = pl.program_id(1)

    @pl.when(ci == 0)
    def _():
        s_scr[...] = jnp.zeros_like(s_scr)

    hk, hv = nh * dk, nh * dv
    if mode == "gla":
        q = slab_ref[:, 0:hk] * (dk ** -0.5)
        k = slab_ref[:, hk:2 * hk]
        v = slab_ref[:, 2 * hk:2 * hk + hv]
        gate = slab_ref[:, 2 * hk + hv:2 * hk + 2 * hv]
        a_lr = slab_ref[:, 2 * hk + 2 * hv:2 * hk + 2 * hv + GLA_RANK]
        z = _dot_hp(a_lr, p1_ref[...]) + p2_ref[...]
        la = _log_sigmoid(z) * (1.0 / GLA_TAU)
    else:
        q = _silu(slab_ref[:, 0:hk])
        x = slab_ref[:, hk:2 * hk]
        v = slab_ref[:, 2 * hk:2 * hk + hv]
        gate = slab_ref[:, 2 * hk + hv:2 * hk + 2 * hv]
        lb = p1_ref[...]
        la = _log_sigmoid(x) + jnp.log1p(lb * jnp.exp(-x))
        k = (1.0 - lb) * _sigmoid(-x)

    bf = lambda t: t.astype(BF16)
    geo = _pair_geometry(nh, dk)
    npair = len(geo)
    chunks = range(nch)
    rows = [slice(i * c, (i + 1) * c) for i in chunks]
    vb = bf(v)

    qs, ks, q_in, k_rem, e_last = [], [], [], [], []
    for i in chunks:
        dall = _dot_sel_l(mats_ref[...], la[rows[i]])
        bcum = dall[nlev * c:(nlev + 1) * c]
        wl = [jnp.exp(-jnp.abs(dall[l * c:(l + 1) * c])) for l in range(nlev)]
        qc, kc = q[rows[i]], k[rows[i]]
        qs.append([bf(qc * w) for w in wl] + [bf(qc)])
        ks.append([bf(kc * w) for w in wl] + [bf(kc)])
        q_in.append(bf(qc * jnp.exp(bcum)))
        k_rem.append(bf(kc * jnp.exp(dall[(nlev + 1) * c:(nlev + 2) * c])))
        e_last.append(jnp.exp(bcum[c - 1:c]))

    out_cols = []
    for p in range(npair):
        start, width, offs = geo[p]
        blk = slice(start, start + width)
        vcol = slice(p * LANE, (p + 1) * LANE)
        lane = lax.broadcasted_iota(jnp.int32, (c, width), 1)
        hmask = [(lane >= o) & (lane < o + dk) for o in offs]
        zero = jnp.zeros((c, width), BF16)
        atts = []
        for i in chunks:
            att = None
            for l in range(nlev + 1):
                qb = qs[i][l][:, blk]
                lhs = jnp.concatenate([jnp.where(hmask[0], qb, zero), jnp.where(hmask[1], qb, zero)], axis=0)
                term = _dot_nt(lhs, ks[i][l][:, blk]) * masks_ref[l]
                att = term if att is None else att + term
            atts.append(bf(att))
        lane_v = lax.broadcasted_iota(jnp.int32, (c, LANE), 1)
        first = lane_v < dv
        intra = []
        for i in chunks:
            r2 = _dot(atts[i], vb[rows[i], vcol])
            intra.append(jnp.where(first, r2[0:c], r2[c:2 * c]))
        ri = lax.broadcasted_iota(jnp.int32, (2 * dv, width), 0)
        li = lax.broadcasted_iota(jnp.int32, (2 * dv, width), 1)
        smask = ((ri < dv) & (li >= offs[0]) & (li < offs[0] + dk)) | ((ri >= dv) & (li >= offs[1]) & (li < offs[1] + dk))
        upd = [jnp.where(smask, _dot_tn(vb[rows[i], vcol], k_rem[i][:, blk]), 0.0) for i in chunks]
        s = s_scr[p, :, 0:width]
        states = []
        for i in chunks:
            states.append(bf(s))
            s = s * e_last[i][:, blk] + upd[i]
        s_scr[p, :, 0:width] = s
        out_cols.append(jnp.concatenate(
            [intra[i] + _dot_nt(q_in[i][:, blk], states[i]) for i in chunks], axis=0))
    o = jnp.concatenate(out_cols, axis=-1)

    if mode == "gla":
        ms = _seg_sum(o * o, ind_ref[...]) * (1.0 / dv)
        o = o * lax.rsqrt(ms + EPS) * ng_ref[...] * _silu(gate)
    else:
        o = o * _sigmoid(gate)
        ms = _seg_sum(o * o, ind_ref[...]) * (1.0 / dv)
        o = o * lax.rsqrt(ms + EPS) * ng_ref[...]
    o_ref[...] = o.astype(o_ref.dtype)

    @pl.when(ci == pl.num_programs(1) - 1)
    def _():
        st_ref[0] = s_scr[...]


def _gla_chunk(mode, slab, nb, seq, p1, p2, ng):
    nh, dk, dv = (H_GLA, DK_GLA, DV_GLA) if mode == "gla" else (H_HG, DK_HG, DV_HG)
    w = slab.shape[1]
    nch = math.gcd(seq // CHUNK, GLA_CHUNKS_PER_STEP)
    sc = nch * CHUNK
    ns = seq // sc
    mats, masks, nlev = _decay_consts(CHUNK)
    ind = _seg_ind(nh, dv)
    ngt = jnp.tile(ng, (1, nh))
    geo = _pair_geometry(nh, dk)
    fix2 = lambda b, c: (0, 0)
    kern = functools.partial(_gla_chunk_kernel, mode, nh, dk, dv, nlev, nch)
    o, st = pl.pallas_call(
        kern,
        grid=(nb, ns),
        in_specs=[pl.BlockSpec((sc, w), lambda b, c: (b * ns + c, 0)),
                  pl.BlockSpec(mats.shape, fix2),
                  pl.BlockSpec(masks.shape, lambda b, c: (0, 0, 0)),
                  pl.BlockSpec(ind.shape, fix2),
                  pl.BlockSpec(p1.shape, fix2), pl.BlockSpec(p2.shape, fix2), pl.BlockSpec(ngt.shape, fix2)],
        out_specs=[pl.BlockSpec((sc, nh * dv), lambda b, c: (b * ns + c, 0)),
                   pl.BlockSpec((1, len(geo), 2 * dv, LANE), lambda b, c: (b, 0, 0, 0))],
        out_shape=[jax.ShapeDtypeStruct((nb * seq, nh * dv), BF16),
                   jax.ShapeDtypeStruct((nb, len(geo), 2 * dv, LANE), F32)],
        scratch_shapes=[pltpu.VMEM((len(geo), 2 * dv, LANE), F32)],
        compiler_params=_cparams(("parallel", "arbitrary")),
        name="chunk_" + mode,
    )(slab, mats, masks, ind, p1, p2, ngt)
    heads = []
    for p, (_, _, offs) in enumerate(geo):
        for j in range(2):
            heads.append(jnp.swapaxes(st[:, p, j * dv:(j + 1) * dv, offs[j]:offs[j] + dk], -1, -2))
    return o, jnp.stack(heads, axis=1)


def _dn_chunk_kernel(nch, slab_ref, abr_ref, lm_ref, umb_ref, ind_ref, cw_ref, pcol_ref, prow_ref, ng_ref,
                     o_ref, st_ref, s_scr, ext_scr):
    c = CHUNK
    sc = nch * c
    nh, dk, dv = H_DN, DK_DN, DV_DN
    npair = nh // 2
    ci = pl.program_id(1)

    @pl.when(ci == 0)
    def _():
        s_scr[...] = jnp.zeros_like(s_scr)
        ext_scr[0:8, :] = jnp.zeros((8, CONV_DIM), F32)

    ext_scr[8:8 + sc, :] = slab_ref[:, 0:CONV_DIM]
    conv = ext_scr[5:5 + sc, :] * cw_ref[0:1, :]
    for i in range(1, CONV_W):
        conv = conv + ext_scr[5 + i:5 + i + sc, :] * cw_ref[i:i + 1, :]
    ext_scr[0:8, :] = ext_scr[sc:sc + 8, :]
    qkv = _silu(conv)
    gate = slab_ref[:, CONV_DIM:DN_AB]
    hk = nh * dk
    q = qkv[:, 0:hk]
    k = qkv[:, hk:2 * hk]
    v = qkv[:, 2 * hk:2 * hk + nh * dv]
    q = q * lax.rsqrt(_seg_sum(q * q, ind_ref[...]) + EPS) * (dk ** -0.5)
    k = k * lax.rsqrt(_seg_sum(k * k, ind_ref[...]) + EPS)

    ab_c = slab_ref[:, DN_AB:DN_AB + LANE]
    g_c = -jnp.exp(pcol_ref[0:1, :]) * _softplus(ab_c + pcol_ref[1:2, :])
    beta_c = _sigmoid(ab_c)
    g_r = -jnp.exp(prow_ref[0]) * _softplus(abr_ref[0, 0] + prow_ref[1])
    gcum_r = _dot_sel_r(g_r, umb_ref[...])

    bf = lambda t: t.astype(BF16)
    rows = [slice(i * c, (i + 1) * c) for i in range(nch)]
    ri = lax.broadcasted_iota(jnp.int32, (2 * c, 2 * c), 0)
    cj = lax.broadcasted_iota(jnp.int32, (2 * c, 2 * c), 1)
    same = (ri < c) == (cj < c)
    tri = same & (ri >= cj)
    strict = same & (ri > cj)
    eye = (ri == cj).astype(F32)
    lane = lax.broadcasted_iota(jnp.int32, (c, LANE), 1)
    first = lane < dk
    zero = jnp.zeros((c, LANE), BF16)
    gcum_cs = [_dot_sel_l(lm_ref[...], g_c[rows[i]]) for i in range(nch)]

    def lanes2(col0, col1):
        return jnp.where(first, jnp.broadcast_to(col0, (c, LANE)), jnp.broadcast_to(col1, (c, LANE)))

    def stack2(col0, col1):
        return jnp.concatenate([jnp.broadcast_to(col0, (c, LANE)), jnp.broadcast_to(col1, (c, LANE))], axis=0)

    def rows2(x):
        return jnp.concatenate([jnp.where(first, x, zero), jnp.where(first, zero, x)], axis=0)

    pairs = range(npair)

    def prepare(i, p):
        h0, h1 = 2 * p, 2 * p + 1
        blk = slice(p * LANE, (p + 1) * LANE)
        gcc = gcum_cs[i]
        gc0, gc1 = gcc[:, h0:h0 + 1], gcc[:, h1:h1 + 1]
        gl0, gl1 = gcc[c - 1:c, h0:h0 + 1], gcc[c - 1:c, h1:h1 + 1]
        b0, b1 = beta_c[rows[i], nh + h0:nh + h0 + 1], beta_c[rows[i], nh + h1:nh + h1 + 1]
        gcr = gcum_r[p:p + 1, i * LANE:(i + 1) * LANE]
        gam = jnp.where(tri, jnp.exp(jnp.where(tri, stack2(gc0, gc1) - gcr, 0.0)), 0.0)
        kc, qc, vc = k[rows[i], blk], q[rows[i], blk], v[rows[i], blk]
        beta_l = lanes2(b0, b1)
        egc_l = jnp.exp(lanes2(gc0, gc1))
        kb = kc * beta_l
        return dict(
            gam=gam, k2=rows2(bf(kc)), kb2=rows2(bf(kb)), q2=rows2(bf(qc)), vb2=rows2(bf(vc * beta_l)),
            ke2=rows2(bf(kb * egc_l)), qe=bf(qc * egc_l),
            kd2=rows2(bf(kc * jnp.exp(lanes2(gl0, gl1) - lanes2(gc0, gc1)))),
            eglast=jnp.exp(stack2(gl0, gl1)))

    units = [(i, p) for i in range(nch) for p in pairs]
    pre = {u: prepare(*u) for u in units}
    ms = {u: jnp.where(strict, _dot_nt(pre[u]["kb2"], pre[u]["k2"]) * pre[u]["gam"], 0.0) for u in units}
    tinv = {u: eye - ms[u] for u in units}
    pw = {u: bf(ms[u]) for u in units}
    for _ in range(int(math.log2(c)) - 1):
        pw = {u: bf(_dot(pw[u], pw[u])) for u in units}
        tinv = {u: tinv[u] + _dot(bf(tinv[u]), pw[u]) for u in units}
    tb = {u: bf(tinv[u]) for u in units}
    uu = {u: _dot(tb[u], pre[u]["vb2"]) for u in units}
    ww = {u: bf(_dot(tb[u], pre[u]["ke2"])) for u in units}
    att = {u: bf(_dot_nt(pre[u]["q2"], pre[u]["k2"]) * pre[u]["gam"]) for u in units}

    states = [s_scr[p] for p in pairs]
    pieces = [[] for _ in pairs]
    for i in range(nch):
        for p in pairs:
            u = (i, p)
            sb = bf(states[p])
            vn = bf(uu[u] - _dot(ww[u], sb))
            o2 = _dot(att[u], vn)
            pieces[p].append(_dot(pre[u]["qe"], sb) + o2[0:c] + o2[c:2 * c])
            states[p] = states[p] * pre[u]["eglast"] + _dot_tn(pre[u]["kd2"], vn)
    for p in pairs:
        s_scr[p] = states[p]
    o = jnp.concatenate([jnp.concatenate(pc, axis=0) for pc in pieces], axis=-1)
    ms_o = _seg_sum(o * o, ind_ref[...]) * (1.0 / dv)
    o_ref[...] = (o * lax.rsqrt(ms_o + EPS) * ng_ref[...] * _silu(gate)).astype(o_ref.dtype)

    @pl.when(ci == pl.num_programs(1) - 1)
    def _():
        st_ref[0] = s_scr[...]


def _dn_chunk(slab, nb, seq, conv_w, a_log, dt_bias, ng):
    c = CHUNK
    nch = math.gcd(seq // c, DN_CHUNKS_PER_STEP)
    sc = nch * c
    ns = seq // sc
    npair = H_DN // 2
    nrow = 16
    idx = np.arange(c)
    lm = jnp.asarray((idx[None, :] <= idx[:, None]), BF16)
    um = (idx[:, None] <= idx[None, :]).astype(np.float32)
    umb = jnp.asarray(np.kron(np.eye(2 * nch, dtype=np.float32), um), BF16)
    ind = _seg_ind(H_DN, DK_DN)
    cw = jnp.pad(conv_w, ((0, 8 - CONV_W), (0, 0)))
    pad = lambda p: jnp.pad(p, (0, LANE - H_DN))
    pcol = jnp.zeros((8, LANE), F32).at[0].set(pad(a_log)).at[1].set(pad(dt_bias))
    a_cols = slab[:, DN_AB:DN_AB + H_DN].reshape(nb, ns, nch, c, npair, 2)
    a_rows = jnp.transpose(a_cols, (0, 1, 4, 2, 5, 3)).reshape(nb, ns, npair, nch * 2 * c)
    a_rows = jnp.pad(a_rows, ((0, 0), (0, 0), (0, nrow - npair), (0, 0)))
    rowp = lambda p: jnp.pad(jnp.broadcast_to(p.reshape(npair, 1, 2, 1), (npair, nch, 2, c)).reshape(npair, nch * 2 * c),
                             ((0, nrow - npair), (0, 0)))
    prow = jnp.stack([rowp(a_log), rowp(dt_bias)])
    ngt = jnp.tile(ng, (1, H_DN))
    fix2 = lambda b, c_: (0, 0)
    o, st = pl.pallas_call(
        functools.partial(_dn_chunk_kernel, nch),
        grid=(nb, ns),
        in_specs=[pl.BlockSpec((sc, DN_W), lambda b, c_: (b * ns + c_, 0)),
                  pl.BlockSpec((1, 1, nrow, nch * 2 * c), lambda b, c_: (b, c_, 0, 0)),
                  pl.BlockSpec((c, c), fix2), pl.BlockSpec(umb.shape, fix2), pl.BlockSpec(ind.shape, fix2),
                  pl.BlockSpec((8, CONV_DIM), fix2), pl.BlockSpec((8, LANE), fix2),
                  pl.BlockSpec(prow.shape, lambda b, c_: (0, 0, 0)), pl.BlockSpec(ngt.shape, fix2)],
        out_specs=[pl.BlockSpec((sc, H_DN * DV_DN), lambda b, c_: (b * ns + c_, 0)),
                   pl.BlockSpec((1, npair, 2 * DK_DN, 2 * DV_DN), lambda b, c_: (b, 0, 0, 0))],
        out_shape=[jax.ShapeDtypeStruct((nb * seq, H_DN * DV_DN), BF16),
                   jax.ShapeDtypeStruct((nb, npair, 2 * DK_DN, 2 * DV_DN), F32)],
        scratch_shapes=[pltpu.VMEM((npair, 2 * DK_DN, 2 * DV_DN), F32), pltpu.VMEM((sc + 8, CONV_DIM), F32)],
        compiler_params=_cparams(("parallel", "arbitrary")),
        name="chunk_dn",
    )(slab, a_rows, lm, umb, ind, cw, pcol, prow, ngt)
    heads = [st[:, p, j * DK_DN:(j + 1) * DK_DN, j * DV_DN:(j + 1) * DV_DN] for p in range(npair) for j in range(2)]
    return o, jnp.stack(heads, axis=1)


def _state_in(s_ref, dk):
    st = s_ref[...].T
    return st.reshape(dk, st.shape[0] // dk, st.shape[1])


def _state_out(so_ref, new):
    so_ref[...] = jnp.concatenate(new, axis=0).T


def _decode_gla_kernel(q_ref, k_ref, v_ref, g_ref, alr_ref, wa2t_ref, ba_ref, ng_ref, s_ref, o_ref, so_ref):
    dk = q_ref.shape[1]
    z = _dot_hp(wa2t_ref[0], alr_ref[...]) + ba_ref[0]
    dec = jnp.exp(_log_sigmoid(z) * (1.0 / GLA_TAU))
    q = q_ref[0] * (dk ** -0.5)
    k = k_ref[0]
    v = v_ref[0]
    st = _state_in(s_ref, dk)
    acc = jnp.zeros_like(v)
    new = []
    for d in range(dk):
        s_new = st[d] * dec[d:d + 1, :] + k[d:d + 1, :] * v
        new.append(s_new)
        acc = acc + q[d:d + 1, :] * s_new
    _state_out(so_ref, new)
    ms = jnp.mean(acc * acc, axis=0, keepdims=True)
    o_ref[0] = acc * lax.rsqrt(ms + EPS) * ng_ref[...] * _silu(g_ref[0])


def _decode_dn_kernel(x_ref, cb_ref, cw_ref, a_ref, b_ref, p_ref, g_ref, ng_ref, s_ref, o_ref, so_ref):
    dk = x_ref.shape[2]
    conv = x_ref[:, 0] * cw_ref[CONV_W - 1, :, 0]
    for i in range(CONV_W - 1):
        conv = conv + cb_ref[i, :, 0] * cw_ref[i, :, 0]
    qkv = _silu(conv)
    q, k, v = qkv[0], qkv[1], qkv[2]
    q = q * lax.rsqrt(jnp.sum(q * q, axis=0, keepdims=True) + EPS) * (dk ** -0.5)
    k = k * lax.rsqrt(jnp.sum(k * k, axis=0, keepdims=True) + EPS)
    eg = jnp.exp(-jnp.exp(p_ref[0, 0:1, :]) * _softplus(a_ref[0] + p_ref[0, 1:2, :]))
    beta = _sigmoid(b_ref[0])
    st = _state_in(s_ref, dk)
    ks = jnp.zeros_like(v)
    for d in range(dk):
        ks = ks + k[d:d + 1, :] * st[d]
    v_new = beta * (v - eg * ks)
    acc = jnp.zeros_like(v)
    new = []
    for d in range(dk):
        s_new = st[d] * eg + k[d:d + 1, :] * v_new
        new.append(s_new)
        acc = acc + q[d:d + 1, :] * s_new
    _state_out(so_ref, new)
    ms = jnp.mean(acc * acc, axis=0, keepdims=True)
    o_ref[0] = acc * lax.rsqrt(ms + EPS) * ng_ref[...] * _silu(g_ref[0])


def _decode_hg_kernel(q_ref, f_ref, v_ref, g_ref, lb_ref, ng_ref, s_ref, o_ref, so_ref):
    dk = q_ref.shape[1]
    x = f_ref[0]
    lb = lb_ref[0]
    f = jnp.exp(_log_sigmoid(x) + jnp.log1p(lb * jnp.exp(-x)))
    k = (1.0 - lb) * _sigmoid(-x)
    q = _silu(q_ref[0])
    v = v_ref[0]
    st = _state_in(s_ref, dk)
    acc = jnp.zeros_like(v)
    new = []
    for d in range(dk):
        s_new = st[d] * f[d:d + 1, :] + k[d:d + 1, :] * v
        new.append(s_new)
        acc = acc + q[d:d + 1, :] * s_new
    _state_out(so_ref, new)
    acc = acc * _sigmoid(g_ref[0])
    ms = jnp.mean(acc * acc, axis=0, keepdims=True)
    o_ref[0] = acc * lax.rsqrt(ms + EPS) * ng_ref[...]


def _head_call(kern, name, nh, dk, dv, nb, li, args, specs):
    o, s = pl.pallas_call(
        kern,
        grid=(nh,),
        in_specs=specs + [pl.BlockSpec((nb, dk * dv), lambda h: (li, h))],
        out_specs=[pl.BlockSpec((1, dv, nb), lambda h: (h, 0, 0)),
                   pl.BlockSpec((nb, dk * dv), lambda h: (0, h))],
        out_shape=[jax.ShapeDtypeStruct((nh, dv, nb), F32), jax.ShapeDtypeStruct((nb, nh * dk * dv), F32)],
        compiler_params=_cparams(("parallel",)),
        name=name,
    )(*args)
    return o.reshape(nh * dv, nb), s.reshape(nb, nh, dk, dv)


def _decode(pa, pb, pc, li, s_gla, s_dn, s_conv, s_hg, wa2, ba, conv_w, a_log, dt_bias, lb, nga, ngb, ngc):
    nb = pa.shape[0]
    bl = lambda p, *shape: jnp.broadcast_to(p.reshape(shape + (1,)), shape + (nb,))
    tr = lambda s: s.reshape(s.shape[0] * nb, -1)
    byh = lambda n: pl.BlockSpec((1, n, nb), lambda h: (h, 0, 0))
    fixed = lambda shape: pl.BlockSpec(shape, lambda h: (0,) * len(shape))

    nh, dk, dv = H_GLA, DK_GLA, DV_GLA
    hk, hv = nh * dk, nh * dv
    pt = pa.T
    args = (pt[0:hk].reshape(nh, dk, nb), pt[hk:2 * hk].reshape(nh, dk, nb),
            pt[2 * hk:2 * hk + hv].reshape(nh, dv, nb), pt[2 * hk + hv:2 * hk + 2 * hv].reshape(nh, dv, nb),
            pt[2 * hk + 2 * hv:2 * hk + 2 * hv + GLA_RANK], wa2.T.reshape(nh, dk, GLA_RANK),
            bl(ba, nh, dk), bl(nga, dv), tr(s_gla))
    specs = [byh(dk), byh(dk), byh(dv), byh(dv), fixed((GLA_RANK, nb)),
             pl.BlockSpec((1, dk, GLA_RANK), lambda h: (h, 0, 0)), byh(dk), fixed((dv, nb))]
    o_a, sa = _head_call(_decode_gla_kernel, "decode_gla", nh, dk, dv, nb, li, args, specs)

    nh, dk, dv = H_DN, DK_DN, DV_DN
    pt = pb.T
    x = pt[0:CONV_DIM].reshape(3, nh, dk, nb)
    cb = jnp.transpose(s_conv, (1, 2, 0)).reshape(CONV_W - 1, 3, nh, dk, nb)
    cw = bl(conv_w, CONV_W, 3, nh, dk)
    prm = jnp.stack([bl(a_log, nh), bl(dt_bias, nh)], axis=1)
    args = (x, cb, cw, pt[DN_AB:DN_AB + nh].reshape(nh, 1, nb), pt[DN_AB + nh:DN_AB + 2 * nh].reshape(nh, 1, nb),
            prm, pt[CONV_DIM:DN_AB].reshape(nh, dv, nb), bl(ngb, dv), tr(s_dn))
    specs = [pl.BlockSpec((3, 1, dk, nb), lambda h: (0, h, 0, 0)),
             pl.BlockSpec((CONV_W - 1, 3, 1, dk, nb), lambda h: (0, 0, h, 0, 0)),
             pl.BlockSpec((CONV_W, 3, 1, dk, nb), lambda h: (0, 0, h, 0, 0)),
             byh(1), byh(1), byh(2), byh(dv), fixed((dv, nb))]
    o_b, sb = _head_call(_decode_dn_kernel, "decode_dn", nh, dk, dv, nb, li, args, specs)

    nh, dk, dv = H_HG, DK_HG, DV_HG
    hk = nh * dk
    pt = pc.T
    args = (pt[0:hk].reshape(nh, dk, nb), pt[hk:2 * hk].reshape(nh, dk, nb),
            pt[2 * hk:3 * hk].reshape(nh, dv, nb), pt[3 * hk:4 * hk].reshape(nh, dv, nb),
            bl(lb, nh, dk), bl(ngc, dv), tr(s_hg))
    specs = [byh(dk), byh(dk), byh(dv), byh(dv), byh(dk), fixed((dv, nb))]
    o_c, sc = _head_call(_decode_hg_kernel, "decode_hgrn2", nh, dk, dv, nb, li, args, specs)

    return jnp.concatenate([o_a, o_b, o_c], axis=0).T.astype(BF16), sa, sb, sc


def _outproj_router_kernel(x_ref, oa_ref, ob_ref, oc_ref, w_ref, g_ref, rw_ref, rb_ref, ui_ref,
                           x1_ref, xa_ref, slot_ref, cnt_ref):
    tm, d = x_ref.shape
    na, nb_ = oa_ref.shape[1], ob_ref.shape[1]
    x1 = (x_ref[...] + _dot(oa_ref[...], w_ref[0:na, :]) + _dot(ob_ref[...], w_ref[na:na + nb_, :])
          + _dot(oc_ref[...], w_ref[na + nb_:, :]))
    x1_ref[...] = x1
    h2 = _rms(x1, g_ref[...]).astype(BF16)
    xa_ref[:, 0:d] = h2
    logits = _dot(h2, rw_ref[...]) + rb_ref[...]
    lane = lax.broadcasted_iota(jnp.int32, logits.shape, 1)
    neg = jnp.float32(-jnp.inf)
    big = jnp.int32(1 << 20)
    is_g = (lane >= N_EXPERTS) & (lane < N_EXPERTS + N_GROUPS)
    lg = jnp.where(is_g, logits, neg)
    mg = jnp.max(lg, axis=-1, keepdims=True)
    pg_top = 1.0 / jnp.sum(jnp.where(is_g, jnp.exp(lg - mg), 0.0), axis=-1, keepdims=True)
    g_idx = jnp.min(jnp.where(lg == mg, lane, big), axis=-1, keepdims=True) - N_EXPERTS
    in_grp = (lane >= g_idx * EXP_PER_GROUP) & (lane < (g_idx + 1) * EXP_PER_GROUP)
    le = jnp.where(in_grp, logits, neg)
    me = jnp.max(le, axis=-1, keepdims=True)
    ex = jnp.where(in_grp, jnp.exp(le - me), 0.0)
    pe = ex / jnp.sum(ex, axis=-1, keepdims=True)
    pe = jnp.where(in_grp, pe, -1.0)
    v1 = jnp.max(pe, axis=-1, keepdims=True)
    i1 = jnp.min(jnp.where(pe == v1, lane, big), axis=-1, keepdims=True)
    pe2 = jnp.where(lane == i1, -1.0, pe)
    v2 = jnp.max(pe2, axis=-1, keepdims=True)
    i2 = jnp.min(jnp.where(pe2 == v2, lane, big), axis=-1, keepdims=True)
    tot = v1 + v2
    gate = pg_top * (jnp.where(lane == i1, v1 / tot, 0.0) + jnp.where(lane == i2, v2 / tot, 0.0))
    hi, mid, lo = _split3(gate)
    xa_ref[:, d:d + LANE] = (hi.astype(F32) + pltpu.roll(mid.astype(F32), N_EXPERTS, 1)
                             + pltpu.roll(lo.astype(F32), 2 * N_EXPERTS, 1)).astype(BF16)
    ind = (lane == g_idx).astype(BF16)
    both = _dot_tn(ind, ui_ref[...])
    slot_ref[0] = jnp.where(both[0:8, tm:2 * tm] > 0.5, both[0:8, 0:tm], -1.0)
    cnt_ref[0] = _dot(jnp.ones((8, tm), BF16), ind).astype(jnp.int32)


def _outproj_router(x, oa, ob, oc, w, g, rw, rb, tm):
    t, d = x.shape
    nt = t // tm
    idx = np.arange(tm)
    ui = jnp.asarray(np.concatenate([idx[:, None] < idx[None, :], np.eye(tm, dtype=bool)], axis=1), BF16)
    row = lambda i: (i, 0)
    fix = lambda i: (0, 0)
    x1, xa, slot, cnt = pl.pallas_call(
        _outproj_router_kernel,
        grid=(nt,),
        in_specs=[pl.BlockSpec((tm, d), row), pl.BlockSpec((tm, oa.shape[1]), row),
                  pl.BlockSpec((tm, ob.shape[1]), row), pl.BlockSpec((tm, oc.shape[1]), row),
                  pl.BlockSpec(w.shape, fix), pl.BlockSpec((1, d), fix),
                  pl.BlockSpec(rw.shape, fix), pl.BlockSpec((1, LANE), fix), pl.BlockSpec(ui.shape, fix)],
        out_specs=[pl.BlockSpec((tm, d), row), pl.BlockSpec((tm, d + LANE), row),
                   pl.BlockSpec((1, 8, tm), lambda i: (i, 0, 0)), pl.BlockSpec((1, 8, LANE), lambda i: (i, 0, 0))],
        out_shape=[jax.ShapeDtypeStruct((t, d), F32), jax.ShapeDtypeStruct((t, d + LANE), BF16),
                   jax.ShapeDtypeStruct((nt, 8, tm), F32), jax.ShapeDtypeStruct((nt, 8, LANE), jnp.int32)],
        compiler_params=_cparams(("parallel",)),
        name="outproj_router",
    )(x, oa, ob, oc, w, g, rw, rb, ui)
    return x1, xa, slot[:, 0:N_GROUPS, None, :], cnt[:, 0, 0:N_GROUPS]


def _moe_kernel(final, rb0, rbx, cnt_ref, x1_ref, xa_ref, slot_ref, ex_ref, w1_ref, w3_ref, w2_ref, fg_ref, y_ref):
    i, g = pl.program_id(0), pl.program_id(1)
    tm, d = x1_ref.shape
    ne, _, f = w1_ref.shape

    @pl.when(g == 0)
    def _():
        y_ref[...] = x1_ref[...]

    cnt = cnt_ref[i, g]
    slot = slot_ref[0, 0]

    def rows(row0, rb):
        rid = (lax.broadcasted_iota(jnp.int32, (rb, tm), 0) + row0).astype(F32)
        sel = (rid == slot).astype(BF16)
        xg = _dot(sel, xa_ref[...])
        xb = xg[:, 0:d].astype(BF16)
        gexp = _dot(xg[:, d:d + LANE].astype(BF16), ex_ref[0])
        hid = [(_silu(_dot(xb, w1_ref[e])) * _dot(xb, w3_ref[e]) * gexp[:, e * f:(e + 1) * f]).astype(BF16)
               for e in range(ne)]
        yg = _dot(jnp.concatenate(hid, axis=-1), w2_ref[...].reshape(ne * f, d))
        y_ref[...] += _dot_tn(sel, yg.astype(BF16))

    @pl.when(cnt > 0)
    def _():
        rows(0, rb0)

    def extra(j, carry):
        rows(rb0 + j * rbx, rbx)
        return carry

    lax.fori_loop(0, (jnp.maximum(cnt - rb0, 0) + rbx - 1) // rbx, extra, 0)

    if final:
        @pl.when(g == pl.num_programs(1) - 1)
        def _():
            y_ref[...] = _rms(y_ref[...], fg_ref[...])


def _gate_expand(f):
    r = np.arange(LANE)
    e = r % N_EXPERTS
    col_e = np.arange(EXP_PER_GROUP * f) // f
    m = [(r[:, None] < 3 * N_EXPERTS) & (e[:, None] == g * EXP_PER_GROUP + col_e[None, :]) for g in range(N_GROUPS)]
    return jnp.asarray(np.stack(m), BF16)


def _moe(x1, xa, slot, cnt, w1, w3, w2, li, expand, fg, final, tm):
    t, d = x1.shape
    f = w1.shape[2]
    gf = EXP_PER_GROUP * f
    rb0 = min(tm, MOE_ROWS_FIRST)
    rbx = min(tm, MOE_ROWS_EXTRA)
    row = lambda i, g, c: (i, 0)
    grp = lambda i, g, c: (g, 0, 0)
    wgrp = lambda i, g, c: (li * N_GROUPS + g, 0, 0)
    return pl.pallas_call(
        functools.partial(_moe_kernel, final, rb0, rbx),
        grid_spec=pltpu.PrefetchScalarGridSpec(
            num_scalar_prefetch=1,
            grid=(t // tm, N_GROUPS),
            in_specs=[pl.BlockSpec((tm, d), row), pl.BlockSpec((tm, d + LANE), row),
                      pl.BlockSpec((1, 1, 1, tm), lambda i, g, c: (i, g, 0, 0)),
                      pl.BlockSpec((1, LANE, gf), grp),
                      pl.BlockSpec((EXP_PER_GROUP, d, f), wgrp), pl.BlockSpec((EXP_PER_GROUP, d, f), wgrp),
                      pl.BlockSpec((EXP_PER_GROUP, f, d), wgrp),
                      pl.BlockSpec((1, d), lambda i, g, c: (0, 0))],
            out_specs=pl.BlockSpec((tm, d), row)),
        out_shape=jax.ShapeDtypeStruct((t, d), F32),
        compiler_params=_cparams(("parallel", "arbitrary")),
        name="moe",
    )(cnt, x1, xa, slot, expand, w1, w3, w2, fg)


def kernel(x_prompt, x_sample, state_gla, state_dn, state_conv, state_hgrn, norm1_g, w_in, gla_wa2, gla_ba, gla_norm_g, dn_conv_w, dn_a_log, dn_dt_bias, dn_norm_g, hg_lb_logits, hg_norm_g, w_out, norm2_g, router_g_w, router_g_b, router_e_w, router_e_b, exp_w1, exp_w3, exp_w2, final_norm_g):
    nbp, seq, d = x_prompt.shape
    nbs = x_sample.shape[0]
    depth = w_in.shape[0]
    assert x_sample.shape[1] == 1 and seq % CHUNK == 0
    tp = nbp * seq
    xp = x_prompt.reshape(tp, d)
    xs = x_sample.reshape(nbs, d)

    sm = jax.nn.softmax(hg_lb_logits.astype(F32), axis=0)
    lb_all = jnp.maximum(jnp.cumsum(sm, axis=0) - sm[0:1], 0.0)

    tm_p = min(512, tp)
    tm_moe = min(MOE_TILE, tp)
    expand = _gate_expand(exp_w1.shape[3])
    row2 = lambda v: v.reshape(1, -1)
    w_in_b = _perm_w_in(w_in)
    stack_e = lambda w: w.astype(BF16).reshape((w.shape[0] * w.shape[1],) + w.shape[2:])
    w1b, w3b, w2b = stack_e(exp_w1), stack_e(exp_w3), stack_e(exp_w2)
    gla_p, dn_p, conv_p, hg_p, gla_s, dn_s, conv_s, hg_s = ([] for _ in range(8))
    for li in range(depth):
        w_out_b = w_out[li].astype(BF16)
        rw = jnp.pad(jnp.concatenate([router_e_w[li], router_g_w[li]], axis=1),
                     ((0, 0), (0, LANE - N_EXPERTS - N_GROUPS))).astype(BF16)
        rb = jnp.pad(jnp.concatenate([router_e_b[li], router_g_b[li]]), (0, LANE - N_EXPERTS - N_GROUPS)).reshape(1, LANE)
        final = li == depth - 1
        g1, g2, fg = row2(norm1_g[li]), row2(norm2_g[li]), row2(final_norm_g)

        pa, pb, pc = _inproj(xp, g1, w_in_b, li, tm_p)
        oa, sa = _gla_chunk("gla", pa, nbp, seq, gla_wa2[li], row2(gla_ba[li]), row2(gla_norm_g[li]))
        ob, sb = _dn_chunk(pb, nbp, seq, dn_conv_w[li], dn_a_log[li], dn_dt_bias[li], row2(dn_norm_g[li]))
        oc, sc = _gla_chunk("hgrn2", pc, nbp, seq, row2(lb_all[li]), row2(lb_all[li]), row2(hg_norm_g[li]))
        gla_p.append(sa)
        dn_p.append(sb)
        conv_p.append(pb.reshape(nbp, seq, DN_W)[:, seq - (CONV_W - 1):, 0:CONV_DIM])
        hg_p.append(sc)
        x1, xa, slot, cnt = _outproj_router(xp, oa, ob, oc, w_out_b, g2, rw, rb, tm_moe)
        xp = _moe(x1, xa, slot, cnt, w1b, w3b, w2b, li, expand, fg, final, tm_moe)

        qa, qb, qc = _inproj(xs, g1, w_in_b, li, nbs)
        o_s, sa, sb, sc = _decode(qa, qb, qc, li, state_gla, state_dn, state_conv[li], state_hgrn,
                                  gla_wa2[li], gla_ba[li], dn_conv_w[li], dn_a_log[li], dn_dt_bias[li],
                                  lb_all[li], gla_norm_g[li], dn_norm_g[li], hg_norm_g[li])
        gla_s.append(sa)
        dn_s.append(sb)
        conv_s.append(jnp.concatenate([state_conv[li][:, 1:], qb[:, None, 0:CONV_DIM]], axis=1))
        hg_s.append(sc)
        na, nb_ = H_GLA * DV_GLA, H_DN * DV_DN
        x1, xa, slot, cnt = _outproj_router(xs, o_s[:, 0:na], o_s[:, na:na + nb_], o_s[:, na + nb_:], w_out_b, g2, rw, rb, nbs)
        xs = _moe(x1, xa, slot, cnt, w1b, w3b, w2b, li, expand, fg, final, nbs)

    st = lambda xs_, ref: jnp.stack(xs_).astype(ref.dtype)
    return (xp.reshape(nbp, seq, d), xs.reshape(nbs, 1, d),
            st(gla_p, state_gla), st(dn_p, state_dn), st(conv_p, state_conv), st(hg_p, state_hgrn),
            st(gla_s, state_gla), st(dn_s, state_dn), st(conv_s, state_conv), st(hg_s, state_hgrn))
```

```python
import functools
import math

import numpy as np
import jax
import jax.numpy as jnp
from jax import lax
from jax.experimental import pallas as pl
from jax.experimental.pallas import tpu as pltpu

F32 = jnp.float32
BF16 = jnp.bfloat16
EPS = 1e-6

H_GLA, DK_GLA, DV_GLA, GLA_RANK, GLA_TAU = 6, 32, 64, 16, 16.0
H_DN, DK_DN, DV_DN, CONV_W = 6, 64, 64, 4
CONV_DIM = H_DN * (2 * DK_DN + DV_DN)
H_HG, DK_HG, DV_HG = 4, 64, 64
N_GROUPS, EXP_PER_GROUP, TOP_K = 4, 8, 2
N_EXPERTS = N_GROUPS * EXP_PER_GROUP

LANE = 128
CHUNK = 64
GLA_CHUNKS_PER_STEP = 8
DN_CHUNKS_PER_STEP = 4
MOE_TILE = 1024
MOE_ROWS_FIRST = 288
MOE_ROWS_EXTRA = 128
ROUTER_ROWS = 256
VMEM_LIMIT = 56 * 1024 * 1024

GLA_W = 1280
DN_W = 1664
HG_W = 1024
DN_AB = 1536


def _cparams(sem):
    return pltpu.CompilerParams(dimension_semantics=sem, vmem_limit_bytes=VMEM_LIMIT)


def _dot(a, b):
    return jnp.dot(a, b, preferred_element_type=F32)


def _dot_nt(a, b):
    return lax.dot_general(a, b, (((1,), (1,)), ((), ())), preferred_element_type=F32)


def _dot_tn(a, b):
    return lax.dot_general(a, b, (((0,), (0,)), ((), ())), preferred_element_type=F32)


def _split3(x):
    hi = x.astype(BF16)
    r = x - hi.astype(F32)
    mid = r.astype(BF16)
    lo = (r - mid.astype(F32)).astype(BF16)
    return hi, mid, lo


def _split2(x):
    hi = x.astype(BF16)
    lo = (x - hi.astype(F32)).astype(BF16)
    return hi, lo


def _dot_sel_l(m, x):
    hi, mid, lo = _split3(x)
    return _dot(m, hi) + _dot(m, mid) + _dot(m, lo)


def _dot_sel_r(x, m):
    hi, mid, lo = _split3(x)
    return _dot(hi, m) + _dot(mid, m) + _dot(lo, m)


def _dot_hp(a, b, fn=_dot):
    ah, al = _split2(a)
    bh, bl = _split2(b)
    return fn(ah, bh) + fn(ah, bl) + fn(al, bh)


def _rms(x, g):
    return x * lax.rsqrt(jnp.mean(x * x, axis=-1, keepdims=True) + EPS) * g


def _sigmoid(x):
    return 1.0 / (1.0 + jnp.exp(-x))


def _silu(x):
    return x * _sigmoid(x)


def _log_sigmoid(x):
    return jnp.minimum(x, 0.0) - jnp.log1p(jnp.exp(-jnp.abs(x)))


def _softplus(x):
    return jnp.maximum(x, 0.0) + jnp.log1p(jnp.exp(-jnp.abs(x)))


def _inproj_kernel(x_ref, g_ref, w_ref, oa_ref, ob_ref, oc_ref):
    h = _rms(x_ref[...], g_ref[...]).astype(BF16)
    oa_ref[...] = _dot(h, w_ref[0, :, 0:GLA_W])
    ob_ref[...] = _dot(h, w_ref[0, :, GLA_W:GLA_W + DN_W])
    oc_ref[...] = _dot(h, w_ref[0, :, GLA_W + DN_W:GLA_W + DN_W + HG_W])


def _inproj(x, g, w, li, tm):
    t, d = x.shape
    n = w.shape[2]
    row = lambda i: (i, 0)
    fix = lambda i: (0, 0)
    return pl.pallas_call(
        _inproj_kernel,
        grid=(t // tm,),
        in_specs=[pl.BlockSpec((tm, d), row), pl.BlockSpec((1, d), fix),
                  pl.BlockSpec((1, d, n), lambda i: (li, 0, 0), pipeline_mode=pl.Buffered(1))],
        out_specs=[pl.BlockSpec((tm, GLA_W), row), pl.BlockSpec((tm, DN_W), row), pl.BlockSpec((tm, HG_W), row)],
        out_shape=[jax.ShapeDtypeStruct((t, GLA_W), F32), jax.ShapeDtypeStruct((t, DN_W), F32),
                   jax.ShapeDtypeStruct((t, HG_W), F32)],
        compiler_params=_cparams(("parallel",)),
        name="inproj",
    )(x, g, w)


def _w_in_segments():
    sizes = (H_GLA * DK_GLA, H_GLA * DK_GLA, H_GLA * DV_GLA, GLA_RANK, H_GLA * DV_GLA,
             CONV_DIM, H_DN, H_DN, H_DN * DV_DN,
             H_HG * DK_HG, H_HG * DK_HG, H_HG * DV_HG, H_HG * DV_HG)
    offs = np.concatenate([[0], np.cumsum(sizes)]).tolist()
    g_q, g_k, g_v, g_a, g_g, d_qkv, d_a, d_b, d_g, h_q, h_f, h_i, h_g = [(offs[i], sizes[i]) for i in range(len(sizes))]
    return [g_q, g_k, g_v, g_g, g_a, (None, GLA_W - 1168),
            d_qkv, d_g, d_a, d_b, (None, DN_W - 1548),
            h_q, h_f, h_i, h_g]


def _perm_w_kernel(w_ref, o_ref):
    w = w_ref[0]
    cols = [jnp.zeros((w.shape[0], n), F32) if src is None else w[:, src:src + n] for src, n in _w_in_segments()]
    o_ref[0] = jnp.concatenate(cols, axis=1).astype(BF16)


def _perm_w_in(w):
    depth, d, n = w.shape
    n_out = GLA_W + DN_W + HG_W
    tr = min(256, d)
    return pl.pallas_call(
        _perm_w_kernel,
        grid=(depth, d // tr),
        in_specs=[pl.BlockSpec((1, tr, n), lambda l, r: (l, r, 0))],
        out_specs=pl.BlockSpec((1, tr, n_out), lambda l, r: (l, r, 0)),
        out_shape=jax.ShapeDtypeStruct((depth, d, n_out), BF16),
        compiler_params=_cparams(("parallel", "parallel")),
        name="perm_w_in",
    )(w)


def _decay_consts(c):
    n = int(math.log2(c))
    idx = np.arange(c)
    lm = (idx[None, :] <= idx[:, None]).astype(np.float32)
    mats, masks = [], []
    for l in range(1, n + 1):
        hs = c >> l
        bs = 2 * hs
        blk = idx // bs
        ref = blk * bs + hs - 1
        mats.append(lm - lm[ref])
        lower = (idx % bs) >= hs
        same = blk[:, None] == blk[None, :]
        masks.append((same & lower[:, None] & (~lower)[None, :]).astype(np.float32))
    mats.append(lm)
    mats.append(1.0 - lm)
    masks.append(np.eye(c, dtype=np.float32))
    masks = np.stack(masks)
    return (jnp.asarray(np.concatenate(mats, 0), BF16), jnp.asarray(np.concatenate([masks, masks], axis=1), F32), n)


def _seg_ind(nh, dv):
    h = np.arange(nh * dv) // dv
    return jnp.asarray(h[:, None] == h[None, :], BF16)


def _seg_sum(x, ind):
    hi, lo = _split2(x)
    return _dot(hi, ind) + _dot(lo, ind)


def _pair_geometry(nh, dk):
    geo = []
    for p in range(nh // 2):
        start = (2 * p * dk) // LANE * LANE
        width = min(LANE, nh * dk - start)
        geo.append((start, width, (2 * p * dk - start, (2 * p + 1) * dk - start)))
    return geo


def _gla_chunk_kernel(mode, nh, dk, dv, nlev, nch, slab_ref, mats_ref, masks_ref, ind_ref, p1_ref, p2_ref, ng_ref,
                      o_ref, st_ref, s_scr):
    c = CHUNK
    ci = pl.program_id(1)

    @pl.when(ci == 0)
    def _():
        s_scr[...] = jnp.zeros_like(s_scr)

    hk, hv = nh * dk, nh * dv
    if mode == "gla":
        q = slab_ref[:, 0:hk] * (dk ** -0.5)
        k = slab_ref[:, hk:2 * hk]
        v = slab_ref[:, 2 * hk:2 * hk + hv]
        gate = slab_ref[:, 2 * hk + hv:2 * hk + 2 * hv]
        a_lr = slab_ref[:, 2 * hk + 2 * hv:2 * hk + 2 * hv + GLA_RANK]
        z = _dot_hp(a_lr, p1_ref[...]) + p2_ref[...]
        la = _log_sigmoid(z) * (1.0 / GLA_TAU)
    else:
        q = _silu(slab_ref[:, 0:hk])
        x = slab_ref[:, hk:2 * hk]
        v = slab_ref[:, 2 * hk:2 * hk + hv]
        gate = slab_ref[:, 2 * hk + hv:2 * hk + 2 * hv]
        lb = p1_ref[...]
        la = _log_sigmoid(x) + jnp.log1p(lb * jnp.exp(-x))
        k = (1.0 - lb) * _sigmoid(-x)

    bf = lambda t: t.astype(BF16)
    geo = _pair_geometry(nh, dk)
    npair = len(geo)
    chunks = range(nch)
    rows = [slice(i * c, (i + 1) * c) for i in chunks]
    vb = bf(v)

    qs, ks, q_in, k_rem, e_last = [], [], [], [], []
    for i in chunks:
        dall = _dot_sel_l(mats_ref[...], la[rows[i]])
        bcum = dall[nlev * c:(nlev + 1) * c]
        wl = [jnp.exp(-jnp.abs(dall[l * c:(l + 1) * c])) for l in range(nlev)]
        qc, kc = q[rows[i]], k[rows[i]]
        qs.append([bf(qc * w) for w in wl] + [bf(qc)])
        ks.append([bf(kc * w) for w in wl] + [bf(kc)])
        q_in.append(bf(qc * jnp.exp(bcum)))
        k_rem.append(bf(kc * jnp.exp(dall[(nlev + 1) * c:(nlev + 2) * c])))
        e_last.append(jnp.exp(bcum[c - 1:c]))

    out_cols = []
    for p in range(npair):
        start, width, offs = geo[p]
        blk = slice(start, start + width)
        vcol = slice(p * LANE, (p + 1) * LANE)
        lane = lax.broadcasted_iota(jnp.int32, (c, width), 1)
        hmask = [(lane >= o) & (lane < o + dk) for o in offs]
        zero = jnp.zeros((c, width), BF16)
        atts = []
        for i in chunks:
            att = None
            for l in range(nlev + 1):
                qb = qs[i][l][:, blk]
                lhs = jnp.concatenate([jnp.where(hmask[0], qb, zero), jnp.where(hmask[1], qb, zero)], axis=0)
                term = _dot_nt(lhs, ks[i][l][:, blk]) * masks_ref[l]
                att = term if att is None else att + term
            atts.append(bf(att))
        lane_v = lax.broadcasted_iota(jnp.int32, (c, LANE), 1)
        first = lane_v < dv
        intra = []
        for i in chunks:
            r2 = _dot(atts[i], vb[rows[i], vcol])
            intra.append(jnp.where(first, r2[0:c], r2[c:2 * c]))
        ri = lax.broadcasted_iota(jnp.int32, (2 * dv, width), 0)
        li = lax.broadcasted_iota(jnp.int32, (2 * dv, width), 1)
        smask = ((ri < dv) & (li >= offs[0]) & (li < offs[0] + dk)) | ((ri >= dv) & (li >= offs[1]) & (li < offs[1] + dk))
        upd = [jnp.where(smask, _dot_tn(vb[rows[i], vcol], k_rem[i][:, blk]), 0.0) for i in chunks]
        s = s_scr[p, :, 0:width]
        states = []
        for i in chunks:
            states.append(bf(s))
            s = s * e_last[i][:, blk] + upd[i]
        s_scr[p, :, 0:width] = s
        out_cols.append(jnp.concatenate(
            [intra[i] + _dot_nt(q_in[i][:, blk], states[i]) for i in chunks], axis=0))
    o = jnp.concatenate(out_cols, axis=-1)

    if mode == "gla":
        ms = _seg_sum(o * o, ind_ref[...]) * (1.0 / dv)
        o = o * lax.rsqrt(ms + EPS) * ng_ref[...] * _silu(gate)
    else:
        o = o * _sigmoid(gate)
        ms = _seg_sum(o * o, ind_ref[...]) * (1.0 / dv)
        o = o * lax.rsqrt(ms + EPS) * ng_ref[...]
    o_ref[...] = o.astype(o_ref.dtype)

    @pl.when(ci == pl.num_programs(1) - 1)
    def _():
        st_ref[0] = s_scr[...]


def _gla_chunk(mode, slab, nb, seq, p1, p2, ng):
    nh, dk, dv = (H_GLA, DK_GLA, DV_GLA) if mode == "gla" else (H_HG, DK_HG, DV_HG)
    w = slab.shape[1]
    nch = math.gcd(seq // CHUNK, GLA_CHUNKS_PER_STEP)
    sc = nch * CHUNK
    ns = seq // sc
    mats, masks, nlev = _decay_consts(CHUNK)
    ind = _seg_ind(nh, dv)
    ngt = jnp.tile(ng, (1, nh))
    geo = _pair_geometry(nh, dk)
    fix2 = lambda b, c: (0, 0)
    kern = functools.partial(_gla_chunk_kernel, mode, nh, dk, dv, nlev, nch)
    o, st = pl.pallas_call(
        kern,
        grid=(nb, ns),
        in_specs=[pl.BlockSpec((sc, w), lambda b, c: (b * ns + c, 0)),
                  pl.BlockSpec(mats.shape, fix2),
                  pl.BlockSpec(masks.shape, lambda b, c: (0, 0, 0)),
                  pl.BlockSpec(ind.shape, fix2),
                  pl.BlockSpec(p1.shape, fix2), pl.BlockSpec(p2.shape, fix2), pl.BlockSpec(ngt.shape, fix2)],
        out_specs=[pl.BlockSpec((sc, nh * dv), lambda b, c: (b * ns + c, 0)),
                   pl.BlockSpec((1, len(geo), 2 * dv, LANE), lambda b, c: (b, 0, 0, 0))],
        out_shape=[jax.ShapeDtypeStruct((nb * seq, nh * dv), BF16),
                   jax.ShapeDtypeStruct((nb, len(geo), 2 * dv, LANE), F32)],
        scratch_shapes=[pltpu.VMEM((len(geo), 2 * dv, LANE), F32)],
        compiler_params=_cparams(("parallel", "arbitrary")),
        name="chunk_" + mode,
    )(slab, mats, masks, ind, p1, p2, ngt)
    heads = []
    for p, (_, _, offs) in enumerate(geo):
        for j in range(2):
            heads.append(jnp.swapaxes(st[:, p, j * dv:(j + 1) * dv, offs[j]:offs[j] + dk], -1, -2))
    return o, jnp.stack(heads, axis=1)


def _dn_chunk_kernel(nch, slab_ref, abr_ref, lm_ref, umb_ref, ind_ref, cw_ref, pcol_ref, prow_ref, ng_ref,
                     o_ref, st_ref, s_scr, ext_scr):
    c = CHUNK
    sc = nch * c
    nh, dk, dv = H_DN, DK_DN, DV_DN
    npair = nh // 2
    ci = pl.program_id(1)

    @pl.when(ci == 0)
    def _():
        s_scr[...] = jnp.zeros_like(s_scr)
        ext_scr[0:8, :] = jnp.zeros((8, CONV_DIM), F32)

    ext_scr[8:8 + sc, :] = slab_ref[:, 0:CONV_DIM]
    conv = ext_scr[5:5 + sc, :] * cw_ref[0:1, :]
    for i in range(1, CONV_W):
        conv = conv + ext_scr[5 + i:5 + i + sc, :] * cw_ref[i:i + 1, :]
    ext_scr[0:8, :] = ext_scr[sc:sc + 8, :]
    qkv = _silu(conv)
    gate = slab_ref[:, CONV_DIM:DN_AB]
    hk = nh * dk
    q = qkv[:, 0:hk]
    k = qkv[:, hk:2 * hk]
    v = qkv[:, 2 * hk:2 * hk + nh * dv]
    q = q * lax.rsqrt(_seg_sum(q * q, ind_ref[...]) + EPS) * (dk ** -0.5)
    k = k * lax.rsqrt(_seg_sum(k * k, ind_ref[...]) + EPS)

    ab_c = slab_ref[:, DN_AB:DN_AB + LANE]
    g_c = -jnp.exp(pcol_ref[0:1, :]) * _softplus(ab_c + pcol_ref[1:2, :])
    beta_c = _sigmoid(ab_c)
    g_r = -jnp.exp(prow_ref[0]) * _softplus(abr_ref[0, 0] + prow_ref[1])
    gcum_r = _dot_sel_r(g_r, umb_ref[...])

    bf = lambda t: t.astype(BF16)
    rows = [slice(i * c, (i + 1) * c) for i in range(nch)]
    ri = lax.broadcasted_iota(jnp.int32, (2 * c, 2 * c), 0)
    cj = lax.broadcasted_iota(jnp.int32, (2 * c, 2 * c), 1)
    same = (ri < c) == (cj < c)
    tri = same & (ri >= cj)
    strict = same & (ri > cj)
    eye = (ri == cj).astype(F32)
    lane = lax.broadcasted_iota(jnp.int32, (c, LANE), 1)
    first = lane < dk
    zero = jnp.zeros((c, LANE), BF16)
    gcum_cs = [_dot_sel_l(lm_ref[...], g_c[rows[i]]) for i in range(nch)]

    def lanes2(col0, col1):
        return jnp.where(first, jnp.broadcast_to(col0, (c, LANE)), jnp.broadcast_to(col1, (c, LANE)))

    def stack2(col0, col1):
        return jnp.concatenate([jnp.broadcast_to(col0, (c, LANE)), jnp.broadcast_to(col1, (c, LANE))], axis=0)

    def rows2(x):
        return jnp.concatenate([jnp.where(first, x, zero), jnp.where(first, zero, x)], axis=0)

    pairs = range(npair)

    def prepare(i, p):
        h0, h1 = 2 * p, 2 * p + 1
        blk = slice(p * LANE, (p + 1) * LANE)
        gcc = gcum_cs[i]
        gc0, gc1 = gcc[:, h0:h0 + 1], gcc[:, h1:h1 + 1]
        gl0, gl1 = gcc[c - 1:c, h0:h0 + 1], gcc[c - 1:c, h1:h1 + 1]
        b0, b1 = beta_c[rows[i], nh + h0:nh + h0 + 1], beta_c[rows[i], nh + h1:nh + h1 + 1]
        gcr = gcum_r[p:p + 1, i * LANE:(i + 1) * LANE]
        gam = jnp.where(tri, jnp.exp(jnp.where(tri, stack2(gc0, gc1) - gcr, 0.0)), 0.0)
        kc, qc, vc = k[rows[i], blk], q[rows[i], blk], v[rows[i], blk]
        beta_l = lanes2(b0, b1)
        egc_l = jnp.exp(lanes2(gc0, gc1))
        kb = kc * beta_l
        return dict(
            gam=gam, k2=rows2(bf(kc)), kb2=rows2(bf(kb)), q2=rows2(bf(qc)), vb2=rows2(bf(vc * beta_l)),
            ke2=rows2(bf(kb * egc_l)), qe=bf(qc * egc_l),
            kd2=rows2(bf(kc * jnp.exp(lanes2(gl0, gl1) - lanes2(gc0, gc1)))),
            eglast=jnp.exp(stack2(gl0, gl1)))

    units = [(i, p) for i in range(nch) for p in pairs]
    pre = {u: prepare(*u) for u in units}
    ms = {u: jnp.where(strict, _dot_nt(pre[u]["kb2"], pre[u]["k2"]) * pre[u]["gam"], 0.0) for u in units}
    tinv = {u: eye - ms[u] for u in units}
    pw = {u: bf(ms[u]) for u in units}
    for _ in range(int(math.log2(c)) - 1):
        pw = {u: bf(_dot(pw[u], pw[u])) for u in units}
        tinv = {u: tinv[u] + _dot(bf(tinv[u]), pw[u]) for u in units}
    tb = {u: bf(tinv[u]) for u in units}
    uu = {u: _dot(tb[u], pre[u]["vb2"]) for u in units}
    ww = {u: bf(_dot(tb[u], pre[u]["ke2"])) for u in units}
    att = {u: bf(_dot_nt(pre[u]["q2"], pre[u]["k2"]) * pre[u]["gam"]) for u in units}

    states = [s_scr[p] for p in pairs]
    pieces = [[] for _ in pairs]
    for i in range(nch):
        for p in pairs:
            u = (i, p)
            sb = bf(states[p])
            vn = bf(uu[u] - _dot(ww[u], sb))
            o2 = _dot(att[u], vn)
            pieces[p].append(_dot(pre[u]["qe"], sb) + o2[0:c] + o2[c:2 * c])
            states[p] = states[p] * pre[u]["eglast"] + _dot_tn(pre[u]["kd2"], vn)
    for p in pairs:
        s_scr[p] = states[p]
    o = jnp.concatenate([jnp.concatenate(pc, axis=0) for pc in pieces], axis=-1)
    ms_o = _seg_sum(o * o, ind_ref[...]) * (1.0 / dv)
    o_ref[...] = (o * lax.rsqrt(ms_o + EPS) * ng_ref[...] * _silu(gate)).astype(o_ref.dtype)

    @pl.when(ci == pl.num_programs(1) - 1)
    def _():
        st_ref[0] = s_scr[...]


def _dn_chunk(slab, nb, seq, conv_w, a_log, dt_bias, ng):
    c = CHUNK
    nch = math.gcd(seq // c, DN_CHUNKS_PER_STEP)
    sc = nch * c
    ns = seq // sc
    npair = H_DN // 2
    nrow = 16
    idx = np.arange(c)
    lm = jnp.asarray((idx[None, :] <= idx[:, None]), BF16)
    um = (idx[:, None] <= idx[None, :]).astype(np.float32)
    umb = jnp.asarray(np.kron(np.eye(2 * nch, dtype=np.float32), um), BF16)
    ind = _seg_ind(H_DN, DK_DN)
    cw = jnp.pad(conv_w, ((0, 8 - CONV_W), (0, 0)))
    pad = lambda p: jnp.pad(p, (0, LANE - H_DN))
    pcol = jnp.zeros((8, LANE), F32).at[0].set(pad(a_log)).at[1].set(pad(dt_bias))
    a_cols = slab[:, DN_AB:DN_AB + H_DN].reshape(nb, ns, nch, c, npair, 2)
    a_rows = jnp.transpose(a_cols, (0, 1, 4, 2, 5, 3)).reshape(nb, ns, npair, nch * 2 * c)
    a_rows = jnp.pad(a_rows, ((0, 0), (0, 0), (0, nrow - npair), (0, 0)))
    rowp = lambda p: jnp.pad(jnp.broadcast_to(p.reshape(npair, 1, 2, 1), (npair, nch, 2, c)).reshape(npair, nch * 2 * c),
                             ((0, nrow - npair), (0, 0)))
    prow = jnp.stack([rowp(a_log), rowp(dt_bias)])
    ngt = jnp.tile(ng, (1, H_DN))
    fix2 = lambda b, c_: (0, 0)
    o, st = pl.pallas_call(
        functools.partial(_dn_chunk_kernel, nch),
        grid=(nb, ns),
        in_specs=[pl.BlockSpec((sc, DN_W), lambda b, c_: (b * ns + c_, 0)),
                  pl.BlockSpec((1, 1, nrow, nch * 2 * c), lambda b, c_: (b, c_, 0, 0)),
                  pl.BlockSpec((c, c), fix2), pl.BlockSpec(umb.shape, fix2), pl.BlockSpec(ind.shape, fix2),
                  pl.BlockSpec((8, CONV_DIM), fix2), pl.BlockSpec((8, LANE), fix2),
                  pl.BlockSpec(prow.shape, lambda b, c_: (0, 0, 0)), pl.BlockSpec(ngt.shape, fix2)],
        out_specs=[pl.BlockSpec((sc, H_DN * DV_DN), lambda b, c_: (b * ns + c_, 0)),
                   pl.BlockSpec((1, npair, 2 * DK_DN, 2 * DV_DN), lambda b, c_: (b, 0, 0, 0))],
        out_shape=[jax.ShapeDtypeStruct((nb * seq, H_DN * DV_DN), BF16),
                   jax.ShapeDtypeStruct((nb, npair, 2 * DK_DN, 2 * DV_DN), F32)],
        scratch_shapes=[pltpu.VMEM((npair, 2 * DK_DN, 2 * DV_DN), F32), pltpu.VMEM((sc + 8, CONV_DIM), F32)],
        compiler_params=_cparams(("parallel", "arbitrary")),
        name="chunk_dn",
    )(slab, a_rows, lm, umb, ind, cw, pcol, prow, ngt)
    heads = [st[:, p, j * DK_DN:(j + 1) * DK_DN, j * DV_DN:(j + 1) * DV_DN] for p in range(npair) for j in range(2)]
    return o, jnp.stack(heads, axis=1)


def _state_in(s_ref, dk):
    st = s_ref[...].T
    return st.reshape(dk, st.shape[0] // dk, st.shape[1])


def _state_out(so_ref, new):
    so_ref[...] = jnp.concatenate(new, axis=0).T


def _decode_gla_kernel(q_ref, k_ref, v_ref, g_ref, alr_ref, wa2t_ref, ba_ref, ng_ref, s_ref, o_ref, so_ref):
    dk = q_ref.shape[1]
    z = _dot_hp(wa2t_ref[0], alr_ref[...]) + ba_ref[0]
    dec = jnp.exp(_log_sigmoid(z) * (1.0 / GLA_TAU))
    q = q_ref[0] * (dk ** -0.5)
    k = k_ref[0]
    v = v_ref[0]
    st = _state_in(s_ref, dk)
    acc = jnp.zeros_like(v)
    new = []
    for d in range(dk):
        s_new = st[d] * dec[d:d + 1, :] + k[d:d + 1, :] * v
        new.append(s_new)
        acc = acc + q[d:d + 1, :] * s_new
    _state_out(so_ref, new)
    ms = jnp.mean(acc * acc, axis=0, keepdims=True)
    o_ref[0] = acc * lax.rsqrt(ms + EPS) * ng_ref[...] * _silu(g_ref[0])


def _decode_dn_kernel(x_ref, cb_ref, cw_ref, a_ref, b_ref, p_ref, g_ref, ng_ref, s_ref, o_ref, so_ref):
    dk = x_ref.shape[2]
    conv = x_ref[:, 0] * cw_ref[CONV_W - 1, :, 0]
    for i in range(CONV_W - 1):
        conv = conv + cb_ref[i, :, 0] * cw_ref[i, :, 0]
    qkv = _silu(conv)
    q, k, v = qkv[0], qkv[1], qkv[2]
    q = q * lax.rsqrt(jnp.sum(q * q, axis=0, keepdims=True) + EPS) * (dk ** -0.5)
    k = k * lax.rsqrt(jnp.sum(k * k, axis=0, keepdims=True) + EPS)
    eg = jnp.exp(-jnp.exp(p_ref[0, 0:1, :]) * _softplus(a_ref[0] + p_ref[0, 1:2, :]))
    beta = _sigmoid(b_ref[0])
    st = _state_in(s_ref, dk)
    ks = jnp.zeros_like(v)
    for d in range(dk):
        ks = ks + k[d:d + 1, :] * st[d]
    v_new = beta * (v - eg * ks)
    acc = jnp.zeros_like(v)
    new = []
    for d in range(dk):
        s_new = st[d] * eg + k[d:d + 1, :] * v_new
        new.append(s_new)
        acc = acc + q[d:d + 1, :] * s_new
    _state_out(so_ref, new)
    ms = jnp.mean(acc * acc, axis=0, keepdims=True)
    o_ref[0] = acc * lax.rsqrt(ms + EPS) * ng_ref[...] * _silu(g_ref[0])


def _decode_hg_kernel(q_ref, f_ref, v_ref, g_ref, lb_ref, ng_ref, s_ref, o_ref, so_ref):
    dk = q_ref.shape[1]
    x = f_ref[0]
    lb = lb_ref[0]
    f = jnp.exp(_log_sigmoid(x) + jnp.log1p(lb * jnp.exp(-x)))
    k = (1.0 - lb) * _sigmoid(-x)
    q = _silu(q_ref[0])
    v = v_ref[0]
    st = _state_in(s_ref, dk)
    acc = jnp.zeros_like(v)
    new = []
    for d in range(dk):
        s_new = st[d] * f[d:d + 1, :] + k[d:d + 1, :] * v
        new.append(s_new)
        acc = acc + q[d:d + 1, :] * s_new
    _state_out(so_ref, new)
    acc = acc * _sigmoid(g_ref[0])
    ms = jnp.mean(acc * acc, axis=0, keepdims=True)
    o_ref[0] = acc * lax.rsqrt(ms + EPS) * ng_ref[...]


def _head_call(kern, name, nh, dk, dv, nb, li, args, specs):
    o, s = pl.pallas_call(
        kern,
        grid=(nh,),
        in_specs=specs + [pl.BlockSpec((nb, dk * dv), lambda h: (li, h))],
        out_specs=[pl.BlockSpec((1, dv, nb), lambda h: (h, 0, 0)),
                   pl.BlockSpec((nb, dk * dv), lambda h: (0, h))],
        out_shape=[jax.ShapeDtypeStruct((nh, dv, nb), F32), jax.ShapeDtypeStruct((nb, nh * dk * dv), F32)],
        compiler_params=_cparams(("parallel",)),
        name=name,
    )(*args)
    return o.reshape(nh * dv, nb), s.reshape(nb, nh, dk, dv)


def _decode(pa, pb, pc, li, s_gla, s_dn, s_conv, s_hg, wa2, ba, conv_w, a_log, dt_bias, lb, nga, ngb, ngc):
    nb = pa.shape[0]
    bl = lambda p, *shape: jnp.broadcast_to(p.reshape(shape + (1,)), shape + (nb,))
    tr = lambda s: s.reshape(s.shape[0] * nb, -1)
    byh = lambda n: pl.BlockSpec((1, n, nb), lambda h: (h, 0, 0))
    fixed = lambda shape: pl.BlockSpec(shape, lambda h: (0,) * len(shape))

    nh, dk, dv = H_GLA, DK_GLA, DV_GLA
    hk, hv = nh * dk, nh * dv
    pt = pa.T
    args = (pt[0:hk].reshape(nh, dk, nb), pt[hk:2 * hk].reshape(nh, dk, nb),
            pt[2 * hk:2 * hk + hv].reshape(nh, dv, nb), pt[2 * hk + hv:2 * hk + 2 * hv].reshape(nh, dv, nb),
            pt[2 * hk + 2 * hv:2 * hk + 2 * hv + GLA_RANK], wa2.T.reshape(nh, dk, GLA_RANK),
            bl(ba, nh, dk), bl(nga, dv), tr(s_gla))
    specs = [byh(dk), byh(dk), byh(dv), byh(dv), fixed((GLA_RANK, nb)),
             pl.BlockSpec((1, dk, GLA_RANK), lambda h: (h, 0, 0)), byh(dk), fixed((dv, nb))]
    o_a, sa = _head_call(_decode_gla_kernel, "decode_gla", nh, dk, dv, nb, li, args, specs)

    nh, dk, dv = H_DN, DK_DN, DV_DN
    pt = pb.T
    x = pt[0:CONV_DIM].reshape(3, nh, dk, nb)
    cb = jnp.transpose(s_conv, (1, 2, 0)).reshape(CONV_W - 1, 3, nh, dk, nb)
    cw = bl(conv_w, CONV_W, 3, nh, dk)
    prm = jnp.stack([bl(a_log, nh), bl(dt_bias, nh)], axis=1)
    args = (x, cb, cw, pt[DN_AB:DN_AB + nh].reshape(nh, 1, nb), pt[DN_AB + nh:DN_AB + 2 * nh].reshape(nh, 1, nb),
            prm, pt[CONV_DIM:DN_AB].reshape(nh, dv, nb), bl(ngb, dv), tr(s_dn))
    specs = [pl.BlockSpec((3, 1, dk, nb), lambda h: (0, h, 0, 0)),
             pl.BlockSpec((CONV_W - 1, 3, 1, dk, nb), lambda h: (0, 0, h, 0, 0)),
             pl.BlockSpec((CONV_W, 3, 1, dk, nb), lambda h: (0, 0, h, 0, 0)),
             byh(1), byh(1), byh(2), byh(dv), fixed((dv, nb))]
    o_b, sb = _head_call(_decode_dn_kernel, "decode_dn", nh, dk, dv, nb, li, args, specs)

    nh, dk, dv = H_HG, DK_HG, DV_HG
    hk = nh * dk
    pt = pc.T
    args = (pt[0:hk].reshape(nh, dk, nb), pt[hk:2 * hk].reshape(nh, dk, nb),
            pt[2 * hk:3 * hk].reshape(nh, dv, nb), pt[3 * hk:4 * hk].reshape(nh, dv, nb),
            bl(lb, nh, dk), bl(ngc, dv), tr(s_hg))
    specs = [byh(dk), byh(dk), byh(dv), byh(dv), byh(dk), fixed((dv, nb))]
    o_c, sc = _head_call(_decode_hg_kernel, "decode_hgrn2", nh, dk, dv, nb, li, args, specs)

    return jnp.concatenate([o_a, o_b, o_c], axis=0).T.astype(BF16), sa, sb, sc


def _outproj_router_kernel(x_ref, oa_ref, ob_ref, oc_ref, w_ref, g_ref, rw_ref, rb_ref, ui_ref,
                           x1_ref, xa_ref, slot_ref, cnt_ref):
    tm, d = x_ref.shape
    na, nb_ = oa_ref.shape[1], ob_ref.shape[1]
    rs = min(tm, ROUTER_ROWS)
    blocks = [slice(r, r + rs) for r in range(0, tm, rs)]
    x1s = [x_ref[b, :] + _dot(oa_ref[b, :], w_ref[0:na, :]) + _dot(ob_ref[b, :], w_ref[na:na + nb_, :])
           + _dot(oc_ref[b, :], w_ref[na + nb_:, :]) for b in blocks]
    h2s = [_rms(x1, g_ref[...]).astype(BF16) for x1 in x1s]
    for b, x1, h2 in zip(blocks, x1s, h2s):
        x1_ref[b, :] = x1
        xa_ref[b, 0:d] = h2
    lane = lax.broadcasted_iota(jnp.int32, (rs, LANE), 1)
    neg = jnp.float32(-jnp.inf)
    big = jnp.int32(1 << 20)
    is_g = (lane >= N_EXPERTS) & (lane < N_EXPERTS + N_GROUPS)

    def route(logits):
        lg = jnp.where(is_g, logits, neg)
        mg = jnp.max(lg, axis=-1, keepdims=True)
        pg_top = 1.0 / jnp.sum(jnp.where(is_g, jnp.exp(lg - mg), 0.0), axis=-1, keepdims=True)
        g_idx = jnp.min(jnp.where(lg == mg, lane, big), axis=-1, keepdims=True) - N_EXPERTS
        in_grp = (lane >= g_idx * EXP_PER_GROUP) & (lane < (g_idx + 1) * EXP_PER_GROUP)
        le = jnp.where(in_grp, logits, neg)
        me = jnp.max(le, axis=-1, keepdims=True)
        ex = jnp.where(in_grp, jnp.exp(le - me), 0.0)
        pe = ex / jnp.sum(ex, axis=-1, keepdims=True)
        pe = jnp.where(in_grp, pe, -1.0)
        v1 = jnp.max(pe, axis=-1, keepdims=True)
        i1 = jnp.min(jnp.where(pe == v1, lane, big), axis=-1, keepdims=True)
        pe2 = jnp.where(lane == i1, -1.0, pe)
        v2 = jnp.max(pe2, axis=-1, keepdims=True)
        i2 = jnp.min(jnp.where(pe2 == v2, lane, big), axis=-1, keepdims=True)
        tot = v1 + v2
        gate = pg_top * (jnp.where(lane == i1, v1 / tot, 0.0) + jnp.where(lane == i2, v2 / tot, 0.0))
        return gate, (lane == g_idx).astype(BF16)

    routed = [route(_dot(h2, rw_ref[...]) + rb_ref[...]) for h2 in h2s]
    both, cnt = None, None
    for b, (gate, ind) in zip(blocks, routed):
        hi, mid, lo = _split3(gate)
        xa_ref[b, d:d + LANE] = (hi.astype(F32) + pltpu.roll(mid.astype(F32), N_EXPERTS, 1)
                                 + pltpu.roll(lo.astype(F32), 2 * N_EXPERTS, 1)).astype(BF16)
        part = _dot_tn(ind, ui_ref[b, :])
        tot_b = _dot(jnp.ones((8, rs), BF16), ind)
        both = part if both is None else both + part
        cnt = tot_b if cnt is None else cnt + tot_b
    slot_ref[0] = jnp.where(both[0:8, tm:2 * tm] > 0.5, both[0:8, 0:tm], -1.0)
    cnt_ref[0] = cnt.astype(jnp.int32)


def _outproj_router(x, oa, ob, oc, w, g, rw, rb, tm):
    t, d = x.shape
    nt = t // tm
    idx = np.arange(tm)
    ui = jnp.asarray(np.concatenate([idx[:, None] < idx[None, :], np.eye(tm, dtype=bool)], axis=1), BF16)
    row = lambda i: (i, 0)
    fix = lambda i: (0, 0)
    x1, xa, slot, cnt = pl.pallas_call(
        _outproj_router_kernel,
        grid=(nt,),
        in_specs=[pl.BlockSpec((tm, d), row), pl.BlockSpec((tm, oa.shape[1]), row),
                  pl.BlockSpec((tm, ob.shape[1]), row), pl.BlockSpec((tm, oc.shape[1]), row),
                  pl.BlockSpec(w.shape, fix), pl.BlockSpec((1, d), fix),
                  pl.BlockSpec(rw.shape, fix), pl.BlockSpec((1, LANE), fix), pl.BlockSpec(ui.shape, fix)],
        out_specs=[pl.BlockSpec((tm, d), row), pl.BlockSpec((tm, d + LANE), row),
                   pl.BlockSpec((1, 8, tm), lambda i: (i, 0, 0)), pl.BlockSpec((1, 8, LANE), lambda i: (i, 0, 0))],
        out_shape=[jax.ShapeDtypeStruct((t, d), F32), jax.ShapeDtypeStruct((t, d + LANE), BF16),
                   jax.ShapeDtypeStruct((nt, 8, tm), F32), jax.ShapeDtypeStruct((nt, 8, LANE), jnp.int32)],
        compiler_params=_cparams(("parallel",)),
        name="outproj_router",
    )(x, oa, ob, oc, w, g, rw, rb, ui)
    return x1, xa, slot[:, 0:N_GROUPS, None, :], cnt[:, 0, 0:N_GROUPS]


def _moe_kernel(final, rb0, rbx, cnt_ref, x1_ref, xa_ref, slot_ref, ex_ref, w1_ref, w3_ref, w2_ref, fg_ref, y_ref):
    i, g = pl.program_id(0), pl.program_id(1)
    tm, d = x1_ref.shape
    ne, _, f = w1_ref.shape

    @pl.when(g == 0)
    def _():
        y_ref[...] = x1_ref[...]

    cnt = cnt_ref[i, g]
    slot = slot_ref[0, 0]

    def rows(row0, rb):
        rid = (lax.broadcasted_iota(jnp.int32, (rb, tm), 0) + row0).astype(F32)
        sel = (rid == slot).astype(BF16)
        xg = _dot(sel, xa_ref[...])
        xb = xg[:, 0:d].astype(BF16)
        gexp = _dot(xg[:, d:d + LANE].astype(BF16), ex_ref[0])
        hid = [(_silu(_dot(xb, w1_ref[e])) * _dot(xb, w3_ref[e]) * gexp[:, e * f:(e + 1) * f]).astype(BF16)
               for e in range(ne)]
        yg = _dot(jnp.concatenate(hid, axis=-1), w2_ref[...].reshape(ne * f, d))
        y_ref[...] += _dot_tn(sel, yg.astype(BF16))

    @pl.when(cnt > 0)
    def _():
        rows(0, rb0)

    def extra(j, carry):
        rows(rb0 + j * rbx, rbx)
        return carry

    lax.fori_loop(0, (jnp.maximum(cnt - rb0, 0) + rbx - 1) // rbx, extra, 0)

    if final:
        @pl.when(g == pl.num_programs(1) - 1)
        def _():
            y_ref[...] = _rms(y_ref[...], fg_ref[...])


def _gate_expand(f):
    r = np.arange(LANE)
    e = r % N_EXPERTS
    col_e = np.arange(EXP_PER_GROUP * f) // f
    m = [(r[:, None] < 3 * N_EXPERTS) & (e[:, None] == g * EXP_PER_GROUP + col_e[None, :]) for g in range(N_GROUPS)]
    return jnp.asarray(np.stack(m), BF16)


def _moe(x1, xa, slot, cnt, w1, w3, w2, li, expand, fg, final, tm):
    t, d = x1.shape
    f = w1.shape[2]
    gf = EXP_PER_GROUP * f
    rb0 = min(tm, MOE_ROWS_FIRST)
    rbx = min(tm, MOE_ROWS_EXTRA)
    row = lambda i, g, c: (i, 0)
    grp = lambda i, g, c: (g, 0, 0)
    wgrp = lambda i, g, c: (li * N_GROUPS + g, 0, 0)
    return pl.pallas_call(
        functools.partial(_moe_kernel, final, rb0, rbx),
        grid_spec=pltpu.PrefetchScalarGridSpec(
            num_scalar_prefetch=1,
            grid=(t // tm, N_GROUPS),
            in_specs=[pl.BlockSpec((tm, d), row), pl.BlockSpec((tm, d + LANE), row),
                      pl.BlockSpec((1, 1, 1, tm), lambda i, g, c: (i, g, 0, 0)),
                      pl.BlockSpec((1, LANE, gf), grp),
                      pl.BlockSpec((EXP_PER_GROUP, d, f), wgrp), pl.BlockSpec((EXP_PER_GROUP, d, f), wgrp),
                      pl.BlockSpec((EXP_PER_GROUP, f, d), wgrp),
                      pl.BlockSpec((1, d), lambda i, g, c: (0, 0))],
            out_specs=pl.BlockSpec((tm, d), row)),
        out_shape=jax.ShapeDtypeStruct((t, d), F32),
        compiler_params=_cparams(("parallel", "arbitrary")),
        name="moe",
    )(cnt, x1, xa, slot, expand, w1, w3, w2, fg)


def kernel(x_prompt, x_sample, state_gla, state_dn, state_conv, state_hgrn, norm1_g, w_in, gla_wa2, gla_ba, gla_norm_g, dn_conv_w, dn_a_log, dn_dt_bias, dn_norm_g, hg_lb_logits, hg_norm_g, w_out, norm2_g, router_g_w, router_g_b, router_e_w, router_e_b, exp_w1, exp_w3, exp_w2, final_norm_g):
    nbp, seq, d = x_prompt.shape
    nbs = x_sample.shape[0]
    depth = w_in.shape[0]
    assert x_sample.shape[1] == 1 and seq % CHUNK == 0
    tp = nbp * seq
    xp = x_prompt.reshape(tp, d)
    xs = x_sample.reshape(nbs, d)

    sm = jax.nn.softmax(hg_lb_logits.astype(F32), axis=0)
    lb_all = jnp.maximum(jnp.cumsum(sm, axis=0) - sm[0:1], 0.0)

    tm_p = min(1024, tp)
    tm_moe = min(MOE_TILE, tp)
    expand = _gate_expand(exp_w1.shape[3])
    row2 = lambda v: v.reshape(1, -1)
    w_in_b = _perm_w_in(w_in)
    stack_e = lambda w: w.astype(BF16).reshape((w.shape[0] * w.shape[1],) + w.shape[2:])
    w1b, w3b, w2b = stack_e(exp_w1), stack_e(exp_w3), stack_e(exp_w2)
    gla_p, dn_p, conv_p, hg_p, gla_s, dn_s, conv_s, hg_s = ([] for _ in range(8))
    for li in range(depth):
        w_out_b = w_out[li].astype(BF16)
        rw = jnp.pad(jnp.concatenate([router_e_w[li], router_g_w[li]], axis=1),
                     ((0, 0), (0, LANE - N_EXPERTS - N_GROUPS))).astype(BF16)
        rb = jnp.pad(jnp.concatenate([router_e_b[li], router_g_b[li]]), (0, LANE - N_EXPERTS - N_GROUPS)).reshape(1, LANE)
        final = li == depth - 1
        g1, g2, fg = row2(norm1_g[li]), row2(norm2_g[li]), row2(final_norm_g)

        pa, pb, pc = _inproj(xp, g1, w_in_b, li, tm_p)
        oa, sa = _gla_chunk("gla", pa, nbp, seq, gla_wa2[li], row2(gla_ba[li]), row2(gla_norm_g[li]))
        ob, sb = _dn_chunk(pb, nbp, seq, dn_conv_w[li], dn_a_log[li], dn_dt_bias[li], row2(dn_norm_g[li]))
        oc, sc = _gla_chunk("hgrn2", pc, nbp, seq, row2(lb_all[li]), row2(lb_all[li]), row2(hg_norm_g[li]))
        gla_p.append(sa)
        dn_p.append(sb)
        conv_p.append(pb.reshape(nbp, seq, DN_W)[:, seq - (CONV_W - 1):, 0:CONV_DIM])
        hg_p.append(sc)
        x1, xa, slot, cnt = _outproj_router(xp, oa, ob, oc, w_out_b, g2, rw, rb, tm_moe)
        xp = _moe(x1, xa, slot, cnt, w1b, w3b, w2b, li, expand, fg, final, tm_moe)

        qa, qb, qc = _inproj(xs, g1, w_in_b, li, nbs)
        o_s, sa, sb, sc = _decode(qa, qb, qc, li, state_gla, state_dn, state_conv[li], state_hgrn,
                                  gla_wa2[li], gla_ba[li], dn_conv_w[li], dn_a_log[li], dn_dt_bias[li],
                                  lb_all[li], gla_norm_g[li], dn_norm_g[li], hg_norm_g[li])
        gla_s.append(sa)
        dn_s.append(sb)
        conv_s.append(jnp.concatenate([state_conv[li][:, 1:], qb[:, None, 0:CONV_DIM]], axis=1))
        hg_s.append(sc)
        na, nb_ = H_GLA * DV_GLA, H_DN * DV_DN
        x1, xa, slot, cnt = _outproj_router(xs, o_s[:, 0:na], o_s[:, na:na + nb_], o_s[:, na + nb_:], w_out_b, g2, rw, rb, nbs)
        xs = _moe(x1, xa, slot, cnt, w1b, w3b, w2b, li, expand, fg, final, nbs)

    st = lambda xs_, ref: jnp.stack(xs_).astype(ref.dtype)
    return (xp.reshape(nbp, seq, d), xs.reshape(nbs, 1, d),
            st(gla_p, state_gla), st(dn_p, state_dn), st(conv_p, state_conv), st(hg_p, state_hgrn),
            st(gla_s, state_gla), st(dn_s, state_dn), st(conv_s, state_conv), st(hg_s, state_hgrn))
```

```python
import functools
import math

import numpy as np
import jax
import jax.numpy as jnp
from jax import lax
from jax.experimental import pallas as pl
from jax.experimental.pallas import tpu as pltpu

F32 = jnp.float32
BF16 = jnp.bfloat16
EPS = 1e-6

H_GLA, DK_GLA, DV_GLA, GLA_RANK, GLA_TAU = 6, 32, 64, 16, 16.0
H_DN, DK_DN, DV_DN, CONV_W = 6, 64, 64, 4
CONV_DIM = H_DN * (2 * DK_DN + DV_DN)
H_HG, DK_HG, DV_HG = 4, 64, 64
N_GROUPS, EXP_PER_GROUP, TOP_K = 4, 8, 2
N_EXPERTS = N_GROUPS * EXP_PER_GROUP

LANE = 128
CHUNK = 64
GLA_CHUNKS_PER_STEP = 8
GLA_WAVE = 8
DN_CHUNKS_PER_STEP = 8
DN_WAVE = 4
MOE_TILE = 1024
MOE_ROWS_FIRST = 288
MOE_ROWS_EXTRA = 128
ROUTER_ROWS = 256
VMEM_LIMIT = 56 * 1024 * 1024

GLA_W = 1280
DN_W = 1664
HG_W = 1024
DN_AB = 1536


def _cparams(sem):
    return pltpu.CompilerParams(dimension_semantics=sem, vmem_limit_bytes=VMEM_LIMIT)


def _dot(a, b):
    return jnp.dot(a, b, preferred_element_type=F32)


def _dot_nt(a, b):
    return lax.dot_general(a, b, (((1,), (1,)), ((), ())), preferred_element_type=F32)


def _dot_tn(a, b):
    return lax.dot_general(a, b, (((0,), (0,)), ((), ())), preferred_element_type=F32)


def _split3(x):
    hi = x.astype(BF16)
    r = x - hi.astype(F32)
    mid = r.astype(BF16)
    lo = (r - mid.astype(F32)).astype(BF16)
    return hi, mid, lo


def _split2(x):
    hi = x.astype(BF16)
    lo = (x - hi.astype(F32)).astype(BF16)
    return hi, lo


def _dot_sel_l(m, x):
    hi, mid, lo = _split3(x)
    return _dot(m, hi) + _dot(m, mid) + _dot(m, lo)


def _dot_sel_r(x, m):
    hi, mid, lo = _split3(x)
    return _dot(hi, m) + _dot(mid, m) + _dot(lo, m)


def _dot_hp(a, b, fn=_dot):
    ah, al = _split2(a)
    bh, bl = _split2(b)
    return fn(ah, bh) + fn(ah, bl) + fn(al, bh)


def _rms(x, g):
    return x * lax.rsqrt(jnp.mean(x * x, axis=-1, keepdims=True) + EPS) * g


def _sigmoid(x):
    return 1.0 / (1.0 + jnp.exp(-x))


def _silu(x):
    return x * _sigmoid(x)


def _log_sigmoid(x):
    return jnp.minimum(x, 0.0) - jnp.log1p(jnp.exp(-jnp.abs(x)))


def _softplus(x):
    return jnp.maximum(x, 0.0) + jnp.log1p(jnp.exp(-jnp.abs(x)))


def _inproj_kernel(x_ref, g_ref, w_ref, oa_ref, ob_ref, oc_ref):
    h = _rms(x_ref[...], g_ref[...]).astype(BF16)
    oa_ref[...] = _dot(h, w_ref[0, :, 0:GLA_W])
    ob_ref[...] = _dot(h, w_ref[0, :, GLA_W:GLA_W + DN_W])
    oc_ref[...] = _dot(h, w_ref[0, :, GLA_W + DN_W:GLA_W + DN_W + HG_W])


def _inproj(x, g, w, li, tm):
    t, d = x.shape
    n = w.shape[2]
    row = lambda i: (i, 0)
    fix = lambda i: (0, 0)
    return pl.pallas_call(
        _inproj_kernel,
        grid=(t // tm,),
        in_specs=[pl.BlockSpec((tm, d), row), pl.BlockSpec((1, d), fix),
                  pl.BlockSpec((1, d, n), lambda i: (li, 0, 0), pipeline_mode=pl.Buffered(1))],
        out_specs=[pl.BlockSpec((tm, GLA_W), row), pl.BlockSpec((tm, DN_W), row), pl.BlockSpec((tm, HG_W), row)],
        out_shape=[jax.ShapeDtypeStruct((t, GLA_W), F32), jax.ShapeDtypeStruct((t, DN_W), F32),
                   jax.ShapeDtypeStruct((t, HG_W), F32)],
        compiler_params=_cparams(("parallel",)),
        name="inproj",
    )(x, g, w)


def _w_in_segments():
    sizes = (H_GLA * DK_GLA, H_GLA * DK_GLA, H_GLA * DV_GLA, GLA_RANK, H_GLA * DV_GLA,
             CONV_DIM, H_DN, H_DN, H_DN * DV_DN,
             H_HG * DK_HG, H_HG * DK_HG, H_HG * DV_HG, H_HG * DV_HG)
    offs = np.concatenate([[0], np.cumsum(sizes)]).tolist()
    g_q, g_k, g_v, g_a, g_g, d_qkv, d_a, d_b, d_g, h_q, h_f, h_i, h_g = [(offs[i], sizes[i]) for i in range(len(sizes))]
    return [g_q, g_k, g_v, g_g, g_a, (None, GLA_W - 1168),
            d_qkv, d_g, d_a, d_b, (None, DN_W - 1548),
            h_q, h_f, h_i, h_g]


def _perm_w_kernel(w_ref, o_ref):
    w = w_ref[0]
    cols = [jnp.zeros((w.shape[0], n), F32) if src is None else w[:, src:src + n] for src, n in _w_in_segments()]
    o_ref[0] = jnp.concatenate(cols, axis=1).astype(BF16)


def _perm_w_in(w):
    depth, d, n = w.shape
    n_out = GLA_W + DN_W + HG_W
    tr = min(256, d)
    return pl.pallas_call(
        _perm_w_kernel,
        grid=(depth, d // tr),
        in_specs=[pl.BlockSpec((1, tr, n), lambda l, r: (l, r, 0))],
        out_specs=pl.BlockSpec((1, tr, n_out), lambda l, r: (l, r, 0)),
        out_shape=jax.ShapeDtypeStruct((depth, d, n_out), BF16),
        compiler_params=_cparams(("parallel", "parallel")),
        name="perm_w_in",
    )(w)


def _decay_consts(c):
    n = int(math.log2(c))
    idx = np.arange(c)
    lm = (idx[None, :] <= idx[:, None]).astype(np.float32)
    mats, masks = [], []
    for l in range(1, n + 1):
        hs = c >> l
        bs = 2 * hs
        blk = idx // bs
        ref = blk * bs + hs - 1
        mats.append(lm - lm[ref])
        lower = (idx % bs) >= hs
        same = blk[:, None] == blk[None, :]
        masks.append((same & lower[:, None] & (~lower)[None, :]).astype(np.float32))
    mats.append(lm)
    mats.append(1.0 - lm)
    masks.append(np.eye(c, dtype=np.float32))
    masks = np.stack(masks)
    return (jnp.asarray(np.concatenate(mats, 0), BF16), jnp.asarray(np.concatenate([masks, masks], axis=1), F32), n)


def _seg_ind(nh, dv):
    h = np.arange(nh * dv) // dv
    return jnp.asarray(h[:, None] == h[None, :], BF16)


def _seg_sum(x, ind):
    hi, lo = _split2(x)
    return _dot(hi, ind) + _dot(lo, ind)


def _pair_geometry(nh, dk):
    geo = []
    for p in range(nh // 2):
        start = (2 * p * dk) // LANE * LANE
        width = min(LANE, nh * dk - start)
        geo.append((start, width, (2 * p * dk - start, (2 * p + 1) * dk - start)))
    return geo


def _gla_chunk_kernel(mode, nh, dk, dv, nlev, nch, slab_ref, mats_ref, masks_ref, ind_ref, p1_ref, p2_ref, ng_ref,
                      o_ref, st_ref, s_scr):
    c = CHUNK
    ci = pl.program_id(1)

    @pl.when(ci == 0)
    def _():
        s_scr[...] = jnp.zeros_like(s_scr)

    hk, hv = nh * dk, nh * dv
    if mode == "gla":
        q = slab_ref[:, 0:hk] * (dk ** -0.5)
        k = slab_ref[:, hk:2 * hk]
        v = slab_ref[:, 2 * hk:2 * hk + hv]
        gate = slab_ref[:, 2 * hk + hv:2 * hk + 2 * hv]
        a_lr = slab_ref[:, 2 * hk + 2 * hv:2 * hk + 2 * hv + GLA_RANK]
        z = _dot_hp(a_lr, p1_ref[...]) + p2_ref[...]
        la = _log_sigmoid(z) * (1.0 / GLA_TAU)
    else:
        q = _silu(slab_ref[:, 0:hk])
        x = slab_ref[:, hk:2 * hk]
        v = slab_ref[:, 2 * hk:2 * hk + hv]
        gate = slab_ref[:, 2 * hk + hv:2 * hk + 2 * hv]
        lb = p1_ref[...]
        la = _log_sigmoid(x) + jnp.log1p(lb * jnp.exp(-x))
        k = (1.0 - lb) * _sigmoid(-x)

    bf = lambda t: t.astype(BF16)
    geo = _pair_geometry(nh, dk)
    npair = len(geo)
    chunks = range(nch)
    rows = [slice(i * c, (i + 1) * c) for i in chunks]
    vb = bf(v)

    qs, ks, q_in, k_rem, e_last = {}, {}, {}, {}, {}

    def prepare(wave):
        for i in wave:
            dall = _dot_sel_l(mats_ref[...], la[rows[i]])
            bcum = dall[nlev * c:(nlev + 1) * c]
            wl = [jnp.exp(-jnp.abs(dall[l * c:(l + 1) * c])) for l in range(nlev)]
            qc, kc = q[rows[i]], k[rows[i]]
            qs[i] = [bf(qc * w) for w in wl] + [bf(qc)]
            ks[i] = [bf(kc * w) for w in wl] + [bf(kc)]
            q_in[i] = bf(qc * jnp.exp(bcum))
            k_rem[i] = bf(kc * jnp.exp(dall[(nlev + 1) * c:(nlev + 2) * c]))
            e_last[i] = jnp.exp(bcum[c - 1:c])
            yield

    lane_v = lax.broadcasted_iota(jnp.int32, (c, LANE), 1)
    first = lane_v < dv
    pair_consts = []
    for p in range(npair):
        start, width, offs = geo[p]
        lane = lax.broadcasted_iota(jnp.int32, (c, width), 1)
        ri = lax.broadcasted_iota(jnp.int32, (2 * dv, width), 0)
        li = lax.broadcasted_iota(jnp.int32, (2 * dv, width), 1)
        smask = ((ri < dv) & (li >= offs[0]) & (li < offs[0] + dk)) | ((ri >= dv) & (li >= offs[1]) & (li < offs[1] + dk))
        pair_consts.append((slice(start, start + width), slice(p * LANE, (p + 1) * LANE),
                            [(lane >= o) & (lane < o + dk) for o in offs], jnp.zeros((c, width), BF16), smask))
    carried = [s_scr[p, :, 0:geo[p][1]] for p in range(npair)]
    pieces = [[] for _ in range(npair)]

    def attend(wave):
        for p in range(npair):
            blk, vcol, hmask, zero, smask = pair_consts[p]
            atts = {}
            for i in wave:
                att = None
                for l in range(nlev + 1):
                    qb = qs[i][l][:, blk]
                    lhs = jnp.concatenate([jnp.where(hmask[0], qb, zero), jnp.where(hmask[1], qb, zero)], axis=0)
                    term = _dot_nt(lhs, ks[i][l][:, blk]) * masks_ref[l]
                    att = term if att is None else att + term
                atts[i] = bf(att)
                yield
            intra, upd, states = {}, {}, {}
            for i in wave:
                r2 = _dot(atts[i], vb[rows[i], vcol])
                intra[i] = jnp.where(first, r2[0:c], r2[c:2 * c])
                upd[i] = jnp.where(smask, _dot_tn(vb[rows[i], vcol], k_rem[i][:, blk]), 0.0)
            yield
            s = carried[p]
            for i in wave:
                states[i] = bf(s)
                s = s * e_last[i][:, blk] + upd[i]
            carried[p] = s
            for i in wave:
                pieces[p].append(intra[i] + _dot_nt(q_in[i][:, blk], states[i]))
            yield

    waves = [list(range(w0, min(w0 + GLA_WAVE, nch))) for w0 in range(0, nch, GLA_WAVE)]
    pending = iter(())
    for wave in waves:
        for _ in prepare(wave):
            next(pending, None)
            next(pending, None)
        for _ in pending:
            pass
        pending = attend(wave)
    for _ in pending:
        pass
    for p in range(npair):
        s_scr[p, :, 0:geo[p][1]] = carried[p]
    o = jnp.concatenate([jnp.concatenate(pc, axis=0) for pc in pieces], axis=-1)

    if mode == "gla":
        ms = _seg_sum(o * o, ind_ref[...]) * (1.0 / dv)
        o = o * lax.rsqrt(ms + EPS) * ng_ref[...] * _silu(gate)
    else:
        o = o * _sigmoid(gate)
        ms = _seg_sum(o * o, ind_ref[...]) * (1.0 / dv)
        o = o * lax.rsqrt(ms + EPS) * ng_ref[...]
    o_ref[...] = o.astype(o_ref.dtype)

    @pl.when(ci == pl.num_programs(1) - 1)
    def _():
        st_ref[0] = s_scr[...]


def _gla_chunk(mode, slab, nb, seq, p1, p2, ng):
    nh, dk, dv = (H_GLA, DK_GLA, DV_GLA) if mode == "gla" else (H_HG, DK_HG, DV_HG)
    w = slab.shape[1]
    nch = math.gcd(seq // CHUNK, GLA_CHUNKS_PER_STEP)
    sc = nch * CHUNK
    ns = seq // sc
    mats, masks, nlev = _decay_consts(CHUNK)
    ind = _seg_ind(nh, dv)
    ngt = jnp.tile(ng, (1, nh))
    geo = _pair_geometry(nh, dk)
    fix2 = lambda b, c: (0, 0)
    kern = functools.partial(_gla_chunk_kernel, mode, nh, dk, dv, nlev, nch)
    o, st = pl.pallas_call(
        kern,
        grid=(nb, ns),
        in_specs=[pl.BlockSpec((sc, w), lambda b, c: (b * ns + c, 0)),
                  pl.BlockSpec(mats.shape, fix2),
                  pl.BlockSpec(masks.shape, lambda b, c: (0, 0, 0)),
                  pl.BlockSpec(ind.shape, fix2),
                  pl.BlockSpec(p1.shape, fix2), pl.BlockSpec(p2.shape, fix2), pl.BlockSpec(ngt.shape, fix2)],
        out_specs=[pl.BlockSpec((sc, nh * dv), lambda b, c: (b * ns + c, 0)),
                   pl.BlockSpec((1, len(geo), 2 * dv, LANE), lambda b, c: (b, 0, 0, 0))],
        out_shape=[jax.ShapeDtypeStruct((nb * seq, nh * dv), BF16),
                   jax.ShapeDtypeStruct((nb, len(geo), 2 * dv, LANE), F32)],
        scratch_shapes=[pltpu.VMEM((len(geo), 2 * dv, LANE), F32)],
        compiler_params=_cparams(("parallel", "arbitrary")),
        name="chunk_" + mode,
    )(slab, mats, masks, ind, p1, p2, ngt)
    heads = []
    for p, (_, _, offs) in enumerate(geo):
        for j in range(2):
            heads.append(jnp.swapaxes(st[:, p, j * dv:(j + 1) * dv, offs[j]:offs[j] + dk], -1, -2))
    return o, jnp.stack(heads, axis=1)


def _dn_chunk_kernel(nch, slab_ref, abr_ref, lm_ref, umb_ref, ind_ref, cw_ref, pcol_ref, prow_ref, ng_ref,
                     o_ref, st_ref, s_scr, ext_scr):
    c = CHUNK
    sc = nch * c
    nh, dk, dv = H_DN, DK_DN, DV_DN
    npair = nh // 2
    ci = pl.program_id(1)

    @pl.when(ci == 0)
    def _():
        s_scr[...] = jnp.zeros_like(s_scr)
        ext_scr[0:8, :] = jnp.zeros((8, CONV_DIM), F32)

    ext_scr[8:8 + sc, :] = slab_ref[:, 0:CONV_DIM]
    conv = ext_scr[5:5 + sc, :] * cw_ref[0:1, :]
    for i in range(1, CONV_W):
        conv = conv + ext_scr[5 + i:5 + i + sc, :] * cw_ref[i:i + 1, :]
    ext_scr[0:8, :] = ext_scr[sc:sc + 8, :]
    qkv = _silu(conv)
    gate = slab_ref[:, CONV_DIM:DN_AB]
    hk = nh * dk
    q = qkv[:, 0:hk]
    k = qkv[:, hk:2 * hk]
    v = qkv[:, 2 * hk:2 * hk + nh * dv]
    q = q * lax.rsqrt(_seg_sum(q * q, ind_ref[...]) + EPS) * (dk ** -0.5)
    k = k * lax.rsqrt(_seg_sum(k * k, ind_ref[...]) + EPS)

    ab_c = slab_ref[:, DN_AB:DN_AB + LANE]
    g_c = -jnp.exp(pcol_ref[0:1, :]) * _softplus(ab_c + pcol_ref[1:2, :])
    beta_c = _sigmoid(ab_c)
    g_r = -jnp.exp(prow_ref[0]) * _softplus(abr_ref[0, 0] + prow_ref[1])
    gcum_r = _dot_sel_r(g_r, umb_ref[...])

    bf = lambda t: t.astype(BF16)
    rows = [slice(i * c, (i + 1) * c) for i in range(nch)]
    ri = lax.broadcasted_iota(jnp.int32, (2 * c, 2 * c), 0)
    cj = lax.broadcasted_iota(jnp.int32, (2 * c, 2 * c), 1)
    same = (ri < c) == (cj < c)
    tri = same & (ri >= cj)
    strict = same & (ri > cj)
    eye = (ri == cj).astype(F32)
    lane = lax.broadcasted_iota(jnp.int32, (c, LANE), 1)
    first = lane < dk
    zero = jnp.zeros((c, LANE), BF16)
    gcum_cs = [_dot_sel_l(lm_ref[...], g_c[rows[i]]) for i in range(nch)]

    def lanes2(col0, col1):
        return jnp.where(first, jnp.broadcast_to(col0, (c, LANE)), jnp.broadcast_to(col1, (c, LANE)))

    def stack2(col0, col1):
        return jnp.concatenate([jnp.broadcast_to(col0, (c, LANE)), jnp.broadcast_to(col1, (c, LANE))], axis=0)

    def rows2(x):
        return jnp.concatenate([jnp.where(first, x, zero), jnp.where(first, zero, x)], axis=0)

    pairs = range(npair)

    def prepare(i, p):
        h0, h1 = 2 * p, 2 * p + 1
        blk = slice(p * LANE, (p + 1) * LANE)
        gcc = gcum_cs[i]
        gc0, gc1 = gcc[:, h0:h0 + 1], gcc[:, h1:h1 + 1]
        gl0, gl1 = gcc[c - 1:c, h0:h0 + 1], gcc[c - 1:c, h1:h1 + 1]
        b0, b1 = beta_c[rows[i], nh + h0:nh + h0 + 1], beta_c[rows[i], nh + h1:nh + h1 + 1]
        gcr = gcum_r[p:p + 1, i * LANE:(i + 1) * LANE]
        gam = jnp.where(tri, jnp.exp(jnp.where(tri, stack2(gc0, gc1) - gcr, 0.0)), 0.0)
        kc, qc, vc = k[rows[i], blk], q[rows[i], blk], v[rows[i], blk]
        beta_l = lanes2(b0, b1)
        egc_l = jnp.exp(lanes2(gc0, gc1))
        kb = kc * beta_l
        return dict(
            gam=gam, k2=rows2(bf(kc)), kb2=rows2(bf(kb)), q2=rows2(bf(qc)), vb2=rows2(bf(vc * beta_l)),
            ke2=rows2(bf(kb * egc_l)), qe=bf(qc * egc_l),
            kd2=rows2(bf(kc * jnp.exp(lanes2(gl0, gl1) - lanes2(gc0, gc1)))),
            eglast=jnp.exp(stack2(gl0, gl1)))

    def independent(chunk_ids, pre):
        units = [(i, p) for i in chunk_ids for p in pairs]
        for u in units:
            pre[u] = prepare(*u)
        yield
        ms = {u: jnp.where(strict, _dot_nt(pre[u]["kb2"], pre[u]["k2"]) * pre[u]["gam"], 0.0) for u in units}
        tinv = {u: eye - ms[u] for u in units}
        pw = {u: bf(ms[u]) for u in units}
        yield
        for _ in range(int(math.log2(c)) - 1):
            pw = {u: bf(_dot(pw[u], pw[u])) for u in units}
            yield
            tinv = {u: tinv[u] + _dot(bf(tinv[u]), pw[u]) for u in units}
            yield
        for u in units:
            tb = bf(tinv[u])
            pre[u]["uu"] = _dot(tb, pre[u]["vb2"])
            pre[u]["ww"] = bf(_dot(tb, pre[u]["ke2"]))
            pre[u]["att"] = bf(_dot_nt(pre[u]["q2"], pre[u]["k2"]) * pre[u]["gam"])
        yield

    states = [s_scr[p] for p in pairs]
    pieces = [[] for _ in pairs]

    def recur(chunk_ids, pre):
        for i in chunk_ids:
            sbs = [bf(states[p]) for p in pairs]
            vns = [bf(pre[(i, p)]["uu"] - _dot(pre[(i, p)]["ww"], sbs[p])) for p in pairs]
            yield
            for p in pairs:
                u = pre[(i, p)]
                o2 = _dot(u["att"], vns[p])
                pieces[p].append(_dot(u["qe"], sbs[p]) + o2[0:c] + o2[c:2 * c])
                states[p] = states[p] * u["eglast"] + _dot_tn(u["kd2"], vns[p])
            yield

    waves = [list(range(w, min(w + DN_WAVE, nch))) for w in range(0, nch, DN_WAVE)]
    pre, pending = {}, iter(())
    for wave in waves:
        for _ in independent(wave, pre):
            next(pending, None)
        for _ in pending:
            pass
        pending = recur(wave, pre)
    for _ in pending:
        pass
    for p in pairs:
        s_scr[p] = states[p]
    o = jnp.concatenate([jnp.concatenate(pc, axis=0) for pc in pieces], axis=-1)
    ms_o = _seg_sum(o * o, ind_ref[...]) * (1.0 / dv)
    o_ref[...] = (o * lax.rsqrt(ms_o + EPS) * ng_ref[...] * _silu(gate)).astype(o_ref.dtype)

    @pl.when(ci == pl.num_programs(1) - 1)
    def _():
        st_ref[0] = s_scr[...]


def _dn_chunk(slab, nb, seq, conv_w, a_log, dt_bias, ng):
    c = CHUNK
    nch = math.gcd(seq // c, DN_CHUNKS_PER_STEP)
    sc = nch * c
    ns = seq // sc
    npair = H_DN // 2
    nrow = 16
    idx = np.arange(c)
    lm = jnp.asarray((idx[None, :] <= idx[:, None]), BF16)
    um = (idx[:, None] <= idx[None, :]).astype(np.float32)
    umb = jnp.asarray(np.kron(np.eye(2 * nch, dtype=np.float32), um), BF16)
    ind = _seg_ind(H_DN, DK_DN)
    cw = jnp.pad(conv_w, ((0, 8 - CONV_W), (0, 0)))
    pad = lambda p: jnp.pad(p, (0, LANE - H_DN))
    pcol = jnp.zeros((8, LANE), F32).at[0].set(pad(a_log)).at[1].set(pad(dt_bias))
    a_cols = slab[:, DN_AB:DN_AB + H_DN].reshape(nb, ns, nch, c, npair, 2)
    a_rows = jnp.transpose(a_cols, (0, 1, 4, 2, 5, 3)).reshape(nb, ns, npair, nch * 2 * c)
    a_rows = jnp.pad(a_rows, ((0, 0), (0, 0), (0, nrow - npair), (0, 0)))
    rowp = lambda p: jnp.pad(jnp.broadcast_to(p.reshape(npair, 1, 2, 1), (npair, nch, 2, c)).reshape(npair, nch * 2 * c),
                             ((0, nrow - npair), (0, 0)))
    prow = jnp.stack([rowp(a_log), rowp(dt_bias)])
    ngt = jnp.tile(ng, (1, H_DN))
    fix2 = lambda b, c_: (0, 0)
    o, st = pl.pallas_call(
        functools.partial(_dn_chunk_kernel, nch),
        grid=(nb, ns),
        in_specs=[pl.BlockSpec((sc, DN_W), lambda b, c_: (b * ns + c_, 0)),
                  pl.BlockSpec((1, 1, nrow, nch * 2 * c), lambda b, c_: (b, c_, 0, 0)),
                  pl.BlockSpec((c, c), fix2), pl.BlockSpec(umb.shape, fix2), pl.BlockSpec(ind.shape, fix2),
                  pl.BlockSpec((8, CONV_DIM), fix2), pl.BlockSpec((8, LANE), fix2),
                  pl.BlockSpec(prow.shape, lambda b, c_: (0, 0, 0)), pl.BlockSpec(ngt.shape, fix2)],
        out_specs=[pl.BlockSpec((sc, H_DN * DV_DN), lambda b, c_: (b * ns + c_, 0)),
                   pl.BlockSpec((1, npair, 2 * DK_DN, 2 * DV_DN), lambda b, c_: (b, 0, 0, 0))],
        out_shape=[jax.ShapeDtypeStruct((nb * seq, H_DN * DV_DN), BF16),
                   jax.ShapeDtypeStruct((nb, npair, 2 * DK_DN, 2 * DV_DN), F32)],
        scratch_shapes=[pltpu.VMEM((npair, 2 * DK_DN, 2 * DV_DN), F32), pltpu.VMEM((sc + 8, CONV_DIM), F32)],
        compiler_params=_cparams(("parallel", "arbitrary")),
        name="chunk_dn",
    )(slab, a_rows, lm, umb, ind, cw, pcol, prow, ngt)
    heads = [st[:, p, j * DK_DN:(j + 1) * DK_DN, j * DV_DN:(j + 1) * DV_DN] for p in range(npair) for j in range(2)]
    return o, jnp.stack(heads, axis=1)


def _state_in(s_ref, dk):
    st = s_ref[...].T
    return st.reshape(dk, st.shape[0] // dk, st.shape[1])


def _state_out(so_ref, new):
    so_ref[...] = jnp.concatenate(new, axis=0).T


def _decode_gla_kernel(q_ref, k_ref, v_ref, g_ref, alr_ref, wa2t_ref, ba_ref, ng_ref, s_ref, o_ref, so_ref):
    dk = q_ref.shape[1]
    z = _dot_hp(wa2t_ref[0], alr_ref[...]) + ba_ref[0]
    dec = jnp.exp(_log_sigmoid(z) * (1.0 / GLA_TAU))
    q = q_ref[0] * (dk ** -0.5)
    k = k_ref[0]
    v = v_ref[0]
    st = _state_in(s_ref, dk)
    acc = jnp.zeros_like(v)
    new = []
    for d in range(dk):
        s_new = st[d] * dec[d:d + 1, :] + k[d:d + 1, :] * v
        new.append(s_new)
        acc = acc + q[d:d + 1, :] * s_new
    _state_out(so_ref, new)
    ms = jnp.mean(acc * acc, axis=0, keepdims=True)
    o_ref[0] = acc * lax.rsqrt(ms + EPS) * ng_ref[...] * _silu(g_ref[0])


def _decode_dn_kernel(x_ref, cb_ref, cw_ref, a_ref, b_ref, p_ref, g_ref, ng_ref, s_ref, o_ref, so_ref):
    dk = x_ref.shape[2]
    conv = x_ref[:, 0] * cw_ref[CONV_W - 1, :, 0]
    for i in range(CONV_W - 1):
        conv = conv + cb_ref[i, :, 0] * cw_ref[i, :, 0]
    qkv = _silu(conv)
    q, k, v = qkv[0], qkv[1], qkv[2]
    q = q * lax.rsqrt(jnp.sum(q * q, axis=0, keepdims=True) + EPS) * (dk ** -0.5)
    k = k * lax.rsqrt(jnp.sum(k * k, axis=0, keepdims=True) + EPS)
    eg = jnp.exp(-jnp.exp(p_ref[0, 0:1, :]) * _softplus(a_ref[0] + p_ref[0, 1:2, :]))
    beta = _sigmoid(b_ref[0])
    st = _state_in(s_ref, dk)
    ks = jnp.zeros_like(v)
    for d in range(dk):
        ks = ks + k[d:d + 1, :] * st[d]
    v_new = beta * (v - eg * ks)
    acc = jnp.zeros_like(v)
    new = []
    for d in range(dk):
        s_new = st[d] * eg + k[d:d + 1, :] * v_new
        new.append(s_new)
        acc = acc + q[d:d + 1, :] * s_new
    _state_out(so_ref, new)
    ms = jnp.mean(acc * acc, axis=0, keepdims=True)
    o_ref[0] = acc * lax.rsqrt(ms + EPS) * ng_ref[...] * _silu(g_ref[0])


def _decode_hg_kernel(q_ref, f_ref, v_ref, g_ref, lb_ref, ng_ref, s_ref, o_ref, so_ref):
    dk = q_ref.shape[1]
    x = f_ref[0]
    lb = lb_ref[0]
    f = jnp.exp(_log_sigmoid(x) + jnp.log1p(lb * jnp.exp(-x)))
    k = (1.0 - lb) * _sigmoid(-x)
    q = _silu(q_ref[0])
    v = v_ref[0]
    st = _state_in(s_ref, dk)
    acc = jnp.zeros_like(v)
    new = []
    for d in range(dk):
        s_new = st[d] * f[d:d + 1, :] + k[d:d + 1, :] * v
        new.append(s_new)
        acc = acc + q[d:d + 1, :] * s_new
    _state_out(so_ref, new)
    acc = acc * _sigmoid(g_ref[0])
    ms = jnp.mean(acc * acc, axis=0, keepdims=True)
    o_ref[0] = acc * lax.rsqrt(ms + EPS) * ng_ref[...]


def _head_call(kern, name, nh, dk, dv, nb, li, args, specs):
    o, s = pl.pallas_call(
        kern,
        grid=(nh,),
        in_specs=specs + [pl.BlockSpec((nb, dk * dv), lambda h: (li, h))],
        out_specs=[pl.BlockSpec((1, dv, nb), lambda h: (h, 0, 0)),
                   pl.BlockSpec((nb, dk * dv), lambda h: (0, h))],
        out_shape=[jax.ShapeDtypeStruct((nh, dv, nb), F32), jax.ShapeDtypeStruct((nb, nh * dk * dv), F32)],
        compiler_params=_cparams(("parallel",)),
        name=name,
    )(*args)
    return o.reshape(nh * dv, nb), s.reshape(nb, nh, dk, dv)


def _decode(pa, pb, pc, li, s_gla, s_dn, s_conv, s_hg, wa2, ba, conv_w, a_log, dt_bias, lb, nga, ngb, ngc):
    nb = pa.shape[0]
    bl = lambda p, *shape: jnp.broadcast_to(p.reshape(shape + (1,)), shape + (nb,))
    tr = lambda s: s.reshape(s.shape[0] * nb, -1)
    byh = lambda n: pl.BlockSpec((1, n, nb), lambda h: (h, 0, 0))
    fixed = lambda shape: pl.BlockSpec(shape, lambda h: (0,) * len(shape))

    nh, dk, dv = H_GLA, DK_GLA, DV_GLA
    hk, hv = nh * dk, nh * dv
    pt = pa.T
    args = (pt[0:hk].reshape(nh, dk, nb), pt[hk:2 * hk].reshape(nh, dk, nb),
            pt[2 * hk:2 * hk + hv].reshape(nh, dv, nb), pt[2 * hk + hv:2 * hk + 2 * hv].reshape(nh, dv, nb),
            pt[2 * hk + 2 * hv:2 * hk + 2 * hv + GLA_RANK], wa2.T.reshape(nh, dk, GLA_RANK),
            bl(ba, nh, dk), bl(nga, dv), tr(s_gla))
    specs = [byh(dk), byh(dk), byh(dv), byh(dv), fixed((GLA_RANK, nb)),
             pl.BlockSpec((1, dk, GLA_RANK), lambda h: (h, 0, 0)), byh(dk), fixed((dv, nb))]
    o_a, sa = _head_call(_decode_gla_kernel, "decode_gla", nh, dk, dv, nb, li, args, specs)

    nh, dk, dv = H_DN, DK_DN, DV_DN
    pt = pb.T
    x = pt[0:CONV_DIM].reshape(3, nh, dk, nb)
    cb = jnp.transpose(s_conv, (1, 2, 0)).reshape(CONV_W - 1, 3, nh, dk, nb)
    cw = bl(conv_w, CONV_W, 3, nh, dk)
    prm = jnp.stack([bl(a_log, nh), bl(dt_bias, nh)], axis=1)
    args = (x, cb, cw, pt[DN_AB:DN_AB + nh].reshape(nh, 1, nb), pt[DN_AB + nh:DN_AB + 2 * nh].reshape(nh, 1, nb),
            prm, pt[CONV_DIM:DN_AB].reshape(nh, dv, nb), bl(ngb, dv), tr(s_dn))
    specs = [pl.BlockSpec((3, 1, dk, nb), lambda h: (0, h, 0, 0)),
             pl.BlockSpec((CONV_W - 1, 3, 1, dk, nb), lambda h: (0, 0, h, 0, 0)),
             pl.BlockSpec((CONV_W, 3, 1, dk, nb), lambda h: (0, 0, h, 0, 0)),
             byh(1), byh(1), byh(2), byh(dv), fixed((dv, nb))]
    o_b, sb = _head_call(_decode_dn_kernel, "decode_dn", nh, dk, dv, nb, li, args, specs)

    nh, dk, dv = H_HG, DK_HG, DV_HG
    hk = nh * dk
    pt = pc.T
    args = (pt[0:hk].reshape(nh, dk, nb), pt[hk:2 * hk].reshape(nh, dk, nb),
            pt[2 * hk:3 * hk].reshape(nh, dv, nb), pt[3 * hk:4 * hk].reshape(nh, dv, nb),
            bl(lb, nh, dk), bl(ngc, dv), tr(s_hg))
    specs = [byh(dk), byh(dk), byh(dv), byh(dv), byh(dk), fixed((dv, nb))]
    o_c, sc = _head_call(_decode_hg_kernel, "decode_hgrn2", nh, dk, dv, nb, li, args, specs)

    return jnp.concatenate([o_a, o_b, o_c], axis=0).T.astype(BF16), sa, sb, sc


def _outproj_router_kernel(x_ref, oa_ref, ob_ref, oc_ref, w_ref, g_ref, rw_ref, rb_ref, ui_ref,
                           x1_ref, xa_ref, slot_ref, cnt_ref):
    tm, d = x_ref.shape
    na, nb_ = oa_ref.shape[1], ob_ref.shape[1]
    rs = min(tm, ROUTER_ROWS)
    blocks = [slice(r, r + rs) for r in range(0, tm, rs)]
    x1s = [x_ref[b, :] + _dot(oa_ref[b, :], w_ref[0:na, :]) + _dot(ob_ref[b, :], w_ref[na:na + nb_, :])
           + _dot(oc_ref[b, :], w_ref[na + nb_:, :]) for b in blocks]
    h2s = [_rms(x1, g_ref[...]).astype(BF16) for x1 in x1s]
    for b, x1, h2 in zip(blocks, x1s, h2s):
        x1_ref[b, :] = x1
        xa_ref[b, 0:d] = h2
    lane = lax.broadcasted_iota(jnp.int32, (rs, LANE), 1)
    neg = jnp.float32(-jnp.inf)
    big = jnp.int32(1 << 20)
    is_g = (lane >= N_EXPERTS) & (lane < N_EXPERTS + N_GROUPS)

    def route(logits):
        lg = jnp.where(is_g, logits, neg)
        mg = jnp.max(lg, axis=-1, keepdims=True)
        pg_top = 1.0 / jnp.sum(jnp.where(is_g, jnp.exp(lg - mg), 0.0), axis=-1, keepdims=True)
        g_idx = jnp.min(jnp.where(lg == mg, lane, big), axis=-1, keepdims=True) - N_EXPERTS
        in_grp = (lane >= g_idx * EXP_PER_GROUP) & (lane < (g_idx + 1) * EXP_PER_GROUP)
        le = jnp.where(in_grp, logits, neg)
        me = jnp.max(le, axis=-1, keepdims=True)
        ex = jnp.where(in_grp, jnp.exp(le - me), 0.0)
        pe = ex / jnp.sum(ex, axis=-1, keepdims=True)
        pe = jnp.where(in_grp, pe, -1.0)
        v1 = jnp.max(pe, axis=-1, keepdims=True)
        i1 = jnp.min(jnp.where(pe == v1, lane, big), axis=-1, keepdims=True)
        pe2 = jnp.where(lane == i1, -1.0, pe)
        v2 = jnp.max(pe2, axis=-1, keepdims=True)
        i2 = jnp.min(jnp.where(pe2 == v2, lane, big), axis=-1, keepdims=True)
        tot = v1 + v2
        gate = pg_top * (jnp.where(lane == i1, v1 / tot, 0.0) + jnp.where(lane == i2, v2 / tot, 0.0))
        return gate, (lane == g_idx).astype(BF16)

    routed = [route(_dot(h2, rw_ref[...]) + rb_ref[...]) for h2 in h2s]
    both, cnt = None, None
    for b, (gate, ind) in zip(blocks, routed):
        hi, mid, lo = _split3(gate)
        xa_ref[b, d:d + LANE] = (hi.astype(F32) + pltpu.roll(mid.astype(F32), N_EXPERTS, 1)
                                 + pltpu.roll(lo.astype(F32), 2 * N_EXPERTS, 1)).astype(BF16)
        part = _dot_tn(ind, ui_ref[b, :])
        tot_b = _dot(jnp.ones((8, rs), BF16), ind)
        both = part if both is None else both + part
        cnt = tot_b if cnt is None else cnt + tot_b
    slot_ref[0] = jnp.where(both[0:8, tm:2 * tm] > 0.5, both[0:8, 0:tm], -1.0)
    cnt_ref[0] = cnt.astype(jnp.int32)


def _outproj_router(x, oa, ob, oc, w, g, rw, rb, tm):
    t, d = x.shape
    nt = t // tm
    idx = np.arange(tm)
    ui = jnp.asarray(np.concatenate([idx[:, None] < idx[None, :], np.eye(tm, dtype=bool)], axis=1), BF16)
    row = lambda i: (i, 0)
    fix = lambda i: (0, 0)
    x1, xa, slot, cnt = pl.pallas_call(
        _outproj_router_kernel,
        grid=(nt,),
        in_specs=[pl.BlockSpec((tm, d), row), pl.BlockSpec((tm, oa.shape[1]), row),
                  pl.BlockSpec((tm, ob.shape[1]), row), pl.BlockSpec((tm, oc.shape[1]), row),
                  pl.BlockSpec(w.shape, fix), pl.BlockSpec((1, d), fix),
                  pl.BlockSpec(rw.shape, fix), pl.BlockSpec((1, LANE), fix), pl.BlockSpec(ui.shape, fix)],
        out_specs=[pl.BlockSpec((tm, d), row), pl.BlockSpec((tm, d + LANE), row),
                   pl.BlockSpec((1, 8, tm), lambda i: (i, 0, 0)), pl.BlockSpec((1, 8, LANE), lambda i: (i, 0, 0))],
        out_shape=[jax.ShapeDtypeStruct((t, d), F32), jax.ShapeDtypeStruct((t, d + LANE), BF16),
                   jax.ShapeDtypeStruct((nt, 8, tm), F32), jax.ShapeDtypeStruct((nt, 8, LANE), jnp.int32)],
        compiler_params=_cparams(("parallel",)),
        name="outproj_router",
    )(x, oa, ob, oc, w, g, rw, rb, ui)
    return x1, xa, slot[:, 0:N_GROUPS, None, :], cnt[:, 0, 0:N_GROUPS]


def _moe_kernel(final, rb0, rbx, cnt_ref, x1_ref, xa_ref, slot_ref, ex_ref, w1_ref, w3_ref, w2_ref, fg_ref, y_ref):
    i, g = pl.program_id(0), pl.program_id(1)
    tm, d = x1_ref.shape
    ne, _, f = w1_ref.shape

    @pl.when(g == 0)
    def _():
        y_ref[...] = x1_ref[...]

    cnt = cnt_ref[i, g]
    slot = slot_ref[0, 0]

    def rows(row0, rb):
        rid = (lax.broadcasted_iota(jnp.int32, (rb, tm), 0) + row0).astype(F32)
        sel = (rid == slot).astype(BF16)
        xg = _dot(sel, xa_ref[...])
        xb = xg[:, 0:d].astype(BF16)
        gexp = _dot(xg[:, d:d + LANE].astype(BF16), ex_ref[0])
        up = [(_dot(xb, w1_ref[e]), _dot(xb, w3_ref[e])) for e in range(ne)]
        hid = [(_silu(a) * b * gexp[:, e * f:(e + 1) * f]).astype(BF16) for e, (a, b) in enumerate(up)]
        yg = _dot(jnp.concatenate(hid, axis=-1), w2_ref[...].reshape(ne * f, d))
        y_ref[...] += _dot_tn(sel, yg.astype(BF16))

    @pl.when(cnt > 0)
    def _():
        rows(0, rb0)

    def extra(j, carry):
        rows(rb0 + j * rbx, rbx)
        return carry

    lax.fori_loop(0, (jnp.maximum(cnt - rb0, 0) + rbx - 1) // rbx, extra, 0)

    if final:
        @pl.when(g == pl.num_programs(1) - 1)
        def _():
            y_ref[...] = _rms(y_ref[...], fg_ref[...])


def _gate_expand(f):
    r = np.arange(LANE)
    e = r % N_EXPERTS
    col_e = np.arange(EXP_PER_GROUP * f) // f
    m = [(r[:, None] < 3 * N_EXPERTS) & (e[:, None] == g * EXP_PER_GROUP + col_e[None, :]) for g in range(N_GROUPS)]
    return jnp.asarray(np.stack(m), BF16)


def _moe(x1, xa, slot, cnt, w1, w3, w2, li, expand, fg, final, tm):
    t, d = x1.shape
    f = w1.shape[2]
    gf = EXP_PER_GROUP * f
    rb0 = min(tm, MOE_ROWS_FIRST)
    rbx = min(tm, MOE_ROWS_EXTRA)
    row = lambda i, g, c: (i, 0)
    grp = lambda i, g, c: (g, 0, 0)
    wgrp = lambda i, g, c: (li * N_GROUPS + g, 0, 0)
    return pl.pallas_call(
        functools.partial(_moe_kernel, final, rb0, rbx),
        grid_spec=pltpu.PrefetchScalarGridSpec(
            num_scalar_prefetch=1,
            grid=(t // tm, N_GROUPS),
            in_specs=[pl.BlockSpec((tm, d), row), pl.BlockSpec((tm, d + LANE), row),
                      pl.BlockSpec((1, 1, 1, tm), lambda i, g, c: (i, g, 0, 0)),
                      pl.BlockSpec((1, LANE, gf), grp),
                      pl.BlockSpec((EXP_PER_GROUP, d, f), wgrp), pl.BlockSpec((EXP_PER_GROUP, d, f), wgrp),
                      pl.BlockSpec((EXP_PER_GROUP, f, d), wgrp),
                      pl.BlockSpec((1, d), lambda i, g, c: (0, 0))],
            out_specs=pl.BlockSpec((tm, d), row)),
        out_shape=jax.ShapeDtypeStruct((t, d), F32),
        compiler_params=_cparams(("parallel", "arbitrary")),
        name="moe",
    )(cnt, x1, xa, slot, expand, w1, w3, w2, fg)


def kernel(x_prompt, x_sample, state_gla, state_dn, state_conv, state_hgrn, norm1_g, w_in, gla_wa2, gla_ba, gla_norm_g, dn_conv_w, dn_a_log, dn_dt_bias, dn_norm_g, hg_lb_logits, hg_norm_g, w_out, norm2_g, router_g_w, router_g_b, router_e_w, router_e_b, exp_w1, exp_w3, exp_w2, final_norm_g):
    nbp, seq, d = x_prompt.shape
    nbs = x_sample.shape[0]
    depth = w_in.shape[0]
    assert x_sample.shape[1] == 1 and seq % CHUNK == 0
    tp = nbp * seq
    xp = x_prompt.reshape(tp, d)
    xs = x_sample.reshape(nbs, d)

    sm = jax.nn.softmax(hg_lb_logits.astype(F32), axis=0)
    lb_all = jnp.maximum(jnp.cumsum(sm, axis=0) - sm[0:1], 0.0)

    tm_p = min(1024, tp)
    tm_moe = min(MOE_TILE, tp)
    expand = _gate_expand(exp_w1.shape[3])
    row2 = lambda v: v.reshape(1, -1)
    w_in_b = _perm_w_in(w_in)
    stack_e = lambda w: w.astype(BF16).reshape((w.shape[0] * w.shape[1],) + w.shape[2:])
    w1b, w3b, w2b = stack_e(exp_w1), stack_e(exp_w3), stack_e(exp_w2)
    gla_p, dn_p, conv_p, hg_p, gla_s, dn_s, conv_s, hg_s = ([] for _ in range(8))
    for li in range(depth):
        w_out_b = w_out[li].astype(BF16)
        rw = jnp.pad(jnp.concatenate([router_e_w[li], router_g_w[li]], axis=1),
                     ((0, 0), (0, LANE - N_EXPERTS - N_GROUPS))).astype(BF16)
        rb = jnp.pad(jnp.concatenate([router_e_b[li], router_g_b[li]]), (0, LANE - N_EXPERTS - N_GROUPS)).reshape(1, LANE)
        final = li == depth - 1
        g1, g2, fg = row2(norm1_g[li]), row2(norm2_g[li]), row2(final_norm_g)

        pa, pb, pc = _inproj(xp, g1, w_in_b, li, tm_p)
        oa, sa = _gla_chunk("gla", pa, nbp, seq, gla_wa2[li], row2(gla_ba[li]), row2(gla_norm_g[li]))
        ob, sb = _dn_chunk(pb, nbp, seq, dn_conv_w[li], dn_a_log[li], dn_dt_bias[li], row2(dn_norm_g[li]))
        oc, sc = _gla_chunk("hgrn2", pc, nbp, seq, row2(lb_all[li]), row2(lb_all[li]), row2(hg_norm_g[li]))
        gla_p.append(sa)
        dn_p.append(sb)
        conv_p.append(pb.reshape(nbp, seq, DN_W)[:, seq - (CONV_W - 1):, 0:CONV_DIM])
        hg_p.append(sc)
        x1, xa, slot, cnt = _outproj_router(xp, oa, ob, oc, w_out_b, g2, rw, rb, tm_moe)
        xp = _moe(x1, xa, slot, cnt, w1b, w3b, w2b, li, expand, fg, final, tm_moe)

        qa, qb, qc = _inproj(xs, g1, w_in_b, li, nbs)
        o_s, sa, sb, sc = _decode(qa, qb, qc, li, state_gla, state_dn, state_conv[li], state_hgrn,
                                  gla_wa2[li], gla_ba[li], dn_conv_w[li], dn_a_log[li], dn_dt_bias[li],
                                  lb_all[li], gla_norm_g[li], dn_norm_g[li], hg_norm_g[li])
        gla_s.append(sa)
        dn_s.append(sb)
        conv_s.append(jnp.concatenate([state_conv[li][:, 1:], qb[:, None, 0:CONV_DIM]], axis=1))
        hg_s.append(sc)
        na, nb_ = H_GLA * DV_GLA, H_DN * DV_DN
        x1, xa, slot, cnt = _outproj_router(xs, o_s[:, 0:na], o_s[:, na:na + nb_], o_s[:, na + nb_:], w_out_b, g2, rw, rb, nbs)
        xs = _moe(x1, xa, slot, cnt, w1b, w3b, w2b, li, expand, fg, final, nbs)

    st = lambda xs_, ref: jnp.stack(xs_).astype(ref.dtype)
    return (xp.reshape(nbp, seq, d), xs.reshape(nbs, 1, d),
            st(gla_p, state_gla), st(dn_p, state_dn), st(conv_p, state_conv), st(hg_p, state_hgrn),
            st(gla_s, state_gla), st(dn_s, state_dn), st(conv_s, state_conv), st(hg_s, state_hgrn))
```

```python
import functools
import math

import numpy as np
import jax
import jax.numpy as jnp
from jax import lax
from jax.experimental import pallas as pl
from jax.experimental.pallas import tpu as pltpu

F32 = jnp.float32
BF16 = jnp.bfloat16
EPS = 1e-6

H_GLA, DK_GLA, DV_GLA, GLA_RANK, GLA_TAU = 6, 32, 64, 16, 16.0
H_DN, DK_DN, DV_DN, CONV_W = 6, 64, 64, 4
CONV_DIM = H_DN * (2 * DK_DN + DV_DN)
H_HG, DK_HG, DV_HG = 4, 64, 64
N_GROUPS, EXP_PER_GROUP, TOP_K = 4, 8, 2
N_EXPERTS = N_GROUPS * EXP_PER_GROUP

LANE = 128
CHUNK = 64
GLA_CHUNKS_PER_STEP = 8
GLA_WAVE = 8
DN_CHUNKS_PER_STEP = 8
DN_WAVE = 4
MOE_TILE = 1024
MOE_ROWS_FIRST = 288
MOE_ROWS_EXTRA = 128
ROUTER_ROWS = 256
VMEM_LIMIT = 56 * 1024 * 1024

GLA_W = 1280
DN_W = 1664
HG_W = 1024
DN_AB = 1536


def _cparams(sem):
    return pltpu.CompilerParams(dimension_semantics=sem, vmem_limit_bytes=VMEM_LIMIT)


def _dot(a, b):
    return jnp.dot(a, b, preferred_element_type=F32)


def _dot_nt(a, b):
    return lax.dot_general(a, b, (((1,), (1,)), ((), ())), preferred_element_type=F32)


def _dot_tn(a, b):
    return lax.dot_general(a, b, (((0,), (0,)), ((), ())), preferred_element_type=F32)


def _split3(x):
    hi = x.astype(BF16)
    r = x - hi.astype(F32)
    mid = r.astype(BF16)
    lo = (r - mid.astype(F32)).astype(BF16)
    return hi, mid, lo


def _split2(x):
    hi = x.astype(BF16)
    lo = (x - hi.astype(F32)).astype(BF16)
    return hi, lo


def _dot_sel_l(m, x):
    hi, mid, lo = _split3(x)
    return _dot(m, hi) + _dot(m, mid) + _dot(m, lo)


def _dot_sel_r(x, m):
    hi, mid, lo = _split3(x)
    return _dot(hi, m) + _dot(mid, m) + _dot(lo, m)


def _dot_hp(a, b, fn=_dot):
    ah, al = _split2(a)
    bh, bl = _split2(b)
    return fn(ah, bh) + fn(ah, bl) + fn(al, bh)


def _rms(x, g):
    return x * lax.rsqrt(jnp.mean(x * x, axis=-1, keepdims=True) + EPS) * g


def _sigmoid(x):
    return 1.0 / (1.0 + jnp.exp(-x))


def _silu(x):
    return x * _sigmoid(x)


def _log_sigmoid(x):
    return jnp.minimum(x, 0.0) - jnp.log1p(jnp.exp(-jnp.abs(x)))


def _softplus(x):
    return jnp.maximum(x, 0.0) + jnp.log1p(jnp.exp(-jnp.abs(x)))


def _inproj_kernel(x_ref, g_ref, w_ref, oa_ref, ob_ref, oc_ref):
    h = _rms(x_ref[...], g_ref[...]).astype(BF16)
    oa_ref[...] = _dot(h, w_ref[0, :, 0:GLA_W])
    ob_ref[...] = _dot(h, w_ref[0, :, GLA_W:GLA_W + DN_W])
    oc_ref[...] = _dot(h, w_ref[0, :, GLA_W + DN_W:GLA_W + DN_W + HG_W])


def _inproj(x, g, w, li, tm):
    t, d = x.shape
    n = w.shape[2]
    row = lambda i: (i, 0)
    fix = lambda i: (0, 0)
    return pl.pallas_call(
        _inproj_kernel,
        grid=(t // tm,),
        in_specs=[pl.BlockSpec((tm, d), row), pl.BlockSpec((1, d), fix),
                  pl.BlockSpec((1, d, n), lambda i: (li, 0, 0), pipeline_mode=pl.Buffered(1))],
        out_specs=[pl.BlockSpec((tm, GLA_W), row), pl.BlockSpec((tm, DN_W), row), pl.BlockSpec((tm, HG_W), row)],
        out_shape=[jax.ShapeDtypeStruct((t, GLA_W), F32), jax.ShapeDtypeStruct((t, DN_W), F32),
                   jax.ShapeDtypeStruct((t, HG_W), F32)],
        compiler_params=_cparams(("parallel",)),
        name="inproj",
    )(x, g, w)


def _w_in_segments():
    sizes = (H_GLA * DK_GLA, H_GLA * DK_GLA, H_GLA * DV_GLA, GLA_RANK, H_GLA * DV_GLA,
             CONV_DIM, H_DN, H_DN, H_DN * DV_DN,
             H_HG * DK_HG, H_HG * DK_HG, H_HG * DV_HG, H_HG * DV_HG)
    offs = np.concatenate([[0], np.cumsum(sizes)]).tolist()
    g_q, g_k, g_v, g_a, g_g, d_qkv, d_a, d_b, d_g, h_q, h_f, h_i, h_g = [(offs[i], sizes[i]) for i in range(len(sizes))]
    return [g_q, g_k, g_v, g_g, g_a, (None, GLA_W - 1168),
            d_qkv, d_g, d_a, d_b, (None, DN_W - 1548),
            h_q, h_f, h_i, h_g]


def _perm_w_kernel(w_ref, o_ref):
    w = w_ref[0]
    cols = [jnp.zeros((w.shape[0], n), F32) if src is None else w[:, src:src + n] for src, n in _w_in_segments()]
    o_ref[0] = jnp.concatenate(cols, axis=1).astype(BF16)


def _perm_w_in(w):
    depth, d, n = w.shape
    n_out = GLA_W + DN_W + HG_W
    tr = min(256, d)
    return pl.pallas_call(
        _perm_w_kernel,
        grid=(depth, d // tr),
        in_specs=[pl.BlockSpec((1, tr, n), lambda l, r: (l, r, 0))],
        out_specs=pl.BlockSpec((1, tr, n_out), lambda l, r: (l, r, 0)),
        out_shape=jax.ShapeDtypeStruct((depth, d, n_out), BF16),
        compiler_params=_cparams(("parallel", "parallel")),
        name="perm_w_in",
    )(w)


def _decay_consts(c):
    n = int(math.log2(c))
    idx = np.arange(c)
    lm = (idx[None, :] <= idx[:, None]).astype(np.float32)
    mats, masks = [], []
    for l in range(1, n + 1):
        hs = c >> l
        bs = 2 * hs
        blk = idx // bs
        ref = blk * bs + hs - 1
        mats.append(lm - lm[ref])
        lower = (idx % bs) >= hs
        same = blk[:, None] == blk[None, :]
        masks.append((same & lower[:, None] & (~lower)[None, :]).astype(np.float32))
    mats.append(lm)
    mats.append(1.0 - lm)
    masks.append(np.eye(c, dtype=np.float32))
    masks = np.stack(masks)
    return (jnp.asarray(np.concatenate(mats, 0), BF16), jnp.asarray(np.concatenate([masks, masks], axis=1), F32), n)


def _seg_ind(nh, dv):
    h = np.arange(nh * dv) // dv
    return jnp.asarray(h[:, None] == h[None, :], BF16)


def _seg_sum(x, ind):
    hi, lo = _split2(x)
    return _dot(hi, ind) + _dot(lo, ind)


def _pair_geometry(nh, dk):
    geo = []
    for p in range(nh // 2):
        start = (2 * p * dk) // LANE * LANE
        width = min(LANE, nh * dk - start)
        geo.append((start, width, (2 * p * dk - start, (2 * p + 1) * dk - start)))
    return geo


def _gla_chunk_kernel(mode, nh, dk, dv, nlev, nch, slab_ref, mats_ref, masks_ref, ind_ref, p1_ref, p2_ref, ng_ref,
                      o_ref, st_ref, s_scr):
    c = CHUNK
    ci = pl.program_id(1)

    @pl.when(ci == 0)
    def _():
        s_scr[...] = jnp.zeros_like(s_scr)

    hk, hv = nh * dk, nh * dv
    if mode == "gla":
        q = slab_ref[:, 0:hk] * (dk ** -0.5)
        k = slab_ref[:, hk:2 * hk]
        v = slab_ref[:, 2 * hk:2 * hk + hv]
        gate = slab_ref[:, 2 * hk + hv:2 * hk + 2 * hv]
        a_lr = slab_ref[:, 2 * hk + 2 * hv:2 * hk + 2 * hv + GLA_RANK]
        z = _dot_hp(a_lr, p1_ref[...]) + p2_ref[...]
        la = _log_sigmoid(z) * (1.0 / GLA_TAU)
    else:
        q = _silu(slab_ref[:, 0:hk])
        x = slab_ref[:, hk:2 * hk]
        v = slab_ref[:, 2 * hk:2 * hk + hv]
        gate = slab_ref[:, 2 * hk + hv:2 * hk + 2 * hv]
        lb = p1_ref[...]
        la = _log_sigmoid(x) + jnp.log1p(lb * jnp.exp(-x))
        k = (1.0 - lb) * _sigmoid(-x)

    bf = lambda t: t.astype(BF16)
    geo = _pair_geometry(nh, dk)
    npair = len(geo)
    chunks = range(nch)
    rows = [slice(i * c, (i + 1) * c) for i in chunks]
    vb = bf(v)

    qs, ks, q_in, k_rem, e_last = {}, {}, {}, {}, {}

    def prepare(wave):
        for i in wave:
            dall = _dot_sel_l(mats_ref[...], la[rows[i]])
            bcum = dall[nlev * c:(nlev + 1) * c]
            wl = [jnp.exp(-jnp.abs(dall[l * c:(l + 1) * c])) for l in range(nlev)]
            qc, kc = q[rows[i]], k[rows[i]]
            qs[i] = [bf(qc * w) for w in wl] + [bf(qc)]
            ks[i] = [bf(kc * w) for w in wl] + [bf(kc)]
            q_in[i] = bf(qc * jnp.exp(bcum))
            k_rem[i] = bf(kc * jnp.exp(dall[(nlev + 1) * c:(nlev + 2) * c]))
            e_last[i] = jnp.exp(bcum[c - 1:c])
            yield

    lane_v = lax.broadcasted_iota(jnp.int32, (c, LANE), 1)
    first = lane_v < dv
    pair_consts = []
    for p in range(npair):
        start, width, offs = geo[p]
        lane = lax.broadcasted_iota(jnp.int32, (c, width), 1)
        ri = lax.broadcasted_iota(jnp.int32, (2 * dv, width), 0)
        li = lax.broadcasted_iota(jnp.int32, (2 * dv, width), 1)
        smask = ((ri < dv) & (li >= offs[0]) & (li < offs[0] + dk)) | ((ri >= dv) & (li >= offs[1]) & (li < offs[1] + dk))
        pair_consts.append((slice(start, start + width), slice(p * LANE, (p + 1) * LANE),
                            [(lane >= o) & (lane < o + dk) for o in offs], jnp.zeros((c, width), BF16), smask))
    carried = [s_scr[p, :, 0:geo[p][1]] for p in range(npair)]
    pieces = [[] for _ in range(npair)]

    def attend(wave):
        for p in range(npair):
            blk, vcol, hmask, zero, smask = pair_consts[p]
            atts = {}
            for i in wave:
                att = None
                for l in range(nlev + 1):
                    qb = qs[i][l][:, blk]
                    lhs = jnp.concatenate([jnp.where(hmask[0], qb, zero), jnp.where(hmask[1], qb, zero)], axis=0)
                    term = _dot_nt(lhs, ks[i][l][:, blk]) * masks_ref[l]
                    att = term if att is None else att + term
                atts[i] = bf(att)
                yield
            intra, upd, states = {}, {}, {}
            for i in wave:
                r2 = _dot(atts[i], vb[rows[i], vcol])
                intra[i] = jnp.where(first, r2[0:c], r2[c:2 * c])
                upd[i] = jnp.where(smask, _dot_tn(vb[rows[i], vcol], k_rem[i][:, blk]), 0.0)
            yield
            s = carried[p]
            for i in wave:
                states[i] = bf(s)
                s = s * e_last[i][:, blk] + upd[i]
            carried[p] = s
            for i in wave:
                pieces[p].append(intra[i] + _dot_nt(q_in[i][:, blk], states[i]))
            yield

    waves = [list(range(w0, min(w0 + GLA_WAVE, nch))) for w0 in range(0, nch, GLA_WAVE)]
    pending = iter(())
    for wave in waves:
        for _ in prepare(wave):
            next(pending, None)
            next(pending, None)
        for _ in pending:
            pass
        pending = attend(wave)
    for _ in pending:
        pass
    for p in range(npair):
        s_scr[p, :, 0:geo[p][1]] = carried[p]
    o = jnp.concatenate([jnp.concatenate(pc, axis=0) for pc in pieces], axis=-1)

    if mode == "gla":
        ms = _seg_sum(o * o, ind_ref[...]) * (1.0 / dv)
        o = o * lax.rsqrt(ms + EPS) * ng_ref[...] * _silu(gate)
    else:
        o = o * _sigmoid(gate)
        ms = _seg_sum(o * o, ind_ref[...]) * (1.0 / dv)
        o = o * lax.rsqrt(ms + EPS) * ng_ref[...]
    o_ref[...] = o.astype(o_ref.dtype)

    @pl.when(ci == pl.num_programs(1) - 1)
    def _():
        st_ref[0] = s_scr[...]


def _gla_chunk(mode, slab, nb, seq, p1, p2, ng):
    nh, dk, dv = (H_GLA, DK_GLA, DV_GLA) if mode == "gla" else (H_HG, DK_HG, DV_HG)
    w = slab.shape[1]
    nch = math.gcd(seq // CHUNK, GLA_CHUNKS_PER_STEP)
    sc = nch * CHUNK
    ns = seq // sc
    mats, masks, nlev = _decay_consts(CHUNK)
    ind = _seg_ind(nh, dv)
    ngt = jnp.tile(ng, (1, nh))
    geo = _pair_geometry(nh, dk)
    fix2 = lambda b, c: (0, 0)
    kern = functools.partial(_gla_chunk_kernel, mode, nh, dk, dv, nlev, nch)
    o, st = pl.pallas_call(
        kern,
        grid=(nb, ns),
        in_specs=[pl.BlockSpec((sc, w), lambda b, c: (b * ns + c, 0)),
                  pl.BlockSpec(mats.shape, fix2),
                  pl.BlockSpec(masks.shape, lambda b, c: (0, 0, 0)),
                  pl.BlockSpec(ind.shape, fix2),
                  pl.BlockSpec(p1.shape, fix2), pl.BlockSpec(p2.shape, fix2), pl.BlockSpec(ngt.shape, fix2)],
        out_specs=[pl.BlockSpec((sc, nh * dv), lambda b, c: (b * ns + c, 0)),
                   pl.BlockSpec((1, len(geo), 2 * dv, LANE), lambda b, c: (b, 0, 0, 0))],
        out_shape=[jax.ShapeDtypeStruct((nb * seq, nh * dv), BF16),
                   jax.ShapeDtypeStruct((nb, len(geo), 2 * dv, LANE), F32)],
        scratch_shapes=[pltpu.VMEM((len(geo), 2 * dv, LANE), F32)],
        compiler_params=_cparams(("parallel", "arbitrary")),
        name="chunk_" + mode,
    )(slab, mats, masks, ind, p1, p2, ngt)
    heads = []
    for p, (_, _, offs) in enumerate(geo):
        for j in range(2):
            heads.append(jnp.swapaxes(st[:, p, j * dv:(j + 1) * dv, offs[j]:offs[j] + dk], -1, -2))
    return o, jnp.stack(heads, axis=1)


def _dn_chunk_kernel(nch, slab_ref, abr_ref, lm_ref, umb_ref, ind_ref, cw_ref, pcol_ref, prow_ref, ng_ref,
                     o_ref, st_ref, s_scr, ext_scr):
    c = CHUNK
    sc = nch * c
    nh, dk, dv = H_DN, DK_DN, DV_DN
    npair = nh // 2
    ci = pl.program_id(1)

    @pl.when(ci == 0)
    def _():
        s_scr[...] = jnp.zeros_like(s_scr)
        ext_scr[0:8, :] = jnp.zeros((8, CONV_DIM), F32)

    ext_scr[8:8 + sc, :] = slab_ref[:, 0:CONV_DIM]
    conv = ext_scr[5:5 + sc, :] * cw_ref[0:1, :]
    for i in range(1, CONV_W):
        conv = conv + ext_scr[5 + i:5 + i + sc, :] * cw_ref[i:i + 1, :]
    ext_scr[0:8, :] = ext_scr[sc:sc + 8, :]
    qkv = _silu(conv)
    gate = slab_ref[:, CONV_DIM:DN_AB]
    hk = nh * dk
    q = qkv[:, 0:hk]
    k = qkv[:, hk:2 * hk]
    v = qkv[:, 2 * hk:2 * hk + nh * dv]
    q = q * lax.rsqrt(_seg_sum(q * q, ind_ref[...]) + EPS) * (dk ** -0.5)
    k = k * lax.rsqrt(_seg_sum(k * k, ind_ref[...]) + EPS)

    ab_c = slab_ref[:, DN_AB:DN_AB + LANE]
    g_c = -jnp.exp(pcol_ref[0:1, :]) * _softplus(ab_c + pcol_ref[1:2, :])
    beta_c = _sigmoid(ab_c)
    g_r = -jnp.exp(prow_ref[0]) * _softplus(abr_ref[0, 0] + prow_ref[1])
    gcum_r = _dot_sel_r(g_r, umb_ref[...])

    bf = lambda t: t.astype(BF16)
    rows = [slice(i * c, (i + 1) * c) for i in range(nch)]
    ri = lax.broadcasted_iota(jnp.int32, (2 * c, 2 * c), 0)
    cj = lax.broadcasted_iota(jnp.int32, (2 * c, 2 * c), 1)
    same = (ri < c) == (cj < c)
    tri = same & (ri >= cj)
    strict = same & (ri > cj)
    eye = (ri == cj).astype(F32)
    lane = lax.broadcasted_iota(jnp.int32, (c, LANE), 1)
    first = lane < dk
    zero = jnp.zeros((c, LANE), BF16)
    gcum_cs = [_dot_sel_l(lm_ref[...], g_c[rows[i]]) for i in range(nch)]

    def lanes2(col0, col1):
        return jnp.where(first, jnp.broadcast_to(col0, (c, LANE)), jnp.broadcast_to(col1, (c, LANE)))

    def stack2(col0, col1):
        return jnp.concatenate([jnp.broadcast_to(col0, (c, LANE)), jnp.broadcast_to(col1, (c, LANE))], axis=0)

    def rows2(x):
        return jnp.concatenate([jnp.where(first, x, zero), jnp.where(first, zero, x)], axis=0)

    pairs = range(npair)

    def prepare(i, p):
        h0, h1 = 2 * p, 2 * p + 1
        blk = slice(p * LANE, (p + 1) * LANE)
        gcc = gcum_cs[i]
        gc0, gc1 = gcc[:, h0:h0 + 1], gcc[:, h1:h1 + 1]
        gl0, gl1 = gcc[c - 1:c, h0:h0 + 1], gcc[c - 1:c, h1:h1 + 1]
        b0, b1 = beta_c[rows[i], nh + h0:nh + h0 + 1], beta_c[rows[i], nh + h1:nh + h1 + 1]
        gcr = gcum_r[p:p + 1, i * LANE:(i + 1) * LANE]
        gam = jnp.where(tri, jnp.exp(jnp.where(tri, stack2(gc0, gc1) - gcr, 0.0)), 0.0)
        kc, qc, vc = k[rows[i], blk], q[rows[i], blk], v[rows[i], blk]
        beta_l = lanes2(b0, b1)
        egc_l = jnp.exp(lanes2(gc0, gc1))
        kb = kc * beta_l
        return dict(
            gam=gam, k2=rows2(bf(kc)), kb2=rows2(bf(kb)), q2=rows2(bf(qc)), vb2=rows2(bf(vc * beta_l)),
            ke2=rows2(bf(kb * egc_l)), qe=bf(qc * egc_l),
            kd2=rows2(bf(kc * jnp.exp(lanes2(gl0, gl1) - lanes2(gc0, gc1)))),
            eglast=jnp.exp(stack2(gl0, gl1)))

    def independent(chunk_ids, pre):
        units = [(i, p) for i in chunk_ids for p in pairs]
        for u in units:
            pre[u] = prepare(*u)
        yield
        ms = {u: jnp.where(strict, _dot_nt(pre[u]["kb2"], pre[u]["k2"]) * pre[u]["gam"], 0.0) for u in units}
        tinv = {u: eye - ms[u] for u in units}
        pw = {u: bf(ms[u]) for u in units}
        yield
        for _ in range(int(math.log2(c)) - 1):
            pw = {u: bf(_dot(pw[u], pw[u])) for u in units}
            yield
            tinv = {u: tinv[u] + _dot(bf(tinv[u]), pw[u]) for u in units}
            yield
        for u in units:
            tb = bf(tinv[u])
            pre[u]["uu"] = _dot(tb, pre[u]["vb2"])
            pre[u]["ww"] = bf(_dot(tb, pre[u]["ke2"]))
            pre[u]["att"] = bf(_dot_nt(pre[u]["q2"], pre[u]["k2"]) * pre[u]["gam"])
        yield

    states = [s_scr[p] for p in pairs]
    pieces = [[] for _ in pairs]

    def recur(chunk_ids, pre):
        for i in chunk_ids:
            sbs = [bf(states[p]) for p in pairs]
            vns = [bf(pre[(i, p)]["uu"] - _dot(pre[(i, p)]["ww"], sbs[p])) for p in pairs]
            yield
            for p in pairs:
                u = pre[(i, p)]
                o2 = _dot(u["att"], vns[p])
                pieces[p].append(_dot(u["qe"], sbs[p]) + o2[0:c] + o2[c:2 * c])
                states[p] = states[p] * u["eglast"] + _dot_tn(u["kd2"], vns[p])
            yield

    waves = [list(range(w, min(w + DN_WAVE, nch))) for w in range(0, nch, DN_WAVE)]
    pre, pending = {}, iter(())
    for wave in waves:
        for _ in independent(wave, pre):
            next(pending, None)
        for _ in pending:
            pass
        pending = recur(wave, pre)
    for _ in pending:
        pass
    for p in pairs:
        s_scr[p] = states[p]
    o = jnp.concatenate([jnp.concatenate(pc, axis=0) for pc in pieces], axis=-1)
    ms_o = _seg_sum(o * o, ind_ref[...]) * (1.0 / dv)
    o_ref[...] = (o * lax.rsqrt(ms_o + EPS) * ng_ref[...] * _silu(gate)).astype(o_ref.dtype)

    @pl.when(ci == pl.num_programs(1) - 1)
    def _():
        st_ref[0] = s_scr[...]


def _dn_chunk(slab, nb, seq, conv_w, a_log, dt_bias, ng):
    c = CHUNK
    nch = math.gcd(seq // c, DN_CHUNKS_PER_STEP)
    sc = nch * c
    ns = seq // sc
    npair = H_DN // 2
    nrow = 16
    idx = np.arange(c)
    lm = jnp.asarray((idx[None, :] <= idx[:, None]), BF16)
    um = (idx[:, None] <= idx[None, :]).astype(np.float32)
    umb = jnp.asarray(np.kron(np.eye(2 * nch, dtype=np.float32), um), BF16)
    ind = _seg_ind(H_DN, DK_DN)
    cw = jnp.pad(conv_w, ((0, 8 - CONV_W), (0, 0)))
    pad = lambda p: jnp.pad(p, (0, LANE - H_DN))
    pcol = jnp.zeros((8, LANE), F32).at[0].set(pad(a_log)).at[1].set(pad(dt_bias))
    a_cols = slab[:, DN_AB:DN_AB + H_DN].reshape(nb, ns, nch, c, npair, 2)
    a_rows = jnp.transpose(a_cols, (0, 1, 4, 2, 5, 3)).reshape(nb, ns, npair, nch * 2 * c)
    a_rows = jnp.pad(a_rows, ((0, 0), (0, 0), (0, nrow - npair), (0, 0)))
    rowp = lambda p: jnp.pad(jnp.broadcast_to(p.reshape(npair, 1, 2, 1), (npair, nch, 2, c)).reshape(npair, nch * 2 * c),
                             ((0, nrow - npair), (0, 0)))
    prow = jnp.stack([rowp(a_log), rowp(dt_bias)])
    ngt = jnp.tile(ng, (1, H_DN))
    fix2 = lambda b, c_: (0, 0)
    o, st = pl.pallas_call(
        functools.partial(_dn_chunk_kernel, nch),
        grid=(nb, ns),
        in_specs=[pl.BlockSpec((sc, DN_W), lambda b, c_: (b * ns + c_, 0)),
                  pl.BlockSpec((1, 1, nrow, nch * 2 * c), lambda b, c_: (b, c_, 0, 0)),
                  pl.BlockSpec((c, c), fix2), pl.BlockSpec(umb.shape, fix2), pl.BlockSpec(ind.shape, fix2),
                  pl.BlockSpec((8, CONV_DIM), fix2), pl.BlockSpec((8, LANE), fix2),
                  pl.BlockSpec(prow.shape, lambda b, c_: (0, 0, 0)), pl.BlockSpec(ngt.shape, fix2)],
        out_specs=[pl.BlockSpec((sc, H_DN * DV_DN), lambda b, c_: (b * ns + c_, 0)),
                   pl.BlockSpec((1, npair, 2 * DK_DN, 2 * DV_DN), lambda b, c_: (b, 0, 0, 0))],
        out_shape=[jax.ShapeDtypeStruct((nb * seq, H_DN * DV_DN), BF16),
                   jax.ShapeDtypeStruct((nb, npair, 2 * DK_DN, 2 * DV_DN), F32)],
        scratch_shapes=[pltpu.VMEM((npair, 2 * DK_DN, 2 * DV_DN), F32), pltpu.VMEM((sc + 8, CONV_DIM), F32)],
        compiler_params=_cparams(("parallel", "arbitrary")),
        name="chunk_dn",
    )(slab, a_rows, lm, umb, ind, cw, pcol, prow, ngt)
    heads = [st[:, p, j * DK_DN:(j + 1) * DK_DN, j * DV_DN:(j + 1) * DV_DN] for p in range(npair) for j in range(2)]
    return o, jnp.stack(heads, axis=1)


def _state_in(s_ref, dk):
    return s_ref[0]


def _state_out(so_ref, new):
    for d, s_new in enumerate(new):
        so_ref[0, d] = s_new


def _decode_gla_kernel(q_ref, k_ref, v_ref, g_ref, alr_ref, wa2t_ref, ba_ref, ng_ref, s_ref, o_ref, so_ref):
    dk = q_ref.shape[1]
    z = _dot_hp(wa2t_ref[0], alr_ref[...]) + ba_ref[0]
    dec = jnp.exp(_log_sigmoid(z) * (1.0 / GLA_TAU))
    q = q_ref[0] * (dk ** -0.5)
    k = k_ref[0]
    v = v_ref[0]
    st = _state_in(s_ref, dk)
    acc = jnp.zeros_like(v)
    new = []
    for d in range(dk):
        s_new = st[d] * dec[d:d + 1, :] + k[d:d + 1, :] * v
        new.append(s_new)
        acc = acc + q[d:d + 1, :] * s_new
    _state_out(so_ref, new)
    ms = jnp.mean(acc * acc, axis=0, keepdims=True)
    o_ref[0] = acc * lax.rsqrt(ms + EPS) * ng_ref[...] * _silu(g_ref[0])


def _decode_dn_kernel(x_ref, cb_ref, cw_ref, a_ref, b_ref, p_ref, g_ref, ng_ref, s_ref, o_ref, so_ref):
    dk = x_ref.shape[2]
    conv = x_ref[:, 0] * cw_ref[CONV_W - 1, :, 0]
    for i in range(CONV_W - 1):
        conv = conv + cb_ref[i, :, 0] * cw_ref[i, :, 0]
    qkv = _silu(conv)
    q, k, v = qkv[0], qkv[1], qkv[2]
    q = q * lax.rsqrt(jnp.sum(q * q, axis=0, keepdims=True) + EPS) * (dk ** -0.5)
    k = k * lax.rsqrt(jnp.sum(k * k, axis=0, keepdims=True) + EPS)
    eg = jnp.exp(-jnp.exp(p_ref[0, 0:1, :]) * _softplus(a_ref[0] + p_ref[0, 1:2, :]))
    beta = _sigmoid(b_ref[0])
    st = _state_in(s_ref, dk)
    ks = jnp.zeros_like(v)
    for d in range(dk):
        ks = ks + k[d:d + 1, :] * st[d]
    v_new = beta * (v - eg * ks)
    acc = jnp.zeros_like(v)
    new = []
    for d in range(dk):
        s_new = st[d] * eg + k[d:d + 1, :] * v_new
        new.append(s_new)
        acc = acc + q[d:d + 1, :] * s_new
    _state_out(so_ref, new)
    ms = jnp.mean(acc * acc, axis=0, keepdims=True)
    o_ref[0] = acc * lax.rsqrt(ms + EPS) * ng_ref[...] * _silu(g_ref[0])


def _decode_hg_kernel(q_ref, f_ref, v_ref, g_ref, lb_ref, ng_ref, s_ref, o_ref, so_ref):
    dk = q_ref.shape[1]
    x = f_ref[0]
    lb = lb_ref[0]
    f = jnp.exp(_log_sigmoid(x) + jnp.log1p(lb * jnp.exp(-x)))
    k = (1.0 - lb) * _sigmoid(-x)
    q = _silu(q_ref[0])
    v = v_ref[0]
    st = _state_in(s_ref, dk)
    acc = jnp.zeros_like(v)
    new = []
    for d in range(dk):
        s_new = st[d] * f[d:d + 1, :] + k[d:d + 1, :] * v
        new.append(s_new)
        acc = acc + q[d:d + 1, :] * s_new
    _state_out(so_ref, new)
    acc = acc * _sigmoid(g_ref[0])
    ms = jnp.mean(acc * acc, axis=0, keepdims=True)
    o_ref[0] = acc * lax.rsqrt(ms + EPS) * ng_ref[...]


def _head_call(kern, name, nh, dk, dv, nb, li, args, specs):
    o, s = pl.pallas_call(
        kern,
        grid=(nh,),
        in_specs=specs + [pl.BlockSpec((1, dk, dv, nb), lambda h: (li * nh + h, 0, 0, 0))],
        out_specs=[pl.BlockSpec((1, dv, nb), lambda h: (h, 0, 0)),
                   pl.BlockSpec((1, dk, dv, nb), lambda h: (h, 0, 0, 0))],
        out_shape=[jax.ShapeDtypeStruct((nh, dv, nb), F32), jax.ShapeDtypeStruct((nh, dk, dv, nb), F32)],
        compiler_params=_cparams(("parallel",)),
        name=name,
    )(*args)
    return o.reshape(nh * dv, nb), jnp.transpose(s, (3, 0, 1, 2))


def _decode(pa, pb, pc, li, s_gla, s_dn, s_conv, s_hg, wa2, ba, conv_w, a_log, dt_bias, lb, nga, ngb, ngc):
    nb = pa.shape[0]
    bl = lambda p, *shape: jnp.broadcast_to(p.reshape(shape + (1,)), shape + (nb,))
    tr = lambda s: jnp.transpose(s, (0, 2, 3, 4, 1)).reshape((s.shape[0] * s.shape[2],) + s.shape[3:] + (nb,))
    byh = lambda n: pl.BlockSpec((1, n, nb), lambda h: (h, 0, 0))
    fixed = lambda shape: pl.BlockSpec(shape, lambda h: (0,) * len(shape))

    nh, dk, dv = H_GLA, DK_GLA, DV_GLA
    hk, hv = nh * dk, nh * dv
    pt = pa.T
    args = (pt[0:hk].reshape(nh, dk, nb), pt[hk:2 * hk].reshape(nh, dk, nb),
            pt[2 * hk:2 * hk + hv].reshape(nh, dv, nb), pt[2 * hk + hv:2 * hk + 2 * hv].reshape(nh, dv, nb),
            pt[2 * hk + 2 * hv:2 * hk + 2 * hv + GLA_RANK], wa2.T.reshape(nh, dk, GLA_RANK),
            bl(ba, nh, dk), bl(nga, dv), tr(s_gla))
    specs = [byh(dk), byh(dk), byh(dv), byh(dv), fixed((GLA_RANK, nb)),
             pl.BlockSpec((1, dk, GLA_RANK), lambda h: (h, 0, 0)), byh(dk), fixed((dv, nb))]
    o_a, sa = _head_call(_decode_gla_kernel, "decode_gla", nh, dk, dv, nb, li, args, specs)

    nh, dk, dv = H_DN, DK_DN, DV_DN
    pt = pb.T
    x = pt[0:CONV_DIM].reshape(3, nh, dk, nb)
    cb = jnp.transpose(s_conv, (1, 2, 0)).reshape(CONV_W - 1, 3, nh, dk, nb)
    cw = bl(conv_w, CONV_W, 3, nh, dk)
    prm = jnp.stack([bl(a_log, nh), bl(dt_bias, nh)], axis=1)
    args = (x, cb, cw, pt[DN_AB:DN_AB + nh].reshape(nh, 1, nb), pt[DN_AB + nh:DN_AB + 2 * nh].reshape(nh, 1, nb),
            prm, pt[CONV_DIM:DN_AB].reshape(nh, dv, nb), bl(ngb, dv), tr(s_dn))
    specs = [pl.BlockSpec((3, 1, dk, nb), lambda h: (0, h, 0, 0)),
             pl.BlockSpec((CONV_W - 1, 3, 1, dk, nb), lambda h: (0, 0, h, 0, 0)),
             pl.BlockSpec((CONV_W, 3, 1, dk, nb), lambda h: (0, 0, h, 0, 0)),
             byh(1), byh(1), byh(2), byh(dv), fixed((dv, nb))]
    o_b, sb = _head_call(_decode_dn_kernel, "decode_dn", nh, dk, dv, nb, li, args, specs)

    nh, dk, dv = H_HG, DK_HG, DV_HG
    hk = nh * dk
    pt = pc.T
    args = (pt[0:hk].reshape(nh, dk, nb), pt[hk:2 * hk].reshape(nh, dk, nb),
            pt[2 * hk:3 * hk].reshape(nh, dv, nb), pt[3 * hk:4 * hk].reshape(nh, dv, nb),
            bl(lb, nh, dk), bl(ngc, dv), tr(s_hg))
    specs = [byh(dk), byh(dk), byh(dv), byh(dv), byh(dk), fixed((dv, nb))]
    o_c, sc = _head_call(_decode_hg_kernel, "decode_hgrn2", nh, dk, dv, nb, li, args, specs)

    return jnp.concatenate([o_a, o_b, o_c], axis=0).T.astype(BF16), sa, sb, sc


def _outproj_router_kernel(x_ref, oa_ref, ob_ref, oc_ref, w_ref, g_ref, rw_ref, rb_ref, ui_ref,
                           x1_ref, xa_ref, slot_ref, cnt_ref):
    tm, d = x_ref.shape
    na, nb_ = oa_ref.shape[1], ob_ref.shape[1]
    rs = min(tm, ROUTER_ROWS)
    blocks = [slice(r, r + rs) for r in range(0, tm, rs)]
    x1s = [x_ref[b, :] + _dot(oa_ref[b, :], w_ref[0:na, :]) + _dot(ob_ref[b, :], w_ref[na:na + nb_, :])
           + _dot(oc_ref[b, :], w_ref[na + nb_:, :]) for b in blocks]
    h2s = [_rms(x1, g_ref[...]).astype(BF16) for x1 in x1s]
    for b, x1, h2 in zip(blocks, x1s, h2s):
        x1_ref[b, :] = x1
        xa_ref[b, 0:d] = h2
    lane = lax.broadcasted_iota(jnp.int32, (rs, LANE), 1)
    neg = jnp.float32(-jnp.inf)
    big = jnp.int32(1 << 20)
    is_g = (lane >= N_EXPERTS) & (lane < N_EXPERTS + N_GROUPS)

    def route(logits):
        lg = jnp.where(is_g, logits, neg)
        mg = jnp.max(lg, axis=-1, keepdims=True)
        pg_top = 1.0 / jnp.sum(jnp.where(is_g, jnp.exp(lg - mg), 0.0), axis=-1, keepdims=True)
        g_idx = jnp.min(jnp.where(lg == mg, lane, big), axis=-1, keepdims=True) - N_EXPERTS
        in_grp = (lane >= g_idx * EXP_PER_GROUP) & (lane < (g_idx + 1) * EXP_PER_GROUP)
        le = jnp.where(in_grp, logits, neg)
        me = jnp.max(le, axis=-1, keepdims=True)
        ex = jnp.where(in_grp, jnp.exp(le - me), 0.0)
        pe = ex / jnp.sum(ex, axis=-1, keepdims=True)
        pe = jnp.where(in_grp, pe, -1.0)
        v1 = jnp.max(pe, axis=-1, keepdims=True)
        i1 = jnp.min(jnp.where(pe == v1, lane, big), axis=-1, keepdims=True)
        pe2 = jnp.where(lane == i1, -1.0, pe)
        v2 = jnp.max(pe2, axis=-1, keepdims=True)
        i2 = jnp.min(jnp.where(pe2 == v2, lane, big), axis=-1, keepdims=True)
        tot = v1 + v2
        gate = pg_top * (jnp.where(lane == i1, v1 / tot, 0.0) + jnp.where(lane == i2, v2 / tot, 0.0))
        return gate, (lane == g_idx).astype(BF16)

    routed = [route(_dot(h2, rw_ref[...]) + rb_ref[...]) for h2 in h2s]
    both, cnt = None, None
    for b, (gate, ind) in zip(blocks, routed):
        hi, mid, lo = _split3(gate)
        xa_ref[b, d:d + LANE] = (hi.astype(F32) + pltpu.roll(mid.astype(F32), N_EXPERTS, 1)
                                 + pltpu.roll(lo.astype(F32), 2 * N_EXPERTS, 1)).astype(BF16)
        part = _dot_tn(ind, ui_ref[b, :])
        tot_b = _dot(jnp.ones((8, rs), BF16), ind)
        both = part if both is None else both + part
        cnt = tot_b if cnt is None else cnt + tot_b
    slot_ref[0] = jnp.where(both[0:8, tm:2 * tm] > 0.5, both[0:8, 0:tm], -1.0)
    cnt_ref[0] = cnt.astype(jnp.int32)


def _outproj_router(x, oa, ob, oc, w, g, rw, rb, tm):
    t, d = x.shape
    nt = t // tm
    idx = np.arange(tm)
    ui = jnp.asarray(np.concatenate([idx[:, None] < idx[None, :], np.eye(tm, dtype=bool)], axis=1), BF16)
    row = lambda i: (i, 0)
    fix = lambda i: (0, 0)
    x1, xa, slot, cnt = pl.pallas_call(
        _outproj_router_kernel,
        grid=(nt,),
        in_specs=[pl.BlockSpec((tm, d), row), pl.BlockSpec((tm, oa.shape[1]), row),
                  pl.BlockSpec((tm, ob.shape[1]), row), pl.BlockSpec((tm, oc.shape[1]), row),
                  pl.BlockSpec(w.shape, fix), pl.BlockSpec((1, d), fix),
                  pl.BlockSpec(rw.shape, fix), pl.BlockSpec((1, LANE), fix), pl.BlockSpec(ui.shape, fix)],
        out_specs=[pl.BlockSpec((tm, d), row), pl.BlockSpec((tm, d + LANE), row),
                   pl.BlockSpec((1, 8, tm), lambda i: (i, 0, 0)), pl.BlockSpec((1, 8, LANE), lambda i: (i, 0, 0))],
        out_shape=[jax.ShapeDtypeStruct((t, d), F32), jax.ShapeDtypeStruct((t, d + LANE), BF16),
                   jax.ShapeDtypeStruct((nt, 8, tm), F32), jax.ShapeDtypeStruct((nt, 8, LANE), jnp.int32)],
        compiler_params=_cparams(("parallel",)),
        name="outproj_router",
    )(x, oa, ob, oc, w, g, rw, rb, ui)
    return x1, xa, slot[:, 0:N_GROUPS, None, :], cnt[:, 0, 0:N_GROUPS]


def _moe_kernel(final, rb0, rbx, cnt_ref, x1_ref, xa_ref, slot_ref, ex_ref, w1_ref, w3_ref, w2_ref, fg_ref, y_ref):
    i, g = pl.program_id(0), pl.program_id(1)
    tm, d = x1_ref.shape
    ne, _, f = w1_ref.shape

    @pl.when(g == 0)
    def _():
        y_ref[...] = x1_ref[...]

    cnt = cnt_ref[i, g]
    slot = slot_ref[0, 0]

    def rows(row0, rb):
        rid = (lax.broadcasted_iota(jnp.int32, (rb, tm), 0) + row0).astype(F32)
        sel = (rid == slot).astype(BF16)
        xg = _dot(sel, xa_ref[...])
        xb = xg[:, 0:d].astype(BF16)
        gexp = _dot(xg[:, d:d + LANE].astype(BF16), ex_ref[0])
        up = [(_dot(xb, w1_ref[e]), _dot(xb, w3_ref[e])) for e in range(ne)]
        hid = [(_silu(a) * b * gexp[:, e * f:(e + 1) * f]).astype(BF16) for e, (a, b) in enumerate(up)]
        yg = _dot(jnp.concatenate(hid, axis=-1), w2_ref[...].reshape(ne * f, d))
        y_ref[...] += _dot_tn(sel, yg.astype(BF16))

    @pl.when(cnt > 0)
    def _():
        rows(0, rb0)

    def extra(j, carry):
        rows(rb0 + j * rbx, rbx)
        return carry

    lax.fori_loop(0, (jnp.maximum(cnt - rb0, 0) + rbx - 1) // rbx, extra, 0)

    if final:
        @pl.when(g == pl.num_programs(1) - 1)
        def _():
            y_ref[...] = _rms(y_ref[...], fg_ref[...])


def _gate_expand(f):
    r = np.arange(LANE)
    e = r % N_EXPERTS
    col_e = np.arange(EXP_PER_GROUP * f) // f
    m = [(r[:, None] < 3 * N_EXPERTS) & (e[:, None] == g * EXP_PER_GROUP + col_e[None, :]) for g in range(N_GROUPS)]
    return jnp.asarray(np.stack(m), BF16)


def _moe(x1, xa, slot, cnt, w1, w3, w2, li, expand, fg, final, tm):
    t, d = x1.shape
    f = w1.shape[2]
    gf = EXP_PER_GROUP * f
    rb0 = min(tm, MOE_ROWS_FIRST)
    rbx = min(tm, MOE_ROWS_EXTRA)
    row = lambda i, g, c: (i, 0)
    grp = lambda i, g, c: (g, 0, 0)
    wgrp = lambda i, g, c: (li * N_GROUPS + g, 0, 0)
    return pl.pallas_call(
        functools.partial(_moe_kernel, final, rb0, rbx),
        grid_spec=pltpu.PrefetchScalarGridSpec(
            num_scalar_prefetch=1,
            grid=(t // tm, N_GROUPS),
            in_specs=[pl.BlockSpec((tm, d), row), pl.BlockSpec((tm, d + LANE), row),
                      pl.BlockSpec((1, 1, 1, tm), lambda i, g, c: (i, g, 0, 0)),
                      pl.BlockSpec((1, LANE, gf), grp),
                      pl.BlockSpec((EXP_PER_GROUP, d, f), wgrp), pl.BlockSpec((EXP_PER_GROUP, d, f), wgrp),
                      pl.BlockSpec((EXP_PER_GROUP, f, d), wgrp),
                      pl.BlockSpec((1, d), lambda i, g, c: (0, 0))],
            out_specs=pl.BlockSpec((tm, d), row)),
        out_shape=jax.ShapeDtypeStruct((t, d), F32),
        compiler_params=_cparams(("parallel", "arbitrary")),
        name="moe",
    )(cnt, x1, xa, slot, expand, w1, w3, w2, fg)


def kernel(x_prompt, x_sample, state_gla, state_dn, state_conv, state_hgrn, norm1_g, w_in, gla_wa2, gla_ba, gla_norm_g, dn_conv_w, dn_a_log, dn_dt_bias, dn_norm_g, hg_lb_logits, hg_norm_g, w_out, norm2_g, router_g_w, router_g_b, router_e_w, router_e_b, exp_w1, exp_w3, exp_w2, final_norm_g):
    nbp, seq, d = x_prompt.shape
    nbs = x_sample.shape[0]
    depth = w_in.shape[0]
    assert x_sample.shape[1] == 1 and seq % CHUNK == 0
    tp = nbp * seq
    xp = x_prompt.reshape(tp, d)
    xs = x_sample.reshape(nbs, d)

    sm = jax.nn.softmax(hg_lb_logits.astype(F32), axis=0)
    lb_all = jnp.maximum(jnp.cumsum(sm, axis=0) - sm[0:1], 0.0)

    tm_p = min(1024, tp)
    tm_moe = min(MOE_TILE, tp)
    expand = _gate_expand(exp_w1.shape[3])
    row2 = lambda v: v.reshape(1, -1)
    w_in_b = _perm_w_in(w_in)
    stack_e = lambda w: w.astype(BF16).reshape((w.shape[0] * w.shape[1],) + w.shape[2:])
    w1b, w3b, w2b = stack_e(exp_w1), stack_e(exp_w3), stack_e(exp_w2)
    gla_p, dn_p, conv_p, hg_p, gla_s, dn_s, conv_s, hg_s = ([] for _ in range(8))
    for li in range(depth):
        w_out_b = w_out[li].astype(BF16)
        rw = jnp.pad(jnp.concatenate([router_e_w[li], router_g_w[li]], axis=1),
                     ((0, 0), (0, LANE - N_EXPERTS - N_GROUPS))).astype(BF16)
        rb = jnp.pad(jnp.concatenate([router_e_b[li], router_g_b[li]]), (0, LANE - N_EXPERTS - N_GROUPS)).reshape(1, LANE)
        final = li == depth - 1
        g1, g2, fg = row2(norm1_g[li]), row2(norm2_g[li]), row2(final_norm_g)

        pa, pb, pc = _inproj(xp, g1, w_in_b, li, tm_p)
        oa, sa = _gla_chunk("gla", pa, nbp, seq, gla_wa2[li], row2(gla_ba[li]), row2(gla_norm_g[li]))
        ob, sb = _dn_chunk(pb, nbp, seq, dn_conv_w[li], dn_a_log[li], dn_dt_bias[li], row2(dn_norm_g[li]))
        oc, sc = _gla_chunk("hgrn2", pc, nbp, seq, row2(lb_all[li]), row2(lb_all[li]), row2(hg_norm_g[li]))
        gla_p.append(sa)
        dn_p.append(sb)
        conv_p.append(pb.reshape(nbp, seq, DN_W)[:, seq - (CONV_W - 1):, 0:CONV_DIM])
        hg_p.append(sc)
        x1, xa, slot, cnt = _outproj_router(xp, oa, ob, oc, w_out_b, g2, rw, rb, tm_moe)
        xp = _moe(x1, xa, slot, cnt, w1b, w3b, w2b, li, expand, fg, final, tm_moe)

        qa, qb, qc = _inproj(xs, g1, w_in_b, li, nbs)
        o_s, sa, sb, sc = _decode(qa, qb, qc, li, state_gla, state_dn, state_conv[li], state_hgrn,
                                  gla_wa2[li], gla_ba[li], dn_conv_w[li], dn_a_log[li], dn_dt_bias[li],
                                  lb_all[li], gla_norm_g[li], dn_norm_g[li], hg_norm_g[li])
        gla_s.append(sa)
        dn_s.append(sb)
        conv_s.append(jnp.concatenate([state_conv[li][:, 1:], qb[:, None, 0:CONV_DIM]], axis=1))
        hg_s.append(sc)
        na, nb_ = H_GLA * DV_GLA, H_DN * DV_DN
        x1, xa, slot, cnt = _outproj_router(xs, o_s[:, 0:na], o_s[:, na:na + nb_], o_s[:, na + nb_:], w_out_b, g2, rw, rb, nbs)
        xs = _moe(x1, xa, slot, cnt, w1b, w3b, w2b, li, expand, fg, final, nbs)

    st = lambda xs_, ref: jnp.stack(xs_).astype(ref.dtype)
    return (xp.reshape(nbp, seq, d), xs.reshape(nbs, 1, d),
            st(gla_p, state_gla), st(dn_p, state_dn), st(conv_p, state_conv), st(hg_p, state_hgrn),
            st(gla_s, state_gla), st(dn_s, state_dn), st(conv_s, state_conv), st(hg_s, state_hgrn))
```

```python
import functools
import math

import numpy as np
import jax
import jax.numpy as jnp
from jax import lax
from jax.experimental import pallas as pl
from jax.experimental.pallas import tpu as pltpu

F32 = jnp.float32
BF16 = jnp.bfloat16
EPS = 1e-6

H_GLA, DK_GLA, DV_GLA, GLA_RANK, GLA_TAU = 6, 32, 64, 16, 16.0
H_DN, DK_DN, DV_DN, CONV_W = 6, 64, 64, 4
CONV_DIM = H_DN * (2 * DK_DN + DV_DN)
H_HG, DK_HG, DV_HG = 4, 64, 64
N_GROUPS, EXP_PER_GROUP, TOP_K = 4, 8, 2
N_EXPERTS = N_GROUPS * EXP_PER_GROUP

LANE = 128
CHUNK = 64
GLA_CHUNKS_PER_STEP = 8
GLA_WAVE = 8
DN_CHUNKS_PER_STEP = 8
DN_WAVE = 4
MOE_TILE = 1024
MOE_ROWS_FIRST = 288
MOE_ROWS_EXTRA = 128
ROUTER_ROWS = 256
VMEM_LIMIT = 56 * 1024 * 1024

GLA_W = 1280
DN_W = 1664
HG_W = 1024
DN_AB = 1536


def _cparams(sem):
    return pltpu.CompilerParams(dimension_semantics=sem, vmem_limit_bytes=VMEM_LIMIT)


def _dot(a, b):
    return jnp.dot(a, b, preferred_element_type=F32)


def _dot_nt(a, b):
    return lax.dot_general(a, b, (((1,), (1,)), ((), ())), preferred_element_type=F32)


def _dot_tn(a, b):
    return lax.dot_general(a, b, (((0,), (0,)), ((), ())), preferred_element_type=F32)


def _split3(x):
    hi = x.astype(BF16)
    r = x - hi.astype(F32)
    mid = r.astype(BF16)
    lo = (r - mid.astype(F32)).astype(BF16)
    return hi, mid, lo


def _split2(x):
    hi = x.astype(BF16)
    lo = (x - hi.astype(F32)).astype(BF16)
    return hi, lo


def _dot_sel_l(m, x):
    hi, mid, lo = _split3(x)
    return _dot(m, hi) + _dot(m, mid) + _dot(m, lo)


def _dot_sel_r(x, m):
    hi, mid, lo = _split3(x)
    return _dot(hi, m) + _dot(mid, m) + _dot(lo, m)


def _dot_hp(a, b, fn=_dot):
    ah, al = _split2(a)
    bh, bl = _split2(b)
    return fn(ah, bh) + fn(ah, bl) + fn(al, bh)


def _rms(x, g):
    return x * lax.rsqrt(jnp.mean(x * x, axis=-1, keepdims=True) + EPS) * g


def _sigmoid(x):
    return 1.0 / (1.0 + jnp.exp(-x))


def _silu(x):
    return x * _sigmoid(x)


def _log_sigmoid(x):
    return jnp.minimum(x, 0.0) - jnp.log1p(jnp.exp(-jnp.abs(x)))


def _softplus(x):
    return jnp.maximum(x, 0.0) + jnp.log1p(jnp.exp(-jnp.abs(x)))


def _inproj_kernel(x_ref, g_ref, w_ref, oa_ref, ob_ref, oc_ref):
    h = _rms(x_ref[...], g_ref[...]).astype(BF16)
    oa_ref[...] = _dot(h, w_ref[0, :, 0:GLA_W])
    ob_ref[...] = _dot(h, w_ref[0, :, GLA_W:GLA_W + DN_W])
    oc_ref[...] = _dot(h, w_ref[0, :, GLA_W + DN_W:GLA_W + DN_W + HG_W])


def _inproj(x, g, w, li, tm):
    t, d = x.shape
    n = w.shape[2]
    row = lambda i: (i, 0)
    fix = lambda i: (0, 0)
    return pl.pallas_call(
        _inproj_kernel,
        grid=(t // tm,),
        in_specs=[pl.BlockSpec((tm, d), row), pl.BlockSpec((1, d), fix),
                  pl.BlockSpec((1, d, n), lambda i: (li, 0, 0), pipeline_mode=pl.Buffered(1))],
        out_specs=[pl.BlockSpec((tm, GLA_W), row), pl.BlockSpec((tm, DN_W), row), pl.BlockSpec((tm, HG_W), row)],
        out_shape=[jax.ShapeDtypeStruct((t, GLA_W), F32), jax.ShapeDtypeStruct((t, DN_W), F32),
                   jax.ShapeDtypeStruct((t, HG_W), F32)],
        compiler_params=_cparams(("parallel",)),
        name="inproj",
    )(x, g, w)


def _w_in_segments():
    sizes = (H_GLA * DK_GLA, H_GLA * DK_GLA, H_GLA * DV_GLA, GLA_RANK, H_GLA * DV_GLA,
             CONV_DIM, H_DN, H_DN, H_DN * DV_DN,
             H_HG * DK_HG, H_HG * DK_HG, H_HG * DV_HG, H_HG * DV_HG)
    offs = np.concatenate([[0], np.cumsum(sizes)]).tolist()
    g_q, g_k, g_v, g_a, g_g, d_qkv, d_a, d_b, d_g, h_q, h_f, h_i, h_g = [(offs[i], sizes[i]) for i in range(len(sizes))]
    return [g_q, g_k, g_v, g_g, g_a, (None, GLA_W - 1168),
            d_qkv, d_g, d_a, d_b, (None, DN_W - 1548),
            h_q, h_f, h_i, h_g]


def _perm_w_kernel(w_ref, o_ref):
    w = w_ref[0]
    cols = [jnp.zeros((w.shape[0], n), F32) if src is None else w[:, src:src + n] for src, n in _w_in_segments()]
    o_ref[0] = jnp.concatenate(cols, axis=1).astype(BF16)


def _perm_w_in(w):
    depth, d, n = w.shape
    n_out = GLA_W + DN_W + HG_W
    tr = min(256, d)
    return pl.pallas_call(
        _perm_w_kernel,
        grid=(depth, d // tr),
        in_specs=[pl.BlockSpec((1, tr, n), lambda l, r: (l, r, 0))],
        out_specs=pl.BlockSpec((1, tr, n_out), lambda l, r: (l, r, 0)),
        out_shape=jax.ShapeDtypeStruct((depth, d, n_out), BF16),
        compiler_params=_cparams(("parallel", "parallel")),
        name="perm_w_in",
    )(w)


def _decay_consts(c):
    n = int(math.log2(c))
    idx = np.arange(c)
    lm = (idx[None, :] <= idx[:, None]).astype(np.float32)
    mats, masks = [], []
    for l in range(1, n + 1):
        hs = c >> l
        bs = 2 * hs
        blk = idx // bs
        ref = blk * bs + hs - 1
        mats.append(lm - lm[ref])
        lower = (idx % bs) >= hs
        same = blk[:, None] == blk[None, :]
        masks.append((same & lower[:, None] & (~lower)[None, :]).astype(np.float32))
    mats.append(lm)
    mats.append(1.0 - lm)
    masks.append(np.eye(c, dtype=np.float32))
    masks = np.stack(masks)
    return (jnp.asarray(np.concatenate(mats, 0), BF16), jnp.asarray(np.concatenate([masks, masks], axis=1), F32), n)


def _seg_ind(nh, dv):
    h = np.arange(nh * dv) // dv
    return jnp.asarray(h[:, None] == h[None, :], BF16)


def _seg_sum(x, ind):
    return _dot(x.astype(BF16), ind)


def _pair_geometry(nh, dk):
    geo = []
    for p in range(nh // 2):
        start = (2 * p * dk) // LANE * LANE
        width = min(LANE, nh * dk - start)
        geo.append((start, width, (2 * p * dk - start, (2 * p + 1) * dk - start)))
    return geo


def _gla_chunk_kernel(mode, nh, dk, dv, nlev, nch, slab_ref, mats_ref, masks_ref, ind_ref, p1_ref, p2_ref, ng_ref,
                      o_ref, st_ref, s_scr):
    c = CHUNK
    ci = pl.program_id(1)

    @pl.when(ci == 0)
    def _():
        s_scr[...] = jnp.zeros_like(s_scr)

    hk, hv = nh * dk, nh * dv
    if mode == "gla":
        q = slab_ref[:, 0:hk] * (dk ** -0.5)
        k = slab_ref[:, hk:2 * hk]
        v = slab_ref[:, 2 * hk:2 * hk + hv]
        gate = slab_ref[:, 2 * hk + hv:2 * hk + 2 * hv]
        a_lr = slab_ref[:, 2 * hk + 2 * hv:2 * hk + 2 * hv + GLA_RANK]
        z = _dot_hp(a_lr, p1_ref[...]) + p2_ref[...]
        la = _log_sigmoid(z) * (1.0 / GLA_TAU)
    else:
        q = _silu(slab_ref[:, 0:hk])
        x = slab_ref[:, hk:2 * hk]
        v = slab_ref[:, 2 * hk:2 * hk + hv]
        gate = slab_ref[:, 2 * hk + hv:2 * hk + 2 * hv]
        lb = p1_ref[...]
        la = _log_sigmoid(x) + jnp.log1p(lb * jnp.exp(-x))
        k = (1.0 - lb) * _sigmoid(-x)

    bf = lambda t: t.astype(BF16)
    geo = _pair_geometry(nh, dk)
    npair = len(geo)
    chunks = range(nch)
    rows = [slice(i * c, (i + 1) * c) for i in chunks]
    vb = bf(v)

    qs, ks, q_in, k_rem, e_last = {}, {}, {}, {}, {}

    def prepare(wave):
        for i in wave:
            dall = _dot_sel_l(mats_ref[...], la[rows[i]])
            bcum = dall[nlev * c:(nlev + 1) * c]
            wl = [jnp.exp(-jnp.abs(dall[l * c:(l + 1) * c])) for l in range(nlev)]
            qc, kc = q[rows[i]], k[rows[i]]
            qs[i] = [bf(qc * w) for w in wl] + [bf(qc)]
            ks[i] = [bf(kc * w) for w in wl] + [bf(kc)]
            q_in[i] = bf(qc * jnp.exp(bcum))
            k_rem[i] = bf(kc * jnp.exp(dall[(nlev + 1) * c:(nlev + 2) * c]))
            e_last[i] = jnp.exp(bcum[c - 1:c])
            yield

    lane_v = lax.broadcasted_iota(jnp.int32, (c, LANE), 1)
    first = lane_v < dv
    pair_consts = []
    for p in range(npair):
        start, width, offs = geo[p]
        lane = lax.broadcasted_iota(jnp.int32, (c, width), 1)
        ri = lax.broadcasted_iota(jnp.int32, (2 * dv, width), 0)
        li = lax.broadcasted_iota(jnp.int32, (2 * dv, width), 1)
        smask = ((ri < dv) & (li >= offs[0]) & (li < offs[0] + dk)) | ((ri >= dv) & (li >= offs[1]) & (li < offs[1] + dk))
        pair_consts.append((slice(start, start + width), slice(p * LANE, (p + 1) * LANE),
                            [(lane >= o) & (lane < o + dk) for o in offs], jnp.zeros((c, width), BF16), smask))
    carried = [s_scr[p, :, 0:geo[p][1]] for p in range(npair)]
    pieces = [[] for _ in range(npair)]

    def attend(wave):
        for p in range(npair):
            blk, vcol, hmask, zero, smask = pair_consts[p]
            atts = {}
            for i in wave:
                att = None
                for l in range(nlev + 1):
                    qb = qs[i][l][:, blk]
                    lhs = jnp.concatenate([jnp.where(hmask[0], qb, zero), jnp.where(hmask[1], qb, zero)], axis=0)
                    term = _dot_nt(lhs, ks[i][l][:, blk]) * masks_ref[l]
                    att = term if att is None else att + term
                atts[i] = bf(att)
                yield
            intra, upd, states = {}, {}, {}
            for i in wave:
                r2 = _dot(atts[i], vb[rows[i], vcol])
                intra[i] = jnp.where(first, r2[0:c], r2[c:2 * c])
                upd[i] = jnp.where(smask, _dot_tn(vb[rows[i], vcol], k_rem[i][:, blk]), 0.0)
            yield
            s = carried[p]
            for i in wave:
                states[i] = bf(s)
                s = s * e_last[i][:, blk] + upd[i]
            carried[p] = s
            for i in wave:
                pieces[p].append(intra[i] + _dot_nt(q_in[i][:, blk], states[i]))
            yield

    waves = [list(range(w0, min(w0 + GLA_WAVE, nch))) for w0 in range(0, nch, GLA_WAVE)]
    pending = iter(())
    for wave in waves:
        for _ in prepare(wave):
            next(pending, None)
            next(pending, None)
        for _ in pending:
            pass
        pending = attend(wave)
    for _ in pending:
        pass
    for p in range(npair):
        s_scr[p, :, 0:geo[p][1]] = carried[p]
    o = jnp.concatenate([jnp.concatenate(pc, axis=0) for pc in pieces], axis=-1)

    if mode == "gla":
        ms = _seg_sum(o * o, ind_ref[...]) * (1.0 / dv)
        o = o * lax.rsqrt(ms + EPS) * ng_ref[...] * _silu(gate)
    else:
        o = o * _sigmoid(gate)
        ms = _seg_sum(o * o, ind_ref[...]) * (1.0 / dv)
        o = o * lax.rsqrt(ms + EPS) * ng_ref[...]
    o_ref[...] = o.astype(o_ref.dtype)

    @pl.when(ci == pl.num_programs(1) - 1)
    def _():
        st_ref[0] = s_scr[...]


def _gla_chunk(mode, slab, nb, seq, p1, p2, ng):
    nh, dk, dv = (H_GLA, DK_GLA, DV_GLA) if mode == "gla" else (H_HG, DK_HG, DV_HG)
    w = slab.shape[1]
    nch = math.gcd(seq // CHUNK, GLA_CHUNKS_PER_STEP)
    sc = nch * CHUNK
    ns = seq // sc
    mats, masks, nlev = _decay_consts(CHUNK)
    ind = _seg_ind(nh, dv)
    ngt = jnp.tile(ng, (1, nh))
    geo = _pair_geometry(nh, dk)
    fix2 = lambda b, c: (0, 0)
    kern = functools.partial(_gla_chunk_kernel, mode, nh, dk, dv, nlev, nch)
    o, st = pl.pallas_call(
        kern,
        grid=(nb, ns),
        in_specs=[pl.BlockSpec((sc, w), lambda b, c: (b * ns + c, 0)),
                  pl.BlockSpec(mats.shape, fix2),
                  pl.BlockSpec(masks.shape, lambda b, c: (0, 0, 0)),
                  pl.BlockSpec(ind.shape, fix2),
                  pl.BlockSpec(p1.shape, fix2), pl.BlockSpec(p2.shape, fix2), pl.BlockSpec(ngt.shape, fix2)],
        out_specs=[pl.BlockSpec((sc, nh * dv), lambda b, c: (b * ns + c, 0)),
                   pl.BlockSpec((1, len(geo), 2 * dv, LANE), lambda b, c: (b, 0, 0, 0))],
        out_shape=[jax.ShapeDtypeStruct((nb * seq, nh * dv), BF16),
                   jax.ShapeDtypeStruct((nb, len(geo), 2 * dv, LANE), F32)],
        scratch_shapes=[pltpu.VMEM((len(geo), 2 * dv, LANE), F32)],
        compiler_params=_cparams(("parallel", "arbitrary")),
        name="chunk_" + mode,
    )(slab, mats, masks, ind, p1, p2, ngt)
    heads = []
    for p, (_, _, offs) in enumerate(geo):
        for j in range(2):
            heads.append(jnp.swapaxes(st[:, p, j * dv:(j + 1) * dv, offs[j]:offs[j] + dk], -1, -2))
    return o, jnp.stack(heads, axis=1)


def _dn_chunk_kernel(nch, slab_ref, abr_ref, lm_ref, umb_ref, ind_ref, cw_ref, pcol_ref, prow_ref, ng_ref,
                     o_ref, st_ref, s_scr, ext_scr):
    c = CHUNK
    sc = nch * c
    nh, dk, dv = H_DN, DK_DN, DV_DN
    npair = nh // 2
    ci = pl.program_id(1)

    @pl.when(ci == 0)
    def _():
        s_scr[...] = jnp.zeros_like(s_scr)
        ext_scr[0:8, :] = jnp.zeros((8, CONV_DIM), F32)

    ext_scr[8:8 + sc, :] = slab_ref[:, 0:CONV_DIM]
    conv = ext_scr[5:5 + sc, :] * cw_ref[0:1, :]
    for i in range(1, CONV_W):
        conv = conv + ext_scr[5 + i:5 + i + sc, :] * cw_ref[i:i + 1, :]
    ext_scr[0:8, :] = ext_scr[sc:sc + 8, :]
    qkv = _silu(conv)
    gate = slab_ref[:, CONV_DIM:DN_AB]
    hk = nh * dk
    q = qkv[:, 0:hk]
    k = qkv[:, hk:2 * hk]
    v = qkv[:, 2 * hk:2 * hk + nh * dv]
    q = q * lax.rsqrt(_seg_sum(q * q, ind_ref[...]) + EPS) * (dk ** -0.5)
    k = k * lax.rsqrt(_seg_sum(k * k, ind_ref[...]) + EPS)

    ab_c = slab_ref[:, DN_AB:DN_AB + LANE]
    g_c = -jnp.exp(pcol_ref[0:1, :]) * _softplus(ab_c + pcol_ref[1:2, :])
    beta_c = _sigmoid(ab_c)
    g_r = -jnp.exp(prow_ref[0]) * _softplus(abr_ref[0, 0] + prow_ref[1])
    gcum_r = _dot_sel_r(g_r, umb_ref[...])

    bf = lambda t: t.astype(BF16)
    rows = [slice(i * c, (i + 1) * c) for i in range(nch)]
    ri = lax.broadcasted_iota(jnp.int32, (2 * c, 2 * c), 0)
    cj = lax.broadcasted_iota(jnp.int32, (2 * c, 2 * c), 1)
    same = (ri < c) == (cj < c)
    tri = same & (ri >= cj)
    strict = same & (ri > cj)
    eye = (ri == cj).astype(F32)
    lane = lax.broadcasted_iota(jnp.int32, (c, LANE), 1)
    first = lane < dk
    zero = jnp.zeros((c, LANE), BF16)
    gcum_cs = [_dot_sel_l(lm_ref[...], g_c[rows[i]]) for i in range(nch)]
    egc_cs = [jnp.exp(gcc) for gcc in gcum_cs]
    ekd_cs = [jnp.exp(gcc[c - 1:c] - gcc) for gcc in gcum_cs]
    egl_cs = [jnp.exp(gcc[c - 1:c]) for gcc in gcum_cs]

    def lanes2(col0, col1):
        return jnp.where(first, jnp.broadcast_to(col0, (c, LANE)), jnp.broadcast_to(col1, (c, LANE)))

    def stack2(col0, col1):
        return jnp.concatenate([jnp.broadcast_to(col0, (c, LANE)), jnp.broadcast_to(col1, (c, LANE))], axis=0)

    def rows2(x):
        return jnp.concatenate([jnp.where(first, x, zero), jnp.where(first, zero, x)], axis=0)

    pairs = range(npair)

    def prepare(i, p):
        h0, h1 = 2 * p, 2 * p + 1
        blk = slice(p * LANE, (p + 1) * LANE)
        gcc = gcum_cs[i]
        gc0, gc1 = gcc[:, h0:h0 + 1], gcc[:, h1:h1 + 1]
        b0, b1 = beta_c[rows[i], nh + h0:nh + h0 + 1], beta_c[rows[i], nh + h1:nh + h1 + 1]
        gcr = gcum_r[p:p + 1, i * LANE:(i + 1) * LANE]
        gam = jnp.where(tri, jnp.exp(jnp.where(tri, stack2(gc0, gc1) - gcr, 0.0)), 0.0)
        kc, qc, vc = k[rows[i], blk], q[rows[i], blk], v[rows[i], blk]
        beta_l = lanes2(b0, b1)
        e_c, e_d, e_l = egc_cs[i], ekd_cs[i], egl_cs[i]
        egc_l = lanes2(e_c[:, h0:h0 + 1], e_c[:, h1:h1 + 1])
        kb = kc * beta_l
        return dict(
            gam=gam, k2=rows2(bf(kc)), kb2=rows2(bf(kb)), q2=rows2(bf(qc)), vb2=rows2(bf(vc * beta_l)),
            ke2=rows2(bf(kb * egc_l)), qe=bf(qc * egc_l),
            kd2=rows2(bf(kc * lanes2(e_d[:, h0:h0 + 1], e_d[:, h1:h1 + 1]))),
            eglast=stack2(e_l[:, h0:h0 + 1], e_l[:, h1:h1 + 1]))

    def independent(chunk_ids, pre):
        units = [(i, p) for i in chunk_ids for p in pairs]
        for u in units:
            pre[u] = prepare(*u)
        yield
        ms = {u: jnp.where(strict, _dot_nt(pre[u]["kb2"], pre[u]["k2"]) * pre[u]["gam"], 0.0) for u in units}
        tinv = {u: eye - ms[u] for u in units}
        pw = {u: bf(ms[u]) for u in units}
        yield
        for _ in range(int(math.log2(c)) - 1):
            pw = {u: bf(_dot(pw[u], pw[u])) for u in units}
            yield
            tinv = {u: tinv[u] + _dot(bf(tinv[u]), pw[u]) for u in units}
            yield
        for u in units:
            tb = bf(tinv[u])
            pre[u]["uu"] = _dot(tb, pre[u]["vb2"])
            pre[u]["ww"] = bf(_dot(tb, pre[u]["ke2"]))
            pre[u]["att"] = bf(_dot_nt(pre[u]["q2"], pre[u]["k2"]) * pre[u]["gam"])
        yield

    states = [s_scr[p] for p in pairs]
    pieces = [[] for _ in pairs]

    def recur(chunk_ids, pre):
        for i in chunk_ids:
            sbs = [bf(states[p]) for p in pairs]
            vns = [bf(pre[(i, p)]["uu"] - _dot(pre[(i, p)]["ww"], sbs[p])) for p in pairs]
            yield
            for p in pairs:
                u = pre[(i, p)]
                o2 = _dot(u["att"], vns[p])
                pieces[p].append(_dot(u["qe"], sbs[p]) + o2[0:c] + o2[c:2 * c])
                states[p] = states[p] * u["eglast"] + _dot_tn(u["kd2"], vns[p])
            yield

    waves = [list(range(w, min(w + DN_WAVE, nch))) for w in range(0, nch, DN_WAVE)]
    pre, pending = {}, iter(())
    for wave in waves:
        for _ in independent(wave, pre):
            next(pending, None)
        for _ in pending:
            pass
        pending = recur(wave, pre)
    for _ in pending:
        pass
    for p in pairs:
        s_scr[p] = states[p]
    o = jnp.concatenate([jnp.concatenate(pc, axis=0) for pc in pieces], axis=-1)
    ms_o = _seg_sum(o * o, ind_ref[...]) * (1.0 / dv)
    o_ref[...] = (o * lax.rsqrt(ms_o + EPS) * ng_ref[...] * _silu(gate)).astype(o_ref.dtype)

    @pl.when(ci == pl.num_programs(1) - 1)
    def _():
        st_ref[0] = s_scr[...]


def _dn_chunk(slab, nb, seq, conv_w, a_log, dt_bias, ng):
    c = CHUNK
    nch = math.gcd(seq // c, DN_CHUNKS_PER_STEP)
    sc = nch * c
    ns = seq // sc
    npair = H_DN // 2
    nrow = 16
    idx = np.arange(c)
    lm = jnp.asarray((idx[None, :] <= idx[:, None]), BF16)
    um = (idx[:, None] <= idx[None, :]).astype(np.float32)
    umb = jnp.asarray(np.kron(np.eye(2 * nch, dtype=np.float32), um), BF16)
    ind = _seg_ind(H_DN, DK_DN)
    cw = jnp.pad(conv_w, ((0, 8 - CONV_W), (0, 0)))
    pad = lambda p: jnp.pad(p, (0, LANE - H_DN))
    pcol = jnp.zeros((8, LANE), F32).at[0].set(pad(a_log)).at[1].set(pad(dt_bias))
    a_cols = slab[:, DN_AB:DN_AB + H_DN].reshape(nb, ns, nch, c, npair, 2)
    a_rows = jnp.transpose(a_cols, (0, 1, 4, 2, 5, 3)).reshape(nb, ns, npair, nch * 2 * c)
    a_rows = jnp.pad(a_rows, ((0, 0), (0, 0), (0, nrow - npair), (0, 0)))
    rowp = lambda p: jnp.pad(jnp.broadcast_to(p.reshape(npair, 1, 2, 1), (npair, nch, 2, c)).reshape(npair, nch * 2 * c),
                             ((0, nrow - npair), (0, 0)))
    prow = jnp.stack([rowp(a_log), rowp(dt_bias)])
    ngt = jnp.tile(ng, (1, H_DN))
    fix2 = lambda b, c_: (0, 0)
    o, st = pl.pallas_call(
        functools.partial(_dn_chunk_kernel, nch),
        grid=(nb, ns),
        in_specs=[pl.BlockSpec((sc, DN_W), lambda b, c_: (b * ns + c_, 0)),
                  pl.BlockSpec((1, 1, nrow, nch * 2 * c), lambda b, c_: (b, c_, 0, 0)),
                  pl.BlockSpec((c, c), fix2), pl.BlockSpec(umb.shape, fix2), pl.BlockSpec(ind.shape, fix2),
                  pl.BlockSpec((8, CONV_DIM), fix2), pl.BlockSpec((8, LANE), fix2),
                  pl.BlockSpec(prow.shape, lambda b, c_: (0, 0, 0)), pl.BlockSpec(ngt.shape, fix2)],
        out_specs=[pl.BlockSpec((sc, H_DN * DV_DN), lambda b, c_: (b * ns + c_, 0)),
                   pl.BlockSpec((1, npair, 2 * DK_DN, 2 * DV_DN), lambda b, c_: (b, 0, 0, 0))],
        out_shape=[jax.ShapeDtypeStruct((nb * seq, H_DN * DV_DN), BF16),
                   jax.ShapeDtypeStruct((nb, npair, 2 * DK_DN, 2 * DV_DN), F32)],
        scratch_shapes=[pltpu.VMEM((npair, 2 * DK_DN, 2 * DV_DN), F32), pltpu.VMEM((sc + 8, CONV_DIM), F32)],
        compiler_params=_cparams(("parallel", "arbitrary")),
        name="chunk_dn",
    )(slab, a_rows, lm, umb, ind, cw, pcol, prow, ngt)
    heads = [st[:, p, j * DK_DN:(j + 1) * DK_DN, j * DV_DN:(j + 1) * DV_DN] for p in range(npair) for j in range(2)]
    return o, jnp.stack(heads, axis=1)


def _state_in(s_ref, dk):
    return s_ref[0]


def _state_out(so_ref, new):
    for d, s_new in enumerate(new):
        so_ref[0, d] = s_new


def _decode_gla_kernel(q_ref, k_ref, v_ref, g_ref, alr_ref, wa2t_ref, ba_ref, ng_ref, s_ref, o_ref, so_ref):
    dk = q_ref.shape[1]
    z = _dot_hp(wa2t_ref[0], alr_ref[...]) + ba_ref[0]
    dec = jnp.exp(_log_sigmoid(z) * (1.0 / GLA_TAU))
    q = q_ref[0] * (dk ** -0.5)
    k = k_ref[0]
    v = v_ref[0]
    st = _state_in(s_ref, dk)
    acc = jnp.zeros_like(v)
    new = []
    for d in range(dk):
        s_new = st[d] * dec[d:d + 1, :] + k[d:d + 1, :] * v
        new.append(s_new)
        acc = acc + q[d:d + 1, :] * s_new
    _state_out(so_ref, new)
    ms = jnp.mean(acc * acc, axis=0, keepdims=True)
    o_ref[0] = acc * lax.rsqrt(ms + EPS) * ng_ref[...] * _silu(g_ref[0])


def _decode_dn_kernel(x_ref, cb_ref, cw_ref, a_ref, b_ref, p_ref, g_ref, ng_ref, s_ref, o_ref, so_ref):
    dk = x_ref.shape[2]
    conv = x_ref[:, 0] * cw_ref[CONV_W - 1, :, 0]
    for i in range(CONV_W - 1):
        conv = conv + cb_ref[i, :, 0] * cw_ref[i, :, 0]
    qkv = _silu(conv)
    q, k, v = qkv[0], qkv[1], qkv[2]
    q = q * lax.rsqrt(jnp.sum(q * q, axis=0, keepdims=True) + EPS) * (dk ** -0.5)
    k = k * lax.rsqrt(jnp.sum(k * k, axis=0, keepdims=True) + EPS)
    eg = jnp.exp(-jnp.exp(p_ref[0, 0:1, :]) * _softplus(a_ref[0] + p_ref[0, 1:2, :]))
    beta = _sigmoid(b_ref[0])
    st = _state_in(s_ref, dk)
    ks = jnp.zeros_like(v)
    for d in range(dk):
        ks = ks + k[d:d + 1, :] * st[d]
    v_new = beta * (v - eg * ks)
    acc = jnp.zeros_like(v)
    new = []
    for d in range(dk):
        s_new = st[d] * eg + k[d:d + 1, :] * v_new
        new.append(s_new)
        acc = acc + q[d:d + 1, :] * s_new
    _state_out(so_ref, new)
    ms = jnp.mean(acc * acc, axis=0, keepdims=True)
    o_ref[0] = acc * lax.rsqrt(ms + EPS) * ng_ref[...] * _silu(g_ref[0])


def _decode_hg_kernel(q_ref, f_ref, v_ref, g_ref, lb_ref, ng_ref, s_ref, o_ref, so_ref):
    dk = q_ref.shape[1]
    x = f_ref[0]
    lb = lb_ref[0]
    f = jnp.exp(_log_sigmoid(x) + jnp.log1p(lb * jnp.exp(-x)))
    k = (1.0 - lb) * _sigmoid(-x)
    q = _silu(q_ref[0])
    v = v_ref[0]
    st = _state_in(s_ref, dk)
    acc = jnp.zeros_like(v)
    new = []
    for d in range(dk):
        s_new = st[d] * f[d:d + 1, :] + k[d:d + 1, :] * v
        new.append(s_new)
        acc = acc + q[d:d + 1, :] * s_new
    _state_out(so_ref, new)
    acc = acc * _sigmoid(g_ref[0])
    ms = jnp.mean(acc * acc, axis=0, keepdims=True)
    o_ref[0] = acc * lax.rsqrt(ms + EPS) * ng_ref[...]


def _head_call(kern, name, nh, dk, dv, nb, li, args, specs):
    o, s = pl.pallas_call(
        kern,
        grid=(nh,),
        in_specs=specs + [pl.BlockSpec((1, dk, dv, nb), lambda h: (li * nh + h, 0, 0, 0))],
        out_specs=[pl.BlockSpec((1, dv, nb), lambda h: (h, 0, 0)),
                   pl.BlockSpec((1, dk, dv, nb), lambda h: (h, 0, 0, 0))],
        out_shape=[jax.ShapeDtypeStruct((nh, dv, nb), F32), jax.ShapeDtypeStruct((nh, dk, dv, nb), F32)],
        compiler_params=_cparams(("parallel",)),
        name=name,
    )(*args)
    return o.reshape(nh * dv, nb), jnp.transpose(s, (3, 0, 1, 2))


def _decode(pa, pb, pc, li, s_gla, s_dn, s_conv, s_hg, wa2, ba, conv_w, a_log, dt_bias, lb, nga, ngb, ngc):
    nb = pa.shape[0]
    bl = lambda p, *shape: jnp.broadcast_to(p.reshape(shape + (1,)), shape + (nb,))
    tr = lambda s: jnp.transpose(s, (0, 2, 3, 4, 1)).reshape((s.shape[0] * s.shape[2],) + s.shape[3:] + (nb,))
    byh = lambda n: pl.BlockSpec((1, n, nb), lambda h: (h, 0, 0))
    fixed = lambda shape: pl.BlockSpec(shape, lambda h: (0,) * len(shape))

    nh, dk, dv = H_GLA, DK_GLA, DV_GLA
    hk, hv = nh * dk, nh * dv
    pt = pa.T
    args = (pt[0:hk].reshape(nh, dk, nb), pt[hk:2 * hk].reshape(nh, dk, nb),
            pt[2 * hk:2 * hk + hv].reshape(nh, dv, nb), pt[2 * hk + hv:2 * hk + 2 * hv].reshape(nh, dv, nb),
            pt[2 * hk + 2 * hv:2 * hk + 2 * hv + GLA_RANK], wa2.T.reshape(nh, dk, GLA_RANK),
            bl(ba, nh, dk), bl(nga, dv), tr(s_gla))
    specs = [byh(dk), byh(dk), byh(dv), byh(dv), fixed((GLA_RANK, nb)),
             pl.BlockSpec((1, dk, GLA_RANK), lambda h: (h, 0, 0)), byh(dk), fixed((dv, nb))]
    o_a, sa = _head_call(_decode_gla_kernel, "decode_gla", nh, dk, dv, nb, li, args, specs)

    nh, dk, dv = H_DN, DK_DN, DV_DN
    pt = pb.T
    x = pt[0:CONV_DIM].reshape(3, nh, dk, nb)
    cb = jnp.transpose(s_conv, (1, 2, 0)).reshape(CONV_W - 1, 3, nh, dk, nb)
    cw = bl(conv_w, CONV_W, 3, nh, dk)
    prm = jnp.stack([bl(a_log, nh), bl(dt_bias, nh)], axis=1)
    args = (x, cb, cw, pt[DN_AB:DN_AB + nh].reshape(nh, 1, nb), pt[DN_AB + nh:DN_AB + 2 * nh].reshape(nh, 1, nb),
            prm, pt[CONV_DIM:DN_AB].reshape(nh, dv, nb), bl(ngb, dv), tr(s_dn))
    specs = [pl.BlockSpec((3, 1, dk, nb), lambda h: (0, h, 0, 0)),
             pl.BlockSpec((CONV_W - 1, 3, 1, dk, nb), lambda h: (0, 0, h, 0, 0)),
             pl.BlockSpec((CONV_W, 3, 1, dk, nb), lambda h: (0, 0, h, 0, 0)),
             byh(1), byh(1), byh(2), byh(dv), fixed((dv, nb))]
    o_b, sb = _head_call(_decode_dn_kernel, "decode_dn", nh, dk, dv, nb, li, args, specs)

    nh, dk, dv = H_HG, DK_HG, DV_HG
    hk = nh * dk
    pt = pc.T
    args = (pt[0:hk].reshape(nh, dk, nb), pt[hk:2 * hk].reshape(nh, dk, nb),
            pt[2 * hk:3 * hk].reshape(nh, dv, nb), pt[3 * hk:4 * hk].reshape(nh, dv, nb),
            bl(lb, nh, dk), bl(ngc, dv), tr(s_hg))
    specs = [byh(dk), byh(dk), byh(dv), byh(dv), byh(dk), fixed((dv, nb))]
    o_c, sc = _head_call(_decode_hg_kernel, "decode_hgrn2", nh, dk, dv, nb, li, args, specs)

    return jnp.concatenate([o_a, o_b, o_c], axis=0).T.astype(BF16), sa, sb, sc


def _outproj_router_kernel(x_ref, oa_ref, ob_ref, oc_ref, w_ref, g_ref, rw_ref, rb_ref, ui_ref,
                           x1_ref, xa_ref, slot_ref, cnt_ref):
    tm, d = x_ref.shape
    na, nb_ = oa_ref.shape[1], ob_ref.shape[1]
    rs = min(tm, ROUTER_ROWS)
    blocks = [slice(r, r + rs) for r in range(0, tm, rs)]
    x1s = [x_ref[b, :] + _dot(oa_ref[b, :], w_ref[0:na, :]) + _dot(ob_ref[b, :], w_ref[na:na + nb_, :])
           + _dot(oc_ref[b, :], w_ref[na + nb_:, :]) for b in blocks]
    h2s = [_rms(x1, g_ref[...]).astype(BF16) for x1 in x1s]
    for b, x1, h2 in zip(blocks, x1s, h2s):
        x1_ref[b, :] = x1
        xa_ref[b, 0:d] = h2
    ge = 16
    rowe = lax.broadcasted_iota(jnp.int32, (N_EXPERTS, rs), 0)
    rowg = lax.broadcasted_iota(jnp.int32, (ge, rs), 0)
    is_g = rowg < N_GROUPS
    neg = jnp.float32(-jnp.inf)
    big = jnp.int32(1 << 20)
    red = dict(axis=0, keepdims=True)

    def route(logits):
        lg = jnp.where(is_g, logits[N_EXPERTS:N_EXPERTS + ge], neg)
        mg = jnp.max(lg, **red)
        pg_top = 1.0 / jnp.sum(jnp.where(is_g, jnp.exp(lg - mg), 0.0), **red)
        g_idx = jnp.min(jnp.where(lg == mg, rowg, big), **red)
        in_grp = (rowe >= g_idx * EXP_PER_GROUP) & (rowe < (g_idx + 1) * EXP_PER_GROUP)
        le = jnp.where(in_grp, logits[0:N_EXPERTS], neg)
        me = jnp.max(le, **red)
        ex = jnp.where(in_grp, jnp.exp(le - me), 0.0)
        pe = ex / jnp.sum(ex, **red)
        pe = jnp.where(in_grp, pe, -1.0)
        v1 = jnp.max(pe, **red)
        i1 = jnp.min(jnp.where(pe == v1, rowe, big), **red)
        pe2 = jnp.where(rowe == i1, -1.0, pe)
        v2 = jnp.max(pe2, **red)
        i2 = jnp.min(jnp.where(pe2 == v2, rowe, big), **red)
        tot = v1 + v2
        gate = pg_top * (jnp.where(rowe == i1, v1 / tot, 0.0) + jnp.where(rowe == i2, v2 / tot, 0.0))
        return gate, (rowg == g_idx).astype(BF16)

    routed = [route(_dot_nt(rw_ref[...], h2) + rb_ref[...]) for h2 in h2s]
    both, cnt = None, None
    for b, (gate, ind) in zip(blocks, routed):
        hi, mid, lo = _split3(gate)
        pieces = jnp.concatenate([hi.astype(F32), mid.astype(F32), lo.astype(F32), jnp.zeros_like(gate)], axis=0)
        xa_ref[b, d:d + LANE] = pieces.T.astype(BF16)
        part = _dot(ind, ui_ref[b, :])
        tot_b = jnp.sum(ind.astype(F32), axis=1, keepdims=True)
        both = part if both is None else both + part
        cnt = tot_b if cnt is None else cnt + tot_b
    slot_ref[0] = jnp.where(both[0:8, tm:2 * tm] > 0.5, both[0:8, 0:tm], -1.0)
    cnt_ref[0] = jnp.broadcast_to(cnt[0:8], (8, LANE)).astype(jnp.int32)


def _outproj_router(x, oa, ob, oc, w, g, rw, rb, tm):
    t, d = x.shape
    nt = t // tm
    idx = np.arange(tm)
    ui = jnp.asarray(np.concatenate([idx[:, None] < idx[None, :], np.eye(tm, dtype=bool)], axis=1), BF16)
    rw = rw.T
    rb = jnp.broadcast_to(rb.reshape(LANE, 1), (LANE, min(tm, ROUTER_ROWS)))
    row = lambda i: (i, 0)
    fix = lambda i: (0, 0)
    x1, xa, slot, cnt = pl.pallas_call(
        _outproj_router_kernel,
        grid=(nt,),
        in_specs=[pl.BlockSpec((tm, d), row), pl.BlockSpec((tm, oa.shape[1]), row),
                  pl.BlockSpec((tm, ob.shape[1]), row), pl.BlockSpec((tm, oc.shape[1]), row),
                  pl.BlockSpec(w.shape, fix), pl.BlockSpec((1, d), fix),
                  pl.BlockSpec(rw.shape, fix), pl.BlockSpec(rb.shape, fix), pl.BlockSpec(ui.shape, fix)],
        out_specs=[pl.BlockSpec((tm, d), row), pl.BlockSpec((tm, d + LANE), row),
                   pl.BlockSpec((1, 8, tm), lambda i: (i, 0, 0)), pl.BlockSpec((1, 8, LANE), lambda i: (i, 0, 0))],
        out_shape=[jax.ShapeDtypeStruct((t, d), F32), jax.ShapeDtypeStruct((t, d + LANE), BF16),
                   jax.ShapeDtypeStruct((nt, 8, tm), F32), jax.ShapeDtypeStruct((nt, 8, LANE), jnp.int32)],
        compiler_params=_cparams(("parallel",)),
        name="outproj_router",
    )(x, oa, ob, oc, w, g, rw, rb, ui)
    return x1, xa, slot[:, 0:N_GROUPS, None, :], cnt[:, 0:N_GROUPS, 0]


def _moe_kernel(final, rb0, rbx, cnt_ref, x1_ref, xa_ref, slot_ref, ex_ref, w1_ref, w3_ref, w2_ref, fg_ref, y_ref):
    i, g = pl.program_id(0), pl.program_id(1)
    tm, d = x1_ref.shape
    ne, _, f = w1_ref.shape

    @pl.when(g == 0)
    def _():
        y_ref[...] = x1_ref[...]

    cnt = cnt_ref[i, g]
    slot = slot_ref[0, 0]

    def rows(row0, rb):
        rid = (lax.broadcasted_iota(jnp.int32, (rb, tm), 0) + row0).astype(F32)
        sel = (rid == slot).astype(BF16)
        xg = _dot(sel, xa_ref[...])
        xb = xg[:, 0:d].astype(BF16)
        gexp = _dot(xg[:, d:d + LANE].astype(BF16), ex_ref[0])
        up = [(_dot(xb, w1_ref[e]), _dot(xb, w3_ref[e])) for e in range(ne)]
        hid = [(_silu(a) * b * gexp[:, e * f:(e + 1) * f]).astype(BF16) for e, (a, b) in enumerate(up)]
        yg = _dot(jnp.concatenate(hid, axis=-1), w2_ref[...].reshape(ne * f, d))
        y_ref[...] += _dot_tn(sel, yg.astype(BF16))

    @pl.when(cnt > 0)
    def _():
        rows(0, rb0)

    def extra(j, carry):
        rows(rb0 + j * rbx, rbx)
        return carry

    lax.fori_loop(0, (jnp.maximum(cnt - rb0, 0) + rbx - 1) // rbx, extra, 0)

    if final:
        @pl.when(g == pl.num_programs(1) - 1)
        def _():
            y_ref[...] = _rms(y_ref[...], fg_ref[...])


def _gate_expand(f):
    r = np.arange(LANE)
    e = r % N_EXPERTS
    col_e = np.arange(EXP_PER_GROUP * f) // f
    m = [(r[:, None] < 3 * N_EXPERTS) & (e[:, None] == g * EXP_PER_GROUP + col_e[None, :]) for g in range(N_GROUPS)]
    return jnp.asarray(np.stack(m), BF16)


def _moe(x1, xa, slot, cnt, w1, w3, w2, li, expand, fg, final, tm):
    t, d = x1.shape
    f = w1.shape[2]
    gf = EXP_PER_GROUP * f
    rb0 = min(tm, MOE_ROWS_FIRST)
    rbx = min(tm, MOE_ROWS_EXTRA)
    row = lambda i, g, c: (i, 0)
    grp = lambda i, g, c: (g, 0, 0)
    wgrp = lambda i, g, c: (li * N_GROUPS + g, 0, 0)
    return pl.pallas_call(
        functools.partial(_moe_kernel, final, rb0, rbx),
        grid_spec=pltpu.PrefetchScalarGridSpec(
            num_scalar_prefetch=1,
            grid=(t // tm, N_GROUPS),
            in_specs=[pl.BlockSpec((tm, d), row), pl.BlockSpec((tm, d + LANE), row),
                      pl.BlockSpec((1, 1, 1, tm), lambda i, g, c: (i, g, 0, 0)),
                      pl.BlockSpec((1, LANE, gf), grp),
                      pl.BlockSpec((EXP_PER_GROUP, d, f), wgrp), pl.BlockSpec((EXP_PER_GROUP, d, f), wgrp),
                      pl.BlockSpec((EXP_PER_GROUP, f, d), wgrp),
                      pl.BlockSpec((1, d), lambda i, g, c: (0, 0))],
            out_specs=pl.BlockSpec((tm, d), row)),
        out_shape=jax.ShapeDtypeStruct((t, d), F32),
        compiler_params=_cparams(("parallel", "arbitrary")),
        name="moe",
    )(cnt, x1, xa, slot, expand, w1, w3, w2, fg)


def kernel(x_prompt, x_sample, state_gla, state_dn, state_conv, state_hgrn, norm1_g, w_in, gla_wa2, gla_ba, gla_norm_g, dn_conv_w, dn_a_log, dn_dt_bias, dn_norm_g, hg_lb_logits, hg_norm_g, w_out, norm2_g, router_g_w, router_g_b, router_e_w, router_e_b, exp_w1, exp_w3, exp_w2, final_norm_g):
    nbp, seq, d = x_prompt.shape
    nbs = x_sample.shape[0]
    depth = w_in.shape[0]
    assert x_sample.shape[1] == 1 and seq % CHUNK == 0
    tp = nbp * seq
    xp = x_prompt.reshape(tp, d)
    xs = x_sample.reshape(nbs, d)

    sm = jax.nn.softmax(hg_lb_logits.astype(F32), axis=0)
    lb_all = jnp.maximum(jnp.cumsum(sm, axis=0) - sm[0:1], 0.0)

    tm_p = min(1024, tp)
    tm_moe = min(MOE_TILE, tp)
    expand = _gate_expand(exp_w1.shape[3])
    row2 = lambda v: v.reshape(1, -1)
    w_in_b = _perm_w_in(w_in)
    stack_e = lambda w: w.astype(BF16).reshape((w.shape[0] * w.shape[1],) + w.shape[2:])
    w1b, w3b, w2b = stack_e(exp_w1), stack_e(exp_w3), stack_e(exp_w2)
    gla_p, dn_p, conv_p, hg_p, gla_s, dn_s, conv_s, hg_s = ([] for _ in range(8))
    for li in range(depth):
        w_out_b = w_out[li].astype(BF16)
        rw = jnp.pad(jnp.concatenate([router_e_w[li], router_g_w[li]], axis=1),
                     ((0, 0), (0, LANE - N_EXPERTS - N_GROUPS))).astype(BF16)
        rb = jnp.pad(jnp.concatenate([router_e_b[li], router_g_b[li]]), (0, LANE - N_EXPERTS - N_GROUPS)).reshape(1, LANE)
        final = li == depth - 1
        g1, g2, fg = row2(norm1_g[li]), row2(norm2_g[li]), row2(final_norm_g)

        pa, pb, pc = _inproj(xp, g1, w_in_b, li, tm_p)
        oa, sa = _gla_chunk("gla", pa, nbp, seq, gla_wa2[li], row2(gla_ba[li]), row2(gla_norm_g[li]))
        ob, sb = _dn_chunk(pb, nbp, seq, dn_conv_w[li], dn_a_log[li], dn_dt_bias[li], row2(dn_norm_g[li]))
        oc, sc = _gla_chunk("hgrn2", pc, nbp, seq, row2(lb_all[li]), row2(lb_all[li]), row2(hg_norm_g[li]))
        gla_p.append(sa)
        dn_p.append(sb)
        conv_p.append(pb.reshape(nbp, seq, DN_W)[:, seq - (CONV_W - 1):, 0:CONV_DIM])
        hg_p.append(sc)
        x1, xa, slot, cnt = _outproj_router(xp, oa, ob, oc, w_out_b, g2, rw, rb, tm_moe)
        xp = _moe(x1, xa, slot, cnt, w1b, w3b, w2b, li, expand, fg, final, tm_moe)

        qa, qb, qc = _inproj(xs, g1, w_in_b, li, nbs)
        o_s, sa, sb, sc = _decode(qa, qb, qc, li, state_gla, state_dn, state_conv[li], state_hgrn,
                                  gla_wa2[li], gla_ba[li], dn_conv_w[li], dn_a_log[li], dn_dt_bias[li],
                                  lb_all[li], gla_norm_g[li], dn_norm_g[li], hg_norm_g[li])
        gla_s.append(sa)
        dn_s.append(sb)
        conv_s.append(jnp.concatenate([state_conv[li][:, 1:], qb[:, None, 0:CONV_DIM]], axis=1))
        hg_s.append(sc)
        na, nb_ = H_GLA * DV_GLA, H_DN * DV_DN
        x1, xa, slot, cnt = _outproj_router(xs, o_s[:, 0:na], o_s[:, na:na + nb_], o_s[:, na + nb_:], w_out_b, g2, rw, rb, nbs)
        xs = _moe(x1, xa, slot, cnt, w1b, w3b, w2b, li, expand, fg, final, nbs)

    st = lambda xs_, ref: jnp.stack(xs_).astype(ref.dtype)
    return (xp.reshape(nbp, seq, d), xs.reshape(nbs, 1, d),
            st(gla_p, state_gla), st(dn_p, state_dn), st(conv_p, state_conv), st(hg_p, state_hgrn),
            st(gla_s, state_gla), st(dn_s, state_dn), st(conv_s, state_conv), st(hg_s, state_hgrn))
```

```python
import functools
import math

import numpy as np
import jax
import jax.numpy as jnp
from jax import lax
from jax.experimental import pallas as pl
from jax.experimental.pallas import tpu as pltpu

F32 = jnp.float32
BF16 = jnp.bfloat16
EPS = 1e-6

H_GLA, DK_GLA, DV_GLA, GLA_RANK, GLA_TAU = 6, 32, 64, 16, 16.0
H_DN, DK_DN, DV_DN, CONV_W = 6, 64, 64, 4
CONV_DIM = H_DN * (2 * DK_DN + DV_DN)
H_HG, DK_HG, DV_HG = 4, 64, 64
N_GROUPS, EXP_PER_GROUP, TOP_K = 4, 8, 2
N_EXPERTS = N_GROUPS * EXP_PER_GROUP

LANE = 128
CHUNK = 64
GLA_CHUNKS_PER_STEP = 16
GLA_WAVE = 8
DN_CHUNKS_PER_STEP = 8
DN_WAVE = 4
MOE_TILE = 1024
MOE_ROWS_FIRST = 288
MOE_ROWS_EXTRA = 128
ROUTER_ROWS = 256
VMEM_LIMIT = 56 * 1024 * 1024

GLA_W = 1280
DN_W = 1664
HG_W = 1024
DN_AB = 1536


def _cparams(sem):
    return pltpu.CompilerParams(dimension_semantics=sem, vmem_limit_bytes=VMEM_LIMIT)


def _dot(a, b):
    return jnp.dot(a, b, preferred_element_type=F32)


def _dot_nt(a, b):
    return lax.dot_general(a, b, (((1,), (1,)), ((), ())), preferred_element_type=F32)


def _dot_tn(a, b):
    return lax.dot_general(a, b, (((0,), (0,)), ((), ())), preferred_element_type=F32)


def _split3(x):
    hi = x.astype(BF16)
    r = x - hi.astype(F32)
    mid = r.astype(BF16)
    lo = (r - mid.astype(F32)).astype(BF16)
    return hi, mid, lo


def _split2(x):
    hi = x.astype(BF16)
    lo = (x - hi.astype(F32)).astype(BF16)
    return hi, lo


def _dot_sel_l(m, x):
    hi, mid, lo = _split3(x)
    return _dot(m, hi) + _dot(m, mid) + _dot(m, lo)


def _dot_sel_r(x, m):
    hi, mid, lo = _split3(x)
    return _dot(hi, m) + _dot(mid, m) + _dot(lo, m)


def _dot_hp(a, b, fn=_dot):
    ah, al = _split2(a)
    bh, bl = _split2(b)
    return fn(ah, bh) + fn(ah, bl) + fn(al, bh)


def _rms(x, g):
    return x * lax.rsqrt(jnp.mean(x * x, axis=-1, keepdims=True) + EPS) * g


def _sigmoid(x):
    return 1.0 / (1.0 + jnp.exp(-x))


def _silu(x):
    return x * _sigmoid(x)


def _log_sigmoid(x):
    return jnp.minimum(x, 0.0) - jnp.log1p(jnp.exp(-jnp.abs(x)))


def _softplus(x):
    return jnp.maximum(x, 0.0) + jnp.log1p(jnp.exp(-jnp.abs(x)))


def _inproj_kernel(x_ref, g_ref, w_ref, oa_ref, ob_ref, oc_ref):
    h = _rms(x_ref[...], g_ref[...]).astype(BF16)
    oa_ref[...] = _dot(h, w_ref[0, :, 0:GLA_W])
    ob_ref[...] = _dot(h, w_ref[0, :, GLA_W:GLA_W + DN_W])
    oc_ref[...] = _dot(h, w_ref[0, :, GLA_W + DN_W:GLA_W + DN_W + HG_W])


def _inproj(x, g, w, li, tm):
    t, d = x.shape
    n = w.shape[2]
    row = lambda i: (i, 0)
    fix = lambda i: (0, 0)
    return pl.pallas_call(
        _inproj_kernel,
        grid=(t // tm,),
        in_specs=[pl.BlockSpec((tm, d), row), pl.BlockSpec((1, d), fix),
                  pl.BlockSpec((1, d, n), lambda i: (li, 0, 0), pipeline_mode=pl.Buffered(1))],
        out_specs=[pl.BlockSpec((tm, GLA_W), row), pl.BlockSpec((tm, DN_W), row), pl.BlockSpec((tm, HG_W), row)],
        out_shape=[jax.ShapeDtypeStruct((t, GLA_W), F32), jax.ShapeDtypeStruct((t, DN_W), F32),
                   jax.ShapeDtypeStruct((t, HG_W), F32)],
        compiler_params=_cparams(("parallel",)),
        name="inproj",
    )(x, g, w)


def _w_in_segments():
    sizes = (H_GLA * DK_GLA, H_GLA * DK_GLA, H_GLA * DV_GLA, GLA_RANK, H_GLA * DV_GLA,
             CONV_DIM, H_DN, H_DN, H_DN * DV_DN,
             H_HG * DK_HG, H_HG * DK_HG, H_HG * DV_HG, H_HG * DV_HG)
    offs = np.concatenate([[0], np.cumsum(sizes)]).tolist()
    g_q, g_k, g_v, g_a, g_g, d_qkv, d_a, d_b, d_g, h_q, h_f, h_i, h_g = [(offs[i], sizes[i]) for i in range(len(sizes))]
    return [g_q, g_k, g_v, g_g, g_a, (None, GLA_W - 1168),
            d_qkv, d_g, d_a, d_b, (None, DN_W - 1548),
            h_q, h_f, h_i, h_g]


def _perm_w_kernel(w_ref, o_ref):
    w = w_ref[0]
    cols = [jnp.zeros((w.shape[0], n), F32) if src is None else w[:, src:src + n] for src, n in _w_in_segments()]
    o_ref[0] = jnp.concatenate(cols, axis=1).astype(BF16)


def _perm_w_in(w):
    depth, d, n = w.shape
    n_out = GLA_W + DN_W + HG_W
    tr = min(256, d)
    return pl.pallas_call(
        _perm_w_kernel,
        grid=(depth, d // tr),
        in_specs=[pl.BlockSpec((1, tr, n), lambda l, r: (l, r, 0))],
        out_specs=pl.BlockSpec((1, tr, n_out), lambda l, r: (l, r, 0)),
        out_shape=jax.ShapeDtypeStruct((depth, d, n_out), BF16),
        compiler_params=_cparams(("parallel", "parallel")),
        name="perm_w_in",
    )(w)


def _decay_consts(c):
    n = int(math.log2(c))
    idx = np.arange(c)
    lm = (idx[None, :] <= idx[:, None]).astype(np.float32)
    mats, masks = [], []
    for l in range(1, n + 1):
        hs = c >> l
        bs = 2 * hs
        blk = idx // bs
        ref = blk * bs + hs - 1
        mats.append(lm - lm[ref])
        lower = (idx % bs) >= hs
        same = blk[:, None] == blk[None, :]
        masks.append((same & lower[:, None] & (~lower)[None, :]).astype(np.float32))
    mats.append(lm)
    mats.append(1.0 - lm)
    masks.append(np.eye(c, dtype=np.float32))
    masks = np.stack(masks)
    return (jnp.asarray(np.concatenate(mats, 0), BF16), jnp.asarray(np.concatenate([masks, masks], axis=1), F32), n)


def _seg_ind(nh, dv):
    h = np.arange(nh * dv) // dv
    return jnp.asarray(h[:, None] == h[None, :], BF16)


def _seg_sum(x, ind):
    return _dot(x.astype(BF16), ind)


def _pair_geometry(nh, dk):
    geo = []
    for p in range(nh // 2):
        start = (2 * p * dk) // LANE * LANE
        width = min(LANE, nh * dk - start)
        geo.append((start, width, (2 * p * dk - start, (2 * p + 1) * dk - start)))
    return geo


def _gla_chunk_kernel(mode, nh, dk, dv, nlev, nch, slab_ref, mats_ref, masks_ref, ind_ref, p1_ref, p2_ref, ng_ref,
                      o_ref, st_ref, s_scr):
    c = CHUNK
    ci = pl.program_id(1)

    @pl.when(ci == 0)
    def _():
        s_scr[...] = jnp.zeros_like(s_scr)

    hk, hv = nh * dk, nh * dv
    if mode == "gla":
        q = slab_ref[:, 0:hk] * (dk ** -0.5)
        k = slab_ref[:, hk:2 * hk]
        v = slab_ref[:, 2 * hk:2 * hk + hv]
        gate = slab_ref[:, 2 * hk + hv:2 * hk + 2 * hv]
        a_lr = slab_ref[:, 2 * hk + 2 * hv:2 * hk + 2 * hv + GLA_RANK]
        z = _dot_hp(a_lr, p1_ref[...]) + p2_ref[...]
        la = _log_sigmoid(z) * (1.0 / GLA_TAU)
    else:
        q = _silu(slab_ref[:, 0:hk])
        x = slab_ref[:, hk:2 * hk]
        v = slab_ref[:, 2 * hk:2 * hk + hv]
        gate = slab_ref[:, 2 * hk + hv:2 * hk + 2 * hv]
        lb = p1_ref[...]
        la = _log_sigmoid(x) + jnp.log1p(lb * jnp.exp(-x))
        k = (1.0 - lb) * _sigmoid(-x)

    bf = lambda t: t.astype(BF16)
    geo = _pair_geometry(nh, dk)
    npair = len(geo)
    chunks = range(nch)
    rows = [slice(i * c, (i + 1) * c) for i in chunks]
    vb = bf(v)

    qs, ks, q_in, k_rem, e_last = {}, {}, {}, {}, {}

    def prepare(wave):
        for i in wave:
            dall = _dot_sel_l(mats_ref[...], la[rows[i]])
            bcum = dall[nlev * c:(nlev + 1) * c]
            wl = [jnp.exp(-jnp.abs(dall[l * c:(l + 1) * c])) for l in range(nlev)]
            qc, kc = q[rows[i]], k[rows[i]]
            qs[i] = [bf(qc * w) for w in wl] + [bf(qc)]
            ks[i] = [bf(kc * w) for w in wl] + [bf(kc)]
            q_in[i] = bf(qc * jnp.exp(bcum))
            k_rem[i] = bf(kc * jnp.exp(dall[(nlev + 1) * c:(nlev + 2) * c]))
            e_last[i] = jnp.exp(bcum[c - 1:c])
            yield

    lane_v = lax.broadcasted_iota(jnp.int32, (c, LANE), 1)
    first = lane_v < dv
    pair_consts = []
    for p in range(npair):
        start, width, offs = geo[p]
        lane = lax.broadcasted_iota(jnp.int32, (c, width), 1)
        ri = lax.broadcasted_iota(jnp.int32, (2 * dv, width), 0)
        li = lax.broadcasted_iota(jnp.int32, (2 * dv, width), 1)
        smask = ((ri < dv) & (li >= offs[0]) & (li < offs[0] + dk)) | ((ri >= dv) & (li >= offs[1]) & (li < offs[1] + dk))
        pair_consts.append((slice(start, start + width), slice(p * LANE, (p + 1) * LANE),
                            [(lane >= o) & (lane < o + dk) for o in offs], jnp.zeros((c, width), BF16), smask))
    carried = [s_scr[p, :, 0:geo[p][1]] for p in range(npair)]
    pieces = [[] for _ in range(npair)]

    def attend(wave):
        for p in range(npair):
            blk, vcol, hmask, zero, smask = pair_consts[p]
            atts = {}
            for i in wave:
                att = None
                for l in range(nlev + 1):
                    qb = qs[i][l][:, blk]
                    lhs = jnp.concatenate([jnp.where(hmask[0], qb, zero), jnp.where(hmask[1], qb, zero)], axis=0)
                    term = _dot_nt(lhs, ks[i][l][:, blk]) * masks_ref[l]
                    att = term if att is None else att + term
                atts[i] = bf(att)
                yield
            intra, upd, states = {}, {}, {}
            for i in wave:
                r2 = _dot(atts[i], vb[rows[i], vcol])
                intra[i] = jnp.where(first, r2[0:c], r2[c:2 * c])
                upd[i] = jnp.where(smask, _dot_tn(vb[rows[i], vcol], k_rem[i][:, blk]), 0.0)
            yield
            s = carried[p]
            for i in wave:
                states[i] = bf(s)
                s = s * e_last[i][:, blk] + upd[i]
            carried[p] = s
            for i in wave:
                pieces[p].append(intra[i] + _dot_nt(q_in[i][:, blk], states[i]))
            yield

    waves = [list(range(w0, min(w0 + GLA_WAVE, nch))) for w0 in range(0, nch, GLA_WAVE)]
    pending = iter(())
    for wave in waves:
        for _ in prepare(wave):
            next(pending, None)
            next(pending, None)
        for _ in pending:
            pass
        pending = attend(wave)
    for _ in pending:
        pass
    for p in range(npair):
        s_scr[p, :, 0:geo[p][1]] = carried[p]
    o = jnp.concatenate([jnp.concatenate(pc, axis=0) for pc in pieces], axis=-1)

    if mode == "gla":
        ms = _seg_sum(o * o, ind_ref[...]) * (1.0 / dv)
        o = o * lax.rsqrt(ms + EPS) * ng_ref[...] * _silu(gate)
    else:
        o = o * _sigmoid(gate)
        ms = _seg_sum(o * o, ind_ref[...]) * (1.0 / dv)
        o = o * lax.rsqrt(ms + EPS) * ng_ref[...]
    o_ref[...] = o.astype(o_ref.dtype)

    @pl.when(ci == pl.num_programs(1) - 1)
    def _():
        st_ref[0] = s_scr[...]


def _gla_chunk(mode, slab, nb, seq, p1, p2, ng):
    nh, dk, dv = (H_GLA, DK_GLA, DV_GLA) if mode == "gla" else (H_HG, DK_HG, DV_HG)
    w = slab.shape[1]
    nch = math.gcd(seq // CHUNK, GLA_CHUNKS_PER_STEP)
    sc = nch * CHUNK
    ns = seq // sc
    mats, masks, nlev = _decay_consts(CHUNK)
    ind = _seg_ind(nh, dv)
    ngt = jnp.tile(ng, (1, nh))
    geo = _pair_geometry(nh, dk)
    fix2 = lambda b, c: (0, 0)
    kern = functools.partial(_gla_chunk_kernel, mode, nh, dk, dv, nlev, nch)
    o, st = pl.pallas_call(
        kern,
        grid=(nb, ns),
        in_specs=[pl.BlockSpec((sc, w), lambda b, c: (b * ns + c, 0)),
                  pl.BlockSpec(mats.shape, fix2),
                  pl.BlockSpec(masks.shape, lambda b, c: (0, 0, 0)),
                  pl.BlockSpec(ind.shape, fix2),
                  pl.BlockSpec(p1.shape, fix2), pl.BlockSpec(p2.shape, fix2), pl.BlockSpec(ngt.shape, fix2)],
        out_specs=[pl.BlockSpec((sc, nh * dv), lambda b, c: (b * ns + c, 0)),
                   pl.BlockSpec((1, len(geo), 2 * dv, LANE), lambda b, c: (b, 0, 0, 0))],
        out_shape=[jax.ShapeDtypeStruct((nb * seq, nh * dv), BF16),
                   jax.ShapeDtypeStruct((nb, len(geo), 2 * dv, LANE), F32)],
        scratch_shapes=[pltpu.VMEM((len(geo), 2 * dv, LANE), F32)],
        compiler_params=_cparams(("parallel", "arbitrary")),
        name="chunk_" + mode,
    )(slab, mats, masks, ind, p1, p2, ngt)
    heads = []
    for p, (_, _, offs) in enumerate(geo):
        for j in range(2):
            heads.append(jnp.swapaxes(st[:, p, j * dv:(j + 1) * dv, offs[j]:offs[j] + dk], -1, -2))
    return o, jnp.stack(heads, axis=1)


def _dn_chunk_kernel(nch, slab_ref, abr_ref, lm_ref, umb_ref, ind_ref, cw_ref, pcol_ref, prow_ref, ng_ref,
                     o_ref, st_ref, s_scr, ext_scr):
    c = CHUNK
    sc = nch * c
    nh, dk, dv = H_DN, DK_DN, DV_DN
    npair = nh // 2
    ci = pl.program_id(1)

    @pl.when(ci == 0)
    def _():
        s_scr[...] = jnp.zeros_like(s_scr)
        ext_scr[0:8, :] = jnp.zeros((8, CONV_DIM), F32)

    ext_scr[8:8 + sc, :] = slab_ref[:, 0:CONV_DIM]
    conv = ext_scr[5:5 + sc, :] * cw_ref[0:1, :]
    for i in range(1, CONV_W):
        conv = conv + ext_scr[5 + i:5 + i + sc, :] * cw_ref[i:i + 1, :]
    ext_scr[0:8, :] = ext_scr[sc:sc + 8, :]
    qkv = _silu(conv)
    gate = slab_ref[:, CONV_DIM:DN_AB]
    hk = nh * dk
    q = qkv[:, 0:hk]
    k = qkv[:, hk:2 * hk]
    v = qkv[:, 2 * hk:2 * hk + nh * dv]
    q = q * lax.rsqrt(_seg_sum(q * q, ind_ref[...]) + EPS) * (dk ** -0.5)
    k = k * lax.rsqrt(_seg_sum(k * k, ind_ref[...]) + EPS)

    ab_c = slab_ref[:, DN_AB:DN_AB + LANE]
    g_c = -jnp.exp(pcol_ref[0:1, :]) * _softplus(ab_c + pcol_ref[1:2, :])
    beta_c = _sigmoid(ab_c)
    g_r = -jnp.exp(prow_ref[0]) * _softplus(abr_ref[0, 0] + prow_ref[1])
    gcum_r = _dot_sel_r(g_r, umb_ref[...])

    bf = lambda t: t.astype(BF16)
    rows = [slice(i * c, (i + 1) * c) for i in range(nch)]
    ri = lax.broadcasted_iota(jnp.int32, (2 * c, 2 * c), 0)
    cj = lax.broadcasted_iota(jnp.int32, (2 * c, 2 * c), 1)
    same = (ri < c) == (cj < c)
    tri = same & (ri >= cj)
    strict = same & (ri > cj)
    eye = (ri == cj).astype(F32)
    lane = lax.broadcasted_iota(jnp.int32, (c, LANE), 1)
    first = lane < dk
    zero = jnp.zeros((c, LANE), BF16)
    gcum_cs = [_dot_sel_l(lm_ref[...], g_c[rows[i]]) for i in range(nch)]
    egc_cs = [jnp.exp(gcc) for gcc in gcum_cs]
    ekd_cs = [jnp.exp(gcc[c - 1:c] - gcc) for gcc in gcum_cs]
    egl_cs = [jnp.exp(gcc[c - 1:c]) for gcc in gcum_cs]

    def lanes2(col0, col1):
        return jnp.where(first, jnp.broadcast_to(col0, (c, LANE)), jnp.broadcast_to(col1, (c, LANE)))

    def stack2(col0, col1):
        return jnp.concatenate([jnp.broadcast_to(col0, (c, LANE)), jnp.broadcast_to(col1, (c, LANE))], axis=0)

    def rows2(x):
        return jnp.concatenate([jnp.where(first, x, zero), jnp.where(first, zero, x)], axis=0)

    pairs = range(npair)

    def prepare(i, p):
        h0, h1 = 2 * p, 2 * p + 1
        blk = slice(p * LANE, (p + 1) * LANE)
        gcc = gcum_cs[i]
        gc0, gc1 = gcc[:, h0:h0 + 1], gcc[:, h1:h1 + 1]
        b0, b1 = beta_c[rows[i], nh + h0:nh + h0 + 1], beta_c[rows[i], nh + h1:nh + h1 + 1]
        gcr = gcum_r[p:p + 1, i * LANE:(i + 1) * LANE]
        gam = jnp.where(tri, jnp.exp(jnp.where(tri, stack2(gc0, gc1) - gcr, 0.0)), 0.0)
        kc, qc, vc = k[rows[i], blk], q[rows[i], blk], v[rows[i], blk]
        beta_l = lanes2(b0, b1)
        e_c, e_d, e_l = egc_cs[i], ekd_cs[i], egl_cs[i]
        egc_l = lanes2(e_c[:, h0:h0 + 1], e_c[:, h1:h1 + 1])
        kb = kc * beta_l
        return dict(
            gam=gam, k2=rows2(bf(kc)), kb2=rows2(bf(kb)), q2=rows2(bf(qc)), vb2=rows2(bf(vc * beta_l)),
            ke2=rows2(bf(kb * egc_l)), qe=bf(qc * egc_l),
            kd2=rows2(bf(kc * lanes2(e_d[:, h0:h0 + 1], e_d[:, h1:h1 + 1]))),
            eglast=stack2(e_l[:, h0:h0 + 1], e_l[:, h1:h1 + 1]))

    def independent(chunk_ids, pre):
        units = [(i, p) for i in chunk_ids for p in pairs]
        for u in units:
            pre[u] = prepare(*u)
        yield
        ms = {u: jnp.where(strict, _dot_nt(pre[u]["kb2"], pre[u]["k2"]) * pre[u]["gam"], 0.0) for u in units}
        tinv = {u: eye - ms[u] for u in units}
        pw = {u: bf(ms[u]) for u in units}
        yield
        for _ in range(int(math.log2(c)) - 1):
            pw = {u: bf(_dot(pw[u], pw[u])) for u in units}
            yield
            tinv = {u: tinv[u] + _dot(bf(tinv[u]), pw[u]) for u in units}
            yield
        for u in units:
            tb = bf(tinv[u])
            pre[u]["uu"] = _dot(tb, pre[u]["vb2"])
            pre[u]["ww"] = bf(_dot(tb, pre[u]["ke2"]))
            pre[u]["att"] = bf(_dot_nt(pre[u]["q2"], pre[u]["k2"]) * pre[u]["gam"])
        yield

    states = [s_scr[p] for p in pairs]
    pieces = [[] for _ in pairs]

    def recur(chunk_ids, pre):
        for i in chunk_ids:
            sbs = [bf(states[p]) for p in pairs]
            vns = [bf(pre[(i, p)]["uu"] - _dot(pre[(i, p)]["ww"], sbs[p])) for p in pairs]
            yield
            for p in pairs:
                u = pre[(i, p)]
                o2 = _dot(u["att"], vns[p])
                pieces[p].append(_dot(u["qe"], sbs[p]) + o2[0:c] + o2[c:2 * c])
                states[p] = states[p] * u["eglast"] + _dot_tn(u["kd2"], vns[p])
            yield

    waves = [list(range(w, min(w + DN_WAVE, nch))) for w in range(0, nch, DN_WAVE)]
    pre, pending = {}, iter(())
    for wave in waves:
        for _ in independent(wave, pre):
            next(pending, None)
        for _ in pending:
            pass
        pending = recur(wave, pre)
    for _ in pending:
        pass
    for p in pairs:
        s_scr[p] = states[p]
    o = jnp.concatenate([jnp.concatenate(pc, axis=0) for pc in pieces], axis=-1)
    ms_o = _seg_sum(o * o, ind_ref[...]) * (1.0 / dv)
    o_ref[...] = (o * lax.rsqrt(ms_o + EPS) * ng_ref[...] * _silu(gate)).astype(o_ref.dtype)

    @pl.when(ci == pl.num_programs(1) - 1)
    def _():
        st_ref[0] = s_scr[...]


def _dn_chunk(slab, nb, seq, conv_w, a_log, dt_bias, ng):
    c = CHUNK
    nch = math.gcd(seq // c, DN_CHUNKS_PER_STEP)
    sc = nch * c
    ns = seq // sc
    npair = H_DN // 2
    nrow = 16
    idx = np.arange(c)
    lm = jnp.asarray((idx[None, :] <= idx[:, None]), BF16)
    um = (idx[:, None] <= idx[None, :]).astype(np.float32)
    umb = jnp.asarray(np.kron(np.eye(2 * nch, dtype=np.float32), um), BF16)
    ind = _seg_ind(H_DN, DK_DN)
    cw = jnp.pad(conv_w, ((0, 8 - CONV_W), (0, 0)))
    pad = lambda p: jnp.pad(p, (0, LANE - H_DN))
    pcol = jnp.zeros((8, LANE), F32).at[0].set(pad(a_log)).at[1].set(pad(dt_bias))
    a_cols = slab[:, DN_AB:DN_AB + H_DN].reshape(nb, ns, nch, c, npair, 2)
    a_rows = jnp.transpose(a_cols, (0, 1, 4, 2, 5, 3)).reshape(nb, ns, npair, nch * 2 * c)
    a_rows = jnp.pad(a_rows, ((0, 0), (0, 0), (0, nrow - npair), (0, 0)))
    rowp = lambda p: jnp.pad(jnp.broadcast_to(p.reshape(npair, 1, 2, 1), (npair, nch, 2, c)).reshape(npair, nch * 2 * c),
                             ((0, nrow - npair), (0, 0)))
    prow = jnp.stack([rowp(a_log), rowp(dt_bias)])
    ngt = jnp.tile(ng, (1, H_DN))
    fix2 = lambda b, c_: (0, 0)
    o, st = pl.pallas_call(
        functools.partial(_dn_chunk_kernel, nch),
        grid=(nb, ns),
        in_specs=[pl.BlockSpec((sc, DN_W), lambda b, c_: (b * ns + c_, 0)),
                  pl.BlockSpec((1, 1, nrow, nch * 2 * c), lambda b, c_: (b, c_, 0, 0)),
                  pl.BlockSpec((c, c), fix2), pl.BlockSpec(umb.shape, fix2), pl.BlockSpec(ind.shape, fix2),
                  pl.BlockSpec((8, CONV_DIM), fix2), pl.BlockSpec((8, LANE), fix2),
                  pl.BlockSpec(prow.shape, lambda b, c_: (0, 0, 0)), pl.BlockSpec(ngt.shape, fix2)],
        out_specs=[pl.BlockSpec((sc, H_DN * DV_DN), lambda b, c_: (b * ns + c_, 0)),
                   pl.BlockSpec((1, npair, 2 * DK_DN, 2 * DV_DN), lambda b, c_: (b, 0, 0, 0))],
        out_shape=[jax.ShapeDtypeStruct((nb * seq, H_DN * DV_DN), BF16),
                   jax.ShapeDtypeStruct((nb, npair, 2 * DK_DN, 2 * DV_DN), F32)],
        scratch_shapes=[pltpu.VMEM((npair, 2 * DK_DN, 2 * DV_DN), F32), pltpu.VMEM((sc + 8, CONV_DIM), F32)],
        compiler_params=_cparams(("parallel", "arbitrary")),
        name="chunk_dn",
    )(slab, a_rows, lm, umb, ind, cw, pcol, prow, ngt)
    heads = [st[:, p, j * DK_DN:(j + 1) * DK_DN, j * DV_DN:(j + 1) * DV_DN] for p in range(npair) for j in range(2)]
    return o, jnp.stack(heads, axis=1)


def _state_in(s_ref, dk):
    return s_ref[0]


def _state_out(so_ref, new):
    for d, s_new in enumerate(new):
        so_ref[0, d] = s_new


def _decode_gla_kernel(q_ref, k_ref, v_ref, g_ref, alr_ref, wa2t_ref, ba_ref, ng_ref, s_ref, o_ref, so_ref):
    dk = q_ref.shape[1]
    z = _dot_hp(wa2t_ref[0], alr_ref[...]) + ba_ref[0]
    dec = jnp.exp(_log_sigmoid(z) * (1.0 / GLA_TAU))
    q = q_ref[0] * (dk ** -0.5)
    k = k_ref[0]
    v = v_ref[0]
    st = _state_in(s_ref, dk)
    acc = jnp.zeros_like(v)
    new = []
    for d in range(dk):
        s_new = st[d] * dec[d:d + 1, :] + k[d:d + 1, :] * v
        new.append(s_new)
        acc = acc + q[d:d + 1, :] * s_new
    _state_out(so_ref, new)
    ms = jnp.mean(acc * acc, axis=0, keepdims=True)
    o_ref[0] = acc * lax.rsqrt(ms + EPS) * ng_ref[...] * _silu(g_ref[0])


def _decode_dn_kernel(x_ref, cb_ref, cw_ref, a_ref, b_ref, p_ref, g_ref, ng_ref, s_ref, o_ref, so_ref):
    dk = x_ref.shape[2]
    conv = x_ref[:, 0] * cw_ref[CONV_W - 1, :, 0]
    for i in range(CONV_W - 1):
        conv = conv + cb_ref[i, :, 0] * cw_ref[i, :, 0]
    qkv = _silu(conv)
    q, k, v = qkv[0], qkv[1], qkv[2]
    q = q * lax.rsqrt(jnp.sum(q * q, axis=0, keepdims=True) + EPS) * (dk ** -0.5)
    k = k * lax.rsqrt(jnp.sum(k * k, axis=0, keepdims=True) + EPS)
    eg = jnp.exp(-jnp.exp(p_ref[0, 0:1, :]) * _softplus(a_ref[0] + p_ref[0, 1:2, :]))
    beta = _sigmoid(b_ref[0])
    st = _state_in(s_ref, dk)
    ks = jnp.zeros_like(v)
    for d in range(dk):
        ks = ks + k[d:d + 1, :] * st[d]
    v_new = beta * (v - eg * ks)
    acc = jnp.zeros_like(v)
    new = []
    for d in range(dk):
        s_new = st[d] * eg + k[d:d + 1, :] * v_new
        new.append(s_new)
        acc = acc + q[d:d + 1, :] * s_new
    _state_out(so_ref, new)
    ms = jnp.mean(acc * acc, axis=0, keepdims=True)
    o_ref[0] = acc * lax.rsqrt(ms + EPS) * ng_ref[...] * _silu(g_ref[0])


def _decode_hg_kernel(q_ref, f_ref, v_ref, g_ref, lb_ref, ng_ref, s_ref, o_ref, so_ref):
    dk = q_ref.shape[1]
    x = f_ref[0]
    lb = lb_ref[0]
    f = jnp.exp(_log_sigmoid(x) + jnp.log1p(lb * jnp.exp(-x)))
    k = (1.0 - lb) * _sigmoid(-x)
    q = _silu(q_ref[0])
    v = v_ref[0]
    st = _state_in(s_ref, dk)
    acc = jnp.zeros_like(v)
    new = []
    for d in range(dk):
        s_new = st[d] * f[d:d + 1, :] + k[d:d + 1, :] * v
        new.append(s_new)
        acc = acc + q[d:d + 1, :] * s_new
    _state_out(so_ref, new)
    acc = acc * _sigmoid(g_ref[0])
    ms = jnp.mean(acc * acc, axis=0, keepdims=True)
    o_ref[0] = acc * lax.rsqrt(ms + EPS) * ng_ref[...]


def _head_call(kern, name, nh, dk, dv, nb, li, args, specs):
    o, s = pl.pallas_call(
        kern,
        grid=(nh,),
        in_specs=specs + [pl.BlockSpec((1, dk, dv, nb), lambda h: (li * nh + h, 0, 0, 0))],
        out_specs=[pl.BlockSpec((1, dv, nb), lambda h: (h, 0, 0)),
                   pl.BlockSpec((1, dk, dv, nb), lambda h: (h, 0, 0, 0))],
        out_shape=[jax.ShapeDtypeStruct((nh, dv, nb), F32), jax.ShapeDtypeStruct((nh, dk, dv, nb), F32)],
        compiler_params=_cparams(("parallel",)),
        name=name,
    )(*args)
    return o.reshape(nh * dv, nb), jnp.transpose(s, (3, 0, 1, 2))


def _decode(pa, pb, pc, li, s_gla, s_dn, s_conv, s_hg, wa2, ba, conv_w, a_log, dt_bias, lb, nga, ngb, ngc):
    nb = pa.shape[0]
    bl = lambda p, *shape: jnp.broadcast_to(p.reshape(shape + (1,)), shape + (nb,))
    tr = lambda s: jnp.transpose(s, (0, 2, 3, 4, 1)).reshape((s.shape[0] * s.shape[2],) + s.shape[3:] + (nb,))
    byh = lambda n: pl.BlockSpec((1, n, nb), lambda h: (h, 0, 0))
    fixed = lambda shape: pl.BlockSpec(shape, lambda h: (0,) * len(shape))

    nh, dk, dv = H_GLA, DK_GLA, DV_GLA
    hk, hv = nh * dk, nh * dv
    pt = pa.T
    args = (pt[0:hk].reshape(nh, dk, nb), pt[hk:2 * hk].reshape(nh, dk, nb),
            pt[2 * hk:2 * hk + hv].reshape(nh, dv, nb), pt[2 * hk + hv:2 * hk + 2 * hv].reshape(nh, dv, nb),
            pt[2 * hk + 2 * hv:2 * hk + 2 * hv + GLA_RANK], wa2.T.reshape(nh, dk, GLA_RANK),
            bl(ba, nh, dk), bl(nga, dv), tr(s_gla))
    specs = [byh(dk), byh(dk), byh(dv), byh(dv), fixed((GLA_RANK, nb)),
             pl.BlockSpec((1, dk, GLA_RANK), lambda h: (h, 0, 0)), byh(dk), fixed((dv, nb))]
    o_a, sa = _head_call(_decode_gla_kernel, "decode_gla", nh, dk, dv, nb, li, args, specs)

    nh, dk, dv = H_DN, DK_DN, DV_DN
    pt = pb.T
    x = pt[0:CONV_DIM].reshape(3, nh, dk, nb)
    cb = jnp.transpose(s_conv, (1, 2, 0)).reshape(CONV_W - 1, 3, nh, dk, nb)
    cw = bl(conv_w, CONV_W, 3, nh, dk)
    prm = jnp.stack([bl(a_log, nh), bl(dt_bias, nh)], axis=1)
    args = (x, cb, cw, pt[DN_AB:DN_AB + nh].reshape(nh, 1, nb), pt[DN_AB + nh:DN_AB + 2 * nh].reshape(nh, 1, nb),
            prm, pt[CONV_DIM:DN_AB].reshape(nh, dv, nb), bl(ngb, dv), tr(s_dn))
    specs = [pl.BlockSpec((3, 1, dk, nb), lambda h: (0, h, 0, 0)),
             pl.BlockSpec((CONV_W - 1, 3, 1, dk, nb), lambda h: (0, 0, h, 0, 0)),
             pl.BlockSpec((CONV_W, 3, 1, dk, nb), lambda h: (0, 0, h, 0, 0)),
             byh(1), byh(1), byh(2), byh(dv), fixed((dv, nb))]
    o_b, sb = _head_call(_decode_dn_kernel, "decode_dn", nh, dk, dv, nb, li, args, specs)

    nh, dk, dv = H_HG, DK_HG, DV_HG
    hk = nh * dk
    pt = pc.T
    args = (pt[0:hk].reshape(nh, dk, nb), pt[hk:2 * hk].reshape(nh, dk, nb),
            pt[2 * hk:3 * hk].reshape(nh, dv, nb), pt[3 * hk:4 * hk].reshape(nh, dv, nb),
            bl(lb, nh, dk), bl(ngc, dv), tr(s_hg))
    specs = [byh(dk), byh(dk), byh(dv), byh(dv), byh(dk), fixed((dv, nb))]
    o_c, sc = _head_call(_decode_hg_kernel, "decode_hgrn2", nh, dk, dv, nb, li, args, specs)

    return jnp.concatenate([o_a, o_b, o_c], axis=0).T.astype(BF16), sa, sb, sc


def _outproj_router_kernel(x_ref, oa_ref, ob_ref, oc_ref, w_ref, g_ref, rw_ref, rb_ref, ui_ref,
                           x1_ref, xa_ref, slot_ref, cnt_ref):
    tm, d = x_ref.shape
    na, nb_ = oa_ref.shape[1], ob_ref.shape[1]
    rs = min(tm, ROUTER_ROWS)
    blocks = [slice(r, r + rs) for r in range(0, tm, rs)]
    x1s = [x_ref[b, :] + _dot(oa_ref[b, :], w_ref[0:na, :]) + _dot(ob_ref[b, :], w_ref[na:na + nb_, :])
           + _dot(oc_ref[b, :], w_ref[na + nb_:, :]) for b in blocks]
    h2s = [_rms(x1, g_ref[...]).astype(BF16) for x1 in x1s]
    for b, x1, h2 in zip(blocks, x1s, h2s):
        x1_ref[b, :] = x1
        xa_ref[b, 0:d] = h2
    ge = 16
    rowe = lax.broadcasted_iota(jnp.int32, (N_EXPERTS, rs), 0)
    rowg = lax.broadcasted_iota(jnp.int32, (ge, rs), 0)
    is_g = rowg < N_GROUPS
    neg = jnp.float32(-jnp.inf)
    big = jnp.int32(1 << 20)
    red = dict(axis=0, keepdims=True)

    def route(logits):
        lg = jnp.where(is_g, logits[N_EXPERTS:N_EXPERTS + ge], neg)
        mg = jnp.max(lg, **red)
        pg_top = 1.0 / jnp.sum(jnp.where(is_g, jnp.exp(lg - mg), 0.0), **red)
        g_idx = jnp.min(jnp.where(lg == mg, rowg, big), **red)
        in_grp = (rowe >= g_idx * EXP_PER_GROUP) & (rowe < (g_idx + 1) * EXP_PER_GROUP)
        le = jnp.where(in_grp, logits[0:N_EXPERTS], neg)
        me = jnp.max(le, **red)
        ex = jnp.where(in_grp, jnp.exp(le - me), 0.0)
        pe = ex / jnp.sum(ex, **red)
        pe = jnp.where(in_grp, pe, -1.0)
        v1 = jnp.max(pe, **red)
        i1 = jnp.min(jnp.where(pe == v1, rowe, big), **red)
        pe2 = jnp.where(rowe == i1, -1.0, pe)
        v2 = jnp.max(pe2, **red)
        i2 = jnp.min(jnp.where(pe2 == v2, rowe, big), **red)
        tot = v1 + v2
        gate = pg_top * (jnp.where(rowe == i1, v1 / tot, 0.0) + jnp.where(rowe == i2, v2 / tot, 0.0))
        return gate, (rowg == g_idx).astype(BF16)

    routed = [route(_dot_nt(rw_ref[...], h2) + rb_ref[...]) for h2 in h2s]
    both, cnt = None, None
    for b, (gate, ind) in zip(blocks, routed):
        hi, mid, lo = _split3(gate)
        pieces = jnp.concatenate([hi.astype(F32), mid.astype(F32), lo.astype(F32), jnp.zeros_like(gate)], axis=0)
        xa_ref[b, d:d + LANE] = pieces.T.astype(BF16)
        part = _dot(ind, ui_ref[b, :])
        tot_b = jnp.sum(ind.astype(F32), axis=1, keepdims=True)
        both = part if both is None else both + part
        cnt = tot_b if cnt is None else cnt + tot_b
    slot_ref[0] = jnp.where(both[0:8, tm:2 * tm] > 0.5, both[0:8, 0:tm], -1.0)
    cnt_ref[0] = jnp.broadcast_to(cnt[0:8], (8, LANE)).astype(jnp.int32)


def _outproj_router(x, oa, ob, oc, w, g, rw, rb, tm):
    t, d = x.shape
    nt = t // tm
    idx = np.arange(tm)
    ui = jnp.asarray(np.concatenate([idx[:, None] < idx[None, :], np.eye(tm, dtype=bool)], axis=1), BF16)
    rw = rw.T
    rb = jnp.broadcast_to(rb.reshape(LANE, 1), (LANE, min(tm, ROUTER_ROWS)))
    row = lambda i: (i, 0)
    fix = lambda i: (0, 0)
    x1, xa, slot, cnt = pl.pallas_call(
        _outproj_router_kernel,
        grid=(nt,),
        in_specs=[pl.BlockSpec((tm, d), row), pl.BlockSpec((tm, oa.shape[1]), row),
                  pl.BlockSpec((tm, ob.shape[1]), row), pl.BlockSpec((tm, oc.shape[1]), row),
                  pl.BlockSpec(w.shape, fix), pl.BlockSpec((1, d), fix),
                  pl.BlockSpec(rw.shape, fix), pl.BlockSpec(rb.shape, fix), pl.BlockSpec(ui.shape, fix)],
        out_specs=[pl.BlockSpec((tm, d), row), pl.BlockSpec((tm, d + LANE), row),
                   pl.BlockSpec((1, 8, tm), lambda i: (i, 0, 0)), pl.BlockSpec((1, 8, LANE), lambda i: (i, 0, 0))],
        out_shape=[jax.ShapeDtypeStruct((t, d), F32), jax.ShapeDtypeStruct((t, d + LANE), BF16),
                   jax.ShapeDtypeStruct((nt, 8, tm), F32), jax.ShapeDtypeStruct((nt, 8, LANE), jnp.int32)],
        compiler_params=_cparams(("parallel",)),
        name="outproj_router",
    )(x, oa, ob, oc, w, g, rw, rb, ui)
    return x1, xa, slot[:, 0:N_GROUPS, None, :], cnt[:, 0:N_GROUPS, 0]


def _moe_kernel(final, rb0, rbx, cnt_ref, x1_ref, xa_ref, slot_ref, ex_ref, w1_ref, w3_ref, w2_ref, fg_ref, y_ref):
    i, g = pl.program_id(0), pl.program_id(1)
    tm, d = x1_ref.shape
    ne, _, f = w1_ref.shape

    @pl.when(g == 0)
    def _():
        y_ref[...] = x1_ref[...]

    cnt = cnt_ref[i, g]
    slot = slot_ref[0, 0]

    def rows(row0, rb):
        rid = (lax.broadcasted_iota(jnp.int32, (rb, tm), 0) + row0).astype(F32)
        sel = (rid == slot).astype(BF16)
        xg = _dot(sel, xa_ref[...])
        xb = xg[:, 0:d].astype(BF16)
        gexp = _dot(xg[:, d:d + LANE].astype(BF16), ex_ref[0])
        up = [(_dot(xb, w1_ref[e]), _dot(xb, w3_ref[e])) for e in range(ne)]
        hid = [(_silu(a) * b * gexp[:, e * f:(e + 1) * f]).astype(BF16) for e, (a, b) in enumerate(up)]
        yg = _dot(jnp.concatenate(hid, axis=-1), w2_ref[...].reshape(ne * f, d))
        y_ref[...] += _dot_tn(sel, yg.astype(BF16))

    @pl.when(cnt > 0)
    def _():
        rows(0, rb0)

    def extra(j, carry):
        rows(rb0 + j * rbx, rbx)
        return carry

    lax.fori_loop(0, (jnp.maximum(cnt - rb0, 0) + rbx - 1) // rbx, extra, 0)

    if final:
        @pl.when(g == pl.num_programs(1) - 1)
        def _():
            y_ref[...] = _rms(y_ref[...], fg_ref[...])


def _gate_expand(f):
    r = np.arange(LANE)
    e = r % N_EXPERTS
    col_e = np.arange(EXP_PER_GROUP * f) // f
    m = [(r[:, None] < 3 * N_EXPERTS) & (e[:, None] == g * EXP_PER_GROUP + col_e[None, :]) for g in range(N_GROUPS)]
    return jnp.asarray(np.stack(m), BF16)


def _moe(x1, xa, slot, cnt, w1, w3, w2, li, expand, fg, final, tm):
    t, d = x1.shape
    f = w1.shape[2]
    gf = EXP_PER_GROUP * f
    rb0 = min(tm, MOE_ROWS_FIRST)
    rbx = min(tm, MOE_ROWS_EXTRA)
    row = lambda i, g, c: (i, 0)
    grp = lambda i, g, c: (g, 0, 0)
    wgrp = lambda i, g, c: (li * N_GROUPS + g, 0, 0)
    return pl.pallas_call(
        functools.partial(_moe_kernel, final, rb0, rbx),
        grid_spec=pltpu.PrefetchScalarGridSpec(
            num_scalar_prefetch=1,
            grid=(t // tm, N_GROUPS),
            in_specs=[pl.BlockSpec((tm, d), row), pl.BlockSpec((tm, d + LANE), row),
                      pl.BlockSpec((1, 1, 1, tm), lambda i, g, c: (i, g, 0, 0)),
                      pl.BlockSpec((1, LANE, gf), grp),
                      pl.BlockSpec((EXP_PER_GROUP, d, f), wgrp), pl.BlockSpec((EXP_PER_GROUP, d, f), wgrp),
                      pl.BlockSpec((EXP_PER_GROUP, f, d), wgrp),
                      pl.BlockSpec((1, d), lambda i, g, c: (0, 0))],
            out_specs=pl.BlockSpec((tm, d), row)),
        out_shape=jax.ShapeDtypeStruct((t, d), F32),
        compiler_params=_cparams(("parallel", "arbitrary")),
        name="moe",
    )(cnt, x1, xa, slot, expand, w1, w3, w2, fg)


def kernel(x_prompt, x_sample, state_gla, state_dn, state_conv, state_hgrn, norm1_g, w_in, gla_wa2, gla_ba, gla_norm_g, dn_conv_w, dn_a_log, dn_dt_bias, dn_norm_g, hg_lb_logits, hg_norm_g, w_out, norm2_g, router_g_w, router_g_b, router_e_w, router_e_b, exp_w1, exp_w3, exp_w2, final_norm_g):
    nbp, seq, d = x_prompt.shape
    nbs = x_sample.shape[0]
    depth = w_in.shape[0]
    assert x_sample.shape[1] == 1 and seq % CHUNK == 0
    tp = nbp * seq
    xp = x_prompt.reshape(tp, d)
    xs = x_sample.reshape(nbs, d)

    sm = jax.nn.softmax(hg_lb_logits.astype(F32), axis=0)
    lb_all = jnp.maximum(jnp.cumsum(sm, axis=0) - sm[0:1], 0.0)

    tm_p = min(1024, tp)
    tm_moe = min(MOE_TILE, tp)
    expand = _gate_expand(exp_w1.shape[3])
    row2 = lambda v: v.reshape(1, -1)
    w_in_b = _perm_w_in(w_in)
    stack_e = lambda w: w.astype(BF16).reshape((w.shape[0] * w.shape[1],) + w.shape[2:])
    w1b, w3b, w2b = stack_e(exp_w1), stack_e(exp_w3), stack_e(exp_w2)
    gla_p, dn_p, conv_p, hg_p, gla_s, dn_s, conv_s, hg_s = ([] for _ in range(8))
    for li in range(depth):
        w_out_b = w_out[li].astype(BF16)
        rw = jnp.pad(jnp.concatenate([router_e_w[li], router_g_w[li]], axis=1),
                     ((0, 0), (0, LANE - N_EXPERTS - N_GROUPS))).astype(BF16)
        rb = jnp.pad(jnp.concatenate([router_e_b[li], router_g_b[li]]), (0, LANE - N_EXPERTS - N_GROUPS)).reshape(1, LANE)
        final = li == depth - 1
        g1, g2, fg = row2(norm1_g[li]), row2(norm2_g[li]), row2(final_norm_g)

        pa, pb, pc = _inproj(xp, g1, w_in_b, li, tm_p)
        oa, sa = _gla_chunk("gla", pa, nbp, seq, gla_wa2[li], row2(gla_ba[li]), row2(gla_norm_g[li]))
        ob, sb = _dn_chunk(pb, nbp, seq, dn_conv_w[li], dn_a_log[li], dn_dt_bias[li], row2(dn_norm_g[li]))
        oc, sc = _gla_chunk("hgrn2", pc, nbp, seq, row2(lb_all[li]), row2(lb_all[li]), row2(hg_norm_g[li]))
        gla_p.append(sa)
        dn_p.append(sb)
        conv_p.append(pb.reshape(nbp, seq, DN_W)[:, seq - (CONV_W - 1):, 0:CONV_DIM])
        hg_p.append(sc)
        x1, xa, slot, cnt = _outproj_router(xp, oa, ob, oc, w_out_b, g2, rw, rb, tm_moe)
        xp = _moe(x1, xa, slot, cnt, w1b, w3b, w2b, li, expand, fg, final, tm_moe)

        qa, qb, qc = _inproj(xs, g1, w_in_b, li, nbs)
        o_s, sa, sb, sc = _decode(qa, qb, qc, li, state_gla, state_dn, state_conv[li], state_hgrn,
                                  gla_wa2[li], gla_ba[li], dn_conv_w[li], dn_a_log[li], dn_dt_bias[li],
                                  lb_all[li], gla_norm_g[li], dn_norm_g[li], hg_norm_g[li])
        gla_s.append(sa)
        dn_s.append(sb)
        conv_s.append(jnp.concatenate([state_conv[li][:, 1:], qb[:, None, 0:CONV_DIM]], axis=1))
        hg_s.append(sc)
        na, nb_ = H_GLA * DV_GLA, H_DN * DV_DN
        x1, xa, slot, cnt = _outproj_router(xs, o_s[:, 0:na], o_s[:, na:na + nb_], o_s[:, na + nb_:], w_out_b, g2, rw, rb, nbs)
        xs = _moe(x1, xa, slot, cnt, w1b, w3b, w2b, li, expand, fg, final, nbs)

    st = lambda xs_, ref: jnp.stack(xs_).astype(ref.dtype)
    return (xp.reshape(nbp, seq, d), xs.reshape(nbs, 1, d),
            st(gla_p, state_gla), st(dn_p, state_dn), st(conv_p, state_conv), st(hg_p, state_hgrn),
            st(gla_s, state_gla), st(dn_s, state_dn), st(conv_s, state_conv), st(hg_s, state_hgrn))
```
